```python
import jax, jax.numpy as jnp
from jax import lax
import numpy as np

D_MODEL = 1024
BATCH = 2
SEQ = 8192
DEPTH = 4
DEC_BATCH = 4
DEC_SEQ = 4096
PAST_LEN = 128

N_MIXERS = 3
N_A = (DEPTH + 2) // 3
N_B = (DEPTH + 1) // 3
N_C = DEPTH // 3
NORM_EPS = 1e-6

GM_WIDTH = 2 * D_MODEL
GM_GROUPS = 8
GM_CHUNK = 128
SC_WIDTH = D_MODEL
SC_KW = 3
MLA_HEADS = 8
MLA_Q_RANK = 384
MLA_KV_RANK = 256
MLA_NOPE = 128
MLA_ROPE = 64
MLA_V = 128
MLA_QBLOCK = 128
ROPE_THETA = 10000.0
N_EXPERTS = 32
TOP_K = 4
D_EXPERT = D_MODEL
SWIGLU_LIMIT = 7.0
SWIGLU_ALPHA = 1.702
MOE_BLOCK = 256

kernel_name = "hybrid_gmlp_shortconv_mla_moe_encoder"


def rmsnorm(x, g):
    xf = x.astype(jnp.float32)
    y = xf * lax.rsqrt(jnp.mean(xf * xf, axis=-1, keepdims=True) + NORM_EPS)
    return y.astype(x.dtype) * g


def gmlp_mixer(h, w_in, g_v, w_s, b_s, w_out):
    b, s, _ = h.shape
    uv = jax.nn.gelu(h @ w_in)
    u, v = uv[..., :GM_WIDTH], uv[..., GM_WIDTH:]
    v = rmsnorm(v, g_v)
    vb = v.reshape(b, s // GM_CHUNK, GM_CHUNK, GM_GROUPS, GM_WIDTH // GM_GROUPS)
    vm = jnp.einsum('gpq,bnqgc->bnpgc', w_s, vb) + b_s.T[None, None, :, :, None]
    return (u * vm.reshape(b, s, GM_WIDTH)) @ w_out


def shortconv_mixer(h, w_in, w_conv, w_out):
    bcx = h @ w_in
    bg = bcx[..., :SC_WIDTH]
    cg = bcx[..., SC_WIDTH:2 * SC_WIDTH]
    xin = bcx[..., 2 * SC_WIDTH:]
    z = jnp.pad(cg * xin, ((0, 0), (1, 1), (0, 0)))
    y = w_conv[0] * z[:, :-2] + w_conv[1] * z[:, 1:-1] + w_conv[2] * z[:, 2:]
    return (bg * y) @ w_out


def rope_tables(s, dtype):
    inv = 1.0 / (ROPE_THETA ** (jnp.arange(0, MLA_ROPE, 2, dtype=jnp.float32) / MLA_ROPE))
    ang = jnp.arange(s, dtype=jnp.float32)[:, None] * inv[None, :]
    return (jnp.cos(ang)[None, :, None, :].astype(dtype),
            jnp.sin(ang)[None, :, None, :].astype(dtype))


def rotate(x, cos, sin):
    half = MLA_ROPE // 2
    x1, x2 = x[..., :half], x[..., half:]
    return jnp.concatenate([x1 * cos - x2 * sin, x1 * sin + x2 * cos], axis=-1)


def mla_mixer(h, w_down, g_q_lat, g_kv_lat, w_uq, w_ukv, g_qh, g_kh, w_o):
    b, s, _ = h.shape
    dk = MLA_NOPE + MLA_ROPE
    lat = h @ w_down
    q_lat = rmsnorm(lat[..., :MLA_Q_RANK], g_q_lat)
    kv_lat = rmsnorm(lat[..., MLA_Q_RANK:MLA_Q_RANK + MLA_KV_RANK], g_kv_lat)
    k_pe = lat[..., MLA_Q_RANK + MLA_KV_RANK:]
    q = (q_lat @ w_uq).reshape(b, s, MLA_HEADS, dk)
    kv = (kv_lat @ w_ukv).reshape(b, s, MLA_HEADS, MLA_NOPE + MLA_V)
    k = jnp.concatenate([kv[..., :MLA_NOPE],
                         jnp.broadcast_to(k_pe[:, :, None, :], (b, s, MLA_HEADS, MLA_ROPE))], axis=-1)
    v = kv[..., MLA_NOPE:]
    q = rmsnorm(q, g_qh)
    k = rmsnorm(k, g_kh)
    cos, sin = rope_tables(s, q.dtype)
    q = jnp.concatenate([q[..., :MLA_NOPE], rotate(q[..., MLA_NOPE:], cos, sin)], axis=-1)
    k = jnp.concatenate([k[..., :MLA_NOPE], rotate(k[..., MLA_NOPE:], cos, sin)], axis=-1)
    scale = dk ** -0.5
    nq = s // MLA_QBLOCK
    qb = q.reshape(b, nq, MLA_QBLOCK, MLA_HEADS, dk).transpose(1, 0, 2, 3, 4)

    def attend(qblk):
        sc = jnp.einsum('bqhd,bkhd->bhqk', qblk, k, preferred_element_type=jnp.float32) * scale
        p = jax.nn.softmax(sc, axis=-1).astype(v.dtype)
        return jnp.einsum('bhqk,bkhd->bqhd', p, v)

    o = lax.map(attend, qb)
    o = o.transpose(1, 0, 2, 3, 4).reshape(b, s, MLA_HEADS * MLA_V)
    return o @ w_o


def moe_ffn(h, w_r, b_r, w_gu, b_gu, w_dn, b_dn):
    b, s, d = h.shape
    t = b * s
    xt = h.reshape(t, d)
    logits = (xt @ w_r).astype(jnp.float32) + b_r.astype(jnp.float32)
    top_v, top_i = lax.top_k(logits, TOP_K)
    gates = jax.nn.softmax(top_v, axis=-1).astype(h.dtype)
    flat_e = top_i.reshape(-1).astype(jnp.int32)
    order = jnp.argsort(flat_e)
    se = flat_e[order]
    stok = (order // TOP_K).astype(jnp.int32)
    sgate = gates.reshape(-1)[order]
    counts = jnp.bincount(flat_e, length=N_EXPERTS).astype(jnp.int32)
    padded = (counts + MOE_BLOCK - 1) // MOE_BLOCK * MOE_BLOCK
    pad_end = jnp.cumsum(padded)
    pad_start = pad_end - padded
    start = jnp.cumsum(counts) - counts
    dest = pad_start[se] + jnp.arange(t * TOP_K, dtype=jnp.int32) - start[se]
    nb = -(-(t * TOP_K) // MOE_BLOCK) + N_EXPERTS
    rows = nb * MOE_BLOCK
    row_tok = jnp.full((rows,), t, jnp.int32).at[dest].set(stok)
    row_gate = jnp.zeros((rows,), h.dtype).at[dest].set(sgate)
    blk_e = jnp.minimum(jnp.searchsorted(pad_end, jnp.arange(nb, dtype=jnp.int32) * MOE_BLOCK,
                                         side='right'), N_EXPERTS - 1)
    xpad = jnp.concatenate([xt, jnp.zeros((1, d), xt.dtype)], axis=0)
    xb = xpad[row_tok].reshape(nb, MOE_BLOCK, d)

    def expert_block(args):
        xblk, e = args
        gu = xblk @ w_gu[e] + b_gu[e]
        g, u = gu[..., :D_EXPERT], gu[..., D_EXPERT:]
        g = jnp.minimum(g, SWIGLU_LIMIT)
        u = jnp.clip(u, -SWIGLU_LIMIT, SWIGLU_LIMIT)
        a = (u + 1) * (g * jax.nn.sigmoid(SWIGLU_ALPHA * g))
        return a @ w_dn[e] + b_dn[e]

    yb = lax.map(expert_block, (xb, blk_e)).reshape(rows, d)
    y = jnp.zeros((t + 1, d), h.dtype).at[row_tok].add(yb * row_gate[:, None])
    return y[:t].reshape(b, s, d)


def trunk(x, c, g_mix, g_ffn, w_ada, b_ada,
          gm_w_in, gm_g_v, gm_w_s, gm_b_s, gm_w_out,
          sc_w_in, sc_w_conv, sc_w_out,
          mla_w_down, mla_g_q_lat, mla_g_kv_lat, mla_w_uq, mla_w_ukv, mla_g_qh, mla_g_kh, mla_w_o,
          moe_w_router, moe_b_router, moe_w_gu, moe_b_gu, moe_w_dn, moe_b_dn):
    sc_c = jax.nn.silu(c)
    for i in range(DEPTH):
        mod = (sc_c @ w_ada[i] + b_ada[i])[:, None, :]
        sh1, sc1, ga1, sh2, sc2, ga2 = jnp.split(mod, 6, axis=-1)
        h = rmsnorm(x, g_mix[i]) * (1 + sc1) + sh1
        kind, j = i % N_MIXERS, i // N_MIXERS
        if kind == 0:
            out = gmlp_mixer(h, gm_w_in[j], gm_g_v[j], gm_w_s[j], gm_b_s[j], gm_w_out[j])
        elif kind == 1:
            out = shortconv_mixer(h, sc_w_in[j], sc_w_conv[j], sc_w_out[j])
        else:
            out = mla_mixer(h, mla_w_down[j], mla_g_q_lat[j], mla_g_kv_lat[j], mla_w_uq[j],
                            mla_w_ukv[j], mla_g_qh[j], mla_g_kh[j], mla_w_o[j])
        x = x + ga1 * out
        h = rmsnorm(x, g_ffn[i]) * (1 + sc2) + sh2
        x = x + ga2 * moe_ffn(h, moe_w_router[i], moe_b_router[i], moe_w_gu[i], moe_b_gu[i],
                              moe_w_dn[i], moe_b_dn[i])
    return x


def setup_inputs(seed: int = 0) -> dict:
    key = jax.random.key(seed)
    ks = iter(jax.random.split(key, 40))

    def nrm(shape, scale):
        return jax.random.normal(next(ks), shape, jnp.float32) * scale

    def gain(shape):
        return 1.0 + nrm(shape, 0.1)

    D = D_MODEL
    dk = MLA_NOPE + MLA_ROPE
    return {
        "x_prompt": nrm((BATCH, SEQ, D), 1.0),
        "x_sample": nrm((DEC_BATCH, DEC_SEQ, D), 1.0),
        "c_prompt": nrm((BATCH, D), 1.0),
        "c_sample": nrm((DEC_BATCH, D), 1.0),
        "g_mix": gain((DEPTH, D)),
        "g_ffn": gain((DEPTH, D)),
        "w_ada": nrm((DEPTH, D, 6 * D), 0.5 * D ** -0.5),
        "b_ada": nrm((DEPTH, 6 * D), 0.02),
        "gm_w_in": nrm((N_A, D, 2 * GM_WIDTH), D ** -0.5),
        "gm_g_v": gain((N_A, GM_WIDTH)),
        "gm_w_s": nrm((N_A, GM_GROUPS, GM_CHUNK, GM_CHUNK), GM_CHUNK ** -0.5),
        "gm_b_s": gain((N_A, GM_GROUPS, GM_CHUNK)),
        "gm_w_out": nrm((N_A, GM_WIDTH, D), GM_WIDTH ** -0.5),
        "sc_w_in": nrm((N_B, D, 3 * SC_WIDTH), D ** -0.5),
        "sc_w_conv": nrm((N_B, SC_KW, SC_WIDTH), SC_KW ** -0.5),
        "sc_w_out": nrm((N_B, SC_WIDTH, D), SC_WIDTH ** -0.5),
        "mla_w_down": nrm((N_C, D, MLA_Q_RANK + MLA_KV_RANK + MLA_ROPE), D ** -0.5),
        "mla_g_q_lat": gain((N_C, MLA_Q_RANK)),
        "mla_g_kv_lat": gain((N_C, MLA_KV_RANK)),
        "mla_w_uq": nrm((N_C, MLA_Q_RANK, MLA_HEADS * dk), MLA_Q_RANK ** -0.5),
        "mla_w_ukv": nrm((N_C, MLA_KV_RANK, MLA_HEADS * (MLA_NOPE + MLA_V)), MLA_KV_RANK ** -0.5),
        "mla_g_qh": gain((N_C, dk)),
        "mla_g_kh": gain((N_C, dk)),
        "mla_w_o": nrm((N_C, MLA_HEADS * MLA_V, D), (MLA_HEADS * MLA_V) ** -0.5),
        "moe_w_router": nrm((DEPTH, D, N_EXPERTS), D ** -0.5),
        "moe_b_router": nrm((DEPTH, N_EXPERTS), 0.01),
        "moe_w_gu": nrm((DEPTH, N_EXPERTS, D, 2 * D_EXPERT), D ** -0.5),
        "moe_b_gu": nrm((DEPTH, N_EXPERTS, 2 * D_EXPERT), 0.02),
        "moe_w_dn": nrm((DEPTH, N_EXPERTS, D_EXPERT, D), D_EXPERT ** -0.5),
        "moe_b_dn": nrm((DEPTH, N_EXPERTS, D), 0.02),
    }


def reference(x_prompt, x_sample, c_prompt, c_sample, g_mix, g_ffn, w_ada, b_ada,
              gm_w_in, gm_g_v, gm_w_s, gm_b_s, gm_w_out,
              sc_w_in, sc_w_conv, sc_w_out,
              mla_w_down, mla_g_q_lat, mla_g_kv_lat, mla_w_uq, mla_w_ukv, mla_g_qh, mla_g_kh, mla_w_o,
              moe_w_router, moe_b_router, moe_w_gu, moe_b_gu, moe_w_dn, moe_b_dn):
    weights = (g_mix, g_ffn, w_ada, b_ada,
               gm_w_in, gm_g_v, gm_w_s, gm_b_s, gm_w_out,
               sc_w_in, sc_w_conv, sc_w_out,
               mla_w_down, mla_g_q_lat, mla_g_kv_lat, mla_w_uq, mla_w_ukv, mla_g_qh, mla_g_kh, mla_w_o,
               moe_w_router, moe_b_router, moe_w_gu, moe_b_gu, moe_w_dn, moe_b_dn)
    y_prompt = trunk(x_prompt, c_prompt, *weights)
    y_sample = trunk(x_sample, c_sample, *weights)
    return (y_prompt, y_sample)
```

```python
import functools

import jax
import jax.numpy as jnp
from jax import lax
from jax.experimental import pallas as pl
from jax.experimental.pallas import tpu as pltpu

F32 = jnp.float32
BF16 = jnp.bfloat16

NORM_EPS = 1e-6
GM_GROUPS = 8
GM_CHUNK = 128
MLA_HEADS = 8
MLA_Q_RANK = 384
MLA_KV_RANK = 256
MLA_NOPE = 128
MLA_ROPE = 64
MLA_V = 128
MLA_DK = MLA_NOPE + MLA_ROPE
ROPE_THETA = 10000.0
N_EXPERTS = 32
TOP_K = 4
SWIGLU_LIMIT = 7.0
SWIGLU_ALPHA = 1.702

LANES = 128
HALO = 16
VMEM_LIMIT = 56 * 1024 * 1024

TM_GMLP = 256
TM_CONV = 512
TM_MLA = 256
TM_ROUTER = 512
TM_OUT = 512
TQ_ATTN = 256
TK_ATTN = 512
MOE_ROWS = 256
GATHER_ROWS = 256


def _params(*sem):
    return pltpu.CompilerParams(dimension_semantics=sem, vmem_limit_bytes=VMEM_LIMIT)


def _const_spec(shape):
    nd = len(shape)
    return pl.BlockSpec(shape, lambda *_: (0,) * nd)


def _norm_mod(x, g, shift, scale):
    ms = jnp.mean(x * x, axis=-1, keepdims=True)
    return (x * lax.rsqrt(ms + NORM_EPS)) * g * (1.0 + scale) + shift


def _ada_kernel(c_ref, w_ref, b_ref, o_ref):
    c = c_ref[...]
    a = (c * jax.nn.sigmoid(c)).astype(BF16)
    w = w_ref[0].astype(BF16)
    o_ref[0] = jnp.dot(a, w, preferred_element_type=F32) + b_ref[0]


def _ada_call(c_pad, w_ada, b_ada):
    depth, d, n = w_ada.shape
    tn = 1536
    rows = c_pad.shape[0]
    return pl.pallas_call(
        _ada_kernel,
        grid=(depth, n // tn),
        in_specs=[
            pl.BlockSpec((rows, d), lambda l, j: (0, 0)),
            pl.BlockSpec((1, d, tn), lambda l, j: (l, 0, j)),
            pl.BlockSpec((1, 1, tn), lambda l, j: (l, 0, j)),
        ],
        out_specs=pl.BlockSpec((1, rows, tn), lambda l, j: (l, 0, j)),
        out_shape=jax.ShapeDtypeStruct((depth, rows, n), F32),
        compiler_params=_params("arbitrary", "arbitrary"),
        name="adaln_mod",
    )(c_pad, w_ada, b_ada.reshape(depth, 1, n))


def _gmlp_kernel(x_ref, mod_ref, g_ref, win_ref, gv_ref, ws_ref, bs_ref, wout_ref,
                 o_ref, gated_ref):
    x = x_ref[...]
    tm = x.shape[0]
    width = gv_ref.shape[1]
    gw = width // GM_GROUPS
    h = _norm_mod(x, g_ref[...], mod_ref[0, 0:1, :], mod_ref[0, 1:2, :]).astype(BF16)
    uv = jax.nn.gelu(jnp.dot(h, win_ref[...], preferred_element_type=F32))
    u = uv[:, :width]
    v = uv[:, width:]
    v = v * lax.rsqrt(jnp.mean(v * v, axis=-1, keepdims=True) + NORM_EPS) * gv_ref[...]
    vb = v.astype(BF16)
    bs = bs_ref[...]
    for c in range(tm // GM_CHUNK):
        r0 = c * GM_CHUNK
        for g in range(GM_GROUPS):
            c0 = g * gw
            vm = jnp.dot(ws_ref[g], vb[r0:r0 + GM_CHUNK, c0:c0 + gw],
                         preferred_element_type=F32) + bs[:, g:g + 1]
            gated_ref[r0:r0 + GM_CHUNK, c0:c0 + gw] = (
                u[r0:r0 + GM_CHUNK, c0:c0 + gw] * vm).astype(BF16)
    out = jnp.dot(gated_ref[...], wout_ref[...], preferred_element_type=F32)
    o_ref[...] = x + mod_ref[0, 2:3, :] * out


def _gmlp_call(x, mod, g, w_in, g_v, w_s, b_s, w_out, seg):
    t, d = x.shape
    tm = TM_GMLP
    width = g_v.shape[-1]
    return pl.pallas_call(
        _gmlp_kernel,
        grid=(t // tm,),
        in_specs=[
            pl.BlockSpec((tm, d), lambda i: (i, 0)),
            pl.BlockSpec((1, 6, d), lambda i: ((i * tm) // seg, 0, 0)),
            _const_spec((1, d)),
            _const_spec(w_in.shape),
            _const_spec((1, width)),
            _const_spec(w_s.shape),
            _const_spec((GM_CHUNK, GM_GROUPS)),
            _const_spec(w_out.shape),
        ],
        out_specs=pl.BlockSpec((tm, d), lambda i: (i, 0)),
        out_shape=jax.ShapeDtypeStruct((t, d), F32),
        scratch_shapes=[pltpu.VMEM((tm, width), BF16)],
        compiler_params=_params("parallel"),
        name="gmlp_mixer",
    )(x, mod, g.reshape(1, d), w_in.astype(BF16), g_v.reshape(1, width),
      w_s.astype(BF16), b_s.T, w_out.astype(BF16))


def _conv_kernel(start_ref, end_ref, x_ref, xp_ref, xn_ref, mod_ref, g_ref, win_ref,
                 wc_ref, wout_ref, o_ref, h_ref, z_ref):
    i = pl.program_id(0)
    x = x_ref[...]
    tm, d = x.shape
    g = g_ref[...]
    sh = mod_ref[0, 0:1, :]
    sc = mod_ref[0, 1:2, :]
    h_ref[0:HALO, :] = _norm_mod(xp_ref[...], g, sh, sc).astype(BF16)
    h_ref[HALO:HALO + tm, :] = _norm_mod(x, g, sh, sc).astype(BF16)
    h_ref[HALO + tm:, :] = _norm_mod(xn_ref[...], g, sh, sc).astype(BF16)
    bcx = jnp.dot(h_ref[...], win_ref[...], preferred_element_type=F32)
    w = bcx.shape[1] // 3
    bg = bcx[HALO:HALO + tm, :w]
    z_ref[...] = bcx[:, w:2 * w] * bcx[:, 2 * w:]
    keep_prev = (start_ref[i] == 0).astype(F32)
    keep_next = (end_ref[i] == 0).astype(F32)
    z_ref[0:HALO, :] = z_ref[0:HALO, :] * keep_prev
    z_ref[HALO + tm:, :] = z_ref[HALO + tm:, :] * keep_next
    y = (wc_ref[0:1, :] * z_ref[HALO - 1:HALO - 1 + tm, :]
         + wc_ref[1:2, :] * z_ref[HALO:HALO + tm, :]
         + wc_ref[2:3, :] * z_ref[HALO + 1:HALO + 1 + tm, :])
    out = jnp.dot((bg * y).astype(BF16), wout_ref[...], preferred_element_type=F32)
    o_ref[...] = x + mod_ref[0, 2:3, :] * out


def _conv_call(x, mod, g, w_in, w_conv, w_out, seg, tile_start, tile_end):
    t, d = x.shape
    tm = TM_CONV
    w = w_conv.shape[-1]
    hb = tm // HALO
    last = t // HALO - 1
    grid_spec = pltpu.PrefetchScalarGridSpec(
        num_scalar_prefetch=2,
        grid=(t // tm,),
        in_specs=[
            pl.BlockSpec((tm, d), lambda i, *_: (i, 0)),
            pl.BlockSpec((HALO, d), lambda i, *_: (jnp.maximum(i * hb - 1, 0), 0)),
            pl.BlockSpec((HALO, d), lambda i, *_: (jnp.minimum((i + 1) * hb, last), 0)),
            pl.BlockSpec((1, 6, d), lambda i, *_: ((i * tm) // seg, 0, 0)),
            _const_spec((1, d)),
            _const_spec(w_in.shape),
            _const_spec(w_conv.shape),
            _const_spec(w_out.shape),
        ],
        out_specs=pl.BlockSpec((tm, d), lambda i, *_: (i, 0)),
        scratch_shapes=[pltpu.VMEM((tm + 2 * HALO, d), BF16),
                        pltpu.VMEM((tm + 2 * HALO, w), F32)],
    )
    return pl.pallas_call(
        _conv_kernel,
        grid_spec=grid_spec,
        out_shape=jax.ShapeDtypeStruct((t, d), F32),
        compiler_params=_params("parallel"),
        name="shortconv_mixer",
    )(tile_start, tile_end, x, x, x, mod, g.reshape(1, d), w_in.astype(BF16),
      w_conv, w_out.astype(BF16))


QH = 3 * LANES
KH = 2 * LANES


def _mla_pre_kernel(pos_ref, x_ref, mod_ref, g_ref, wd_ref, gq_ref, gkv_ref, wuq_ref,
                    wukv_ref, gh_ref, cos_ref, sin_ref, q_ref, k_ref, v_ref):
    del pos_ref
    x = x_ref[...]
    h = _norm_mod(x, g_ref[...], mod_ref[0, 0:1, :], mod_ref[0, 1:2, :]).astype(BF16)
    lat = jnp.dot(h, wd_ref[...], preferred_element_type=F32)
    ql = lat[:, :MLA_Q_RANK]
    kvl = lat[:, MLA_Q_RANK:MLA_Q_RANK + MLA_KV_RANK]
    pe = lat[:, MLA_Q_RANK + MLA_KV_RANK:MLA_Q_RANK + MLA_KV_RANK + LANES]
    pe_sw = lat[:, MLA_Q_RANK + MLA_KV_RANK + LANES:]
    ql = ql * lax.rsqrt(jnp.mean(ql * ql, axis=-1, keepdims=True) + NORM_EPS) * gq_ref[...]
    kvl = kvl * lax.rsqrt(jnp.mean(kvl * kvl, axis=-1, keepdims=True) + NORM_EPS) * gkv_ref[...]
    q = jnp.dot(ql.astype(BF16), wuq_ref[...], preferred_element_type=F32)
    kv = jnp.dot(kvl.astype(BF16), wukv_ref[...], preferred_element_type=F32)
    cos = cos_ref[...]
    sin = sin_ref[...]
    gqn, gqr, gqs = gh_ref[0:1, :], gh_ref[1:2, :], gh_ref[2:3, :]
    gkn, gkr, gks = gh_ref[3:4, :], gh_ref[4:5, :], gh_ref[5:6, :]
    pe_ss = jnp.sum(pe * pe, axis=-1, keepdims=True)
    qscale = MLA_DK ** -0.5
    for hd in range(MLA_HEADS):
        qn = q[:, hd * QH:hd * QH + LANES]
        qr = q[:, hd * QH + LANES:hd * QH + 2 * LANES]
        qs = q[:, hd * QH + 2 * LANES:(hd + 1) * QH]
        ss = jnp.sum(qn * qn, axis=-1, keepdims=True) + jnp.sum(qr * qr, axis=-1, keepdims=True)
        rinv = lax.rsqrt(ss * (1.0 / MLA_DK) + NORM_EPS) * qscale
        q_ref[hd, :, 0:LANES] = (qn * rinv * gqn).astype(BF16)
        q_ref[hd, :, LANES:] = ((qr * rinv * gqr) * cos + (qs * rinv * gqs) * sin).astype(BF16)
        kn = kv[:, hd * 2 * LANES:hd * 2 * LANES + LANES]
        ss = jnp.sum(kn * kn, axis=-1, keepdims=True) + pe_ss
        rinv = lax.rsqrt(ss * (1.0 / MLA_DK) + NORM_EPS)
        k_ref[hd, :, 0:LANES] = (kn * rinv * gkn).astype(BF16)
        k_ref[hd, :, LANES:] = ((pe * rinv * gkr) * cos + (pe_sw * rinv * gks) * sin).astype(BF16)
        v_ref[hd] = kv[:, hd * 2 * LANES + LANES:(hd + 1) * 2 * LANES].astype(BF16)


def _mla_prepare_weights(w_down, w_uq, w_ukv, g_qh, g_kh):
    half = MLA_ROPE // 2
    pad = LANES - MLA_ROPE

    def swap(a):
        return jnp.concatenate([a[..., half:], a[..., :half]], axis=-1)

    def pad_lanes(a):
        return jnp.pad(a, [(0, 0)] * (a.ndim - 1) + [(0, pad)])

    d = w_down.shape[0]
    pe_w = w_down[:, MLA_Q_RANK + MLA_KV_RANK:]
    wd = jnp.concatenate([w_down[:, :MLA_Q_RANK + MLA_KV_RANK],
                          pad_lanes(pe_w), pad_lanes(swap(pe_w))], axis=1)
    uq = w_uq.reshape(MLA_Q_RANK, MLA_HEADS, MLA_DK)
    uq_r = uq[..., MLA_NOPE:]
    uq = jnp.concatenate([uq[..., :MLA_NOPE], pad_lanes(uq_r), pad_lanes(swap(uq_r))], axis=-1)
    uq = uq.reshape(MLA_Q_RANK, MLA_HEADS * QH)

    def gains(gv):
        r = gv[MLA_NOPE:]
        return [gv[:MLA_NOPE], pad_lanes(r), pad_lanes(swap(r))]

    gh = jnp.stack(gains(g_qh) + gains(g_kh) + [jnp.zeros((LANES,), F32)] * 2)
    return wd.astype(BF16), uq.astype(BF16), w_ukv.astype(BF16), gh


def _rope_tables(s):
    half = MLA_ROPE // 2
    inv = 1.0 / (ROPE_THETA ** (jnp.arange(0, MLA_ROPE, 2, dtype=F32) / MLA_ROPE))
    ang = jnp.arange(s, dtype=F32)[:, None] * inv[None, :]
    cos, sin = jnp.cos(ang), jnp.sin(ang)
    zeros = jnp.zeros((s, LANES - MLA_ROPE), F32)
    del half
    return (jnp.concatenate([cos, cos, zeros], axis=1),
            jnp.concatenate([-sin, sin, zeros], axis=1))


def _mla_pre_call(x, mod, g, wd, g_q_lat, g_kv_lat, uq, ukv, gh, cos, sin, seg, tile_pos):
    t, d = x.shape
    tm = TM_MLA
    grid_spec = pltpu.PrefetchScalarGridSpec(
        num_scalar_prefetch=1,
        grid=(t // tm,),
        in_specs=[
            pl.BlockSpec((tm, d), lambda i, p: (i, 0)),
            pl.BlockSpec((1, 6, d), lambda i, p: ((i * tm) // seg, 0, 0)),
            _const_spec((1, d)),
            _const_spec(wd.shape),
            _const_spec((1, MLA_Q_RANK)),
            _const_spec((1, MLA_KV_RANK)),
            _const_spec(uq.shape),
            _const_spec(ukv.shape),
            _const_spec(gh.shape),
            pl.BlockSpec((tm, LANES), lambda i, p: (p[i], 0)),
            pl.BlockSpec((tm, LANES), lambda i, p: (p[i], 0)),
        ],
        out_specs=[
            pl.BlockSpec((MLA_HEADS, tm, KH), lambda i, p: (0, i, 0)),
            pl.BlockSpec((MLA_HEADS, tm, KH), lambda i, p: (0, i, 0)),
            pl.BlockSpec((MLA_HEADS, tm, MLA_V), lambda i, p: (0, i, 0)),
        ],
    )
    return pl.pallas_call(
        _mla_pre_kernel,
        grid_spec=grid_spec,
        out_shape=[jax.ShapeDtypeStruct((MLA_HEADS, t, KH), BF16),
                   jax.ShapeDtypeStruct((MLA_HEADS, t, KH), BF16),
                   jax.ShapeDtypeStruct((MLA_HEADS, t, MLA_V), BF16)],
        compiler_params=_params("parallel"),
        name="mla_qkv",
    )(tile_pos, x, mod, g.reshape(1, d), wd, g_q_lat.reshape(1, -1), g_kv_lat.reshape(1, -1),
      uq, ukv, gh, cos, sin)


def _attn_kernel(q_ref, k_ref, v_ref, o_ref):
    q = q_ref[0]
    tq = q.shape[0]
    s_len = k_ref.shape[1]

    def body(j, carry):
        m, l, acc = carry
        off = pl.multiple_of(j * TK_ATTN, TK_ATTN)
        k = k_ref[0, pl.ds(off, TK_ATTN), :]
        v = v_ref[0, pl.ds(off, TK_ATTN), :]
        s = lax.dot_general(q, k, (((1,), (1,)), ((), ())), preferred_element_type=F32)
        m_new = jnp.maximum(m, jnp.max(s, axis=-1, keepdims=True))
        alpha = jnp.exp(m - m_new)
        p = jnp.exp(s - m_new)
        l = alpha * l + jnp.sum(p, axis=-1, keepdims=True)
        acc = alpha * acc + jnp.dot(p.astype(BF16), v, preferred_element_type=F32)
        return m_new, l, acc

    m0 = jnp.full((tq, 1), -jnp.inf, F32)
    l0 = jnp.zeros((tq, 1), F32)
    a0 = jnp.zeros((tq, MLA_V), F32)
    _, l, acc = lax.fori_loop(0, s_len // TK_ATTN, body, (m0, l0, a0))
    o_ref[...] = (acc / l).astype(BF16)


def _attn_call(q, k, v, tok0, n_seq, s_len):
    tq = TQ_ATTN
    nq = s_len // tq
    qb0 = tok0 // tq
    sb0 = tok0 // s_len
    return pl.pallas_call(
        _attn_kernel,
        grid=(n_seq, MLA_HEADS, nq),
        in_specs=[
            pl.BlockSpec((1, tq, KH), lambda s, h, i: (h, qb0 + s * nq + i, 0)),
            pl.BlockSpec((1, s_len, KH), lambda s, h, i: (h, sb0 + s, 0)),
            pl.BlockSpec((1, s_len, MLA_V), lambda s, h, i: (h, sb0 + s, 0)),
        ],
        out_specs=pl.BlockSpec((tq, MLA_V), lambda s, h, i: (s * nq + i, h)),
        out_shape=jax.ShapeDtypeStruct((n_seq * s_len, MLA_HEADS * MLA_V), BF16),
        compiler_params=_params("parallel", "parallel", "arbitrary"),
        name="mla_attention",
    )(q, k, v)


def _proj_res_kernel(x_ref, o_ref, mod_ref, w_ref, out_ref):
    out = jnp.dot(o_ref[...], w_ref[...], preferred_element_type=F32)
    out_ref[...] = x_ref[...] + mod_ref[0, 2:3, :] * out


def _proj_res_call(x, o, mod, w_o, seg):
    t, d = x.shape
    tm = TM_OUT
    return pl.pallas_call(
        _proj_res_kernel,
        grid=(t // tm,),
        in_specs=[
            pl.BlockSpec((tm, d), lambda i: (i, 0)),
            pl.BlockSpec((tm, o.shape[1]), lambda i: (i, 0)),
            pl.BlockSpec((1, 6, d), lambda i: ((i * tm) // seg, 0, 0)),
            _const_spec(w_o.shape),
        ],
        out_specs=pl.BlockSpec((tm, d), lambda i: (i, 0)),
        out_shape=jax.ShapeDtypeStruct((t, d), F32),
        compiler_params=_params("parallel"),
        name="mla_out_proj",
    )(x, o, mod, w_o.astype(BF16))


def _router_kernel(x_ref, mod_ref, g_ref, wr_ref, br_ref,
                   h_ref, ids_ref, gates_ref, rank_ref, gcol_ref, cnt_ref, carry_ref):
    i = pl.program_id(0)

    @pl.when(i == 0)
    def _():
        carry_ref[...] = jnp.zeros_like(carry_ref)

    x = x_ref[...]
    tm = x.shape[0]
    h = _norm_mod(x, g_ref[...], mod_ref[0, 3:4, :], mod_ref[0, 4:5, :])
    h_ref[...] = h
    logits = lax.dot_general(wr_ref[...], h, (((1,), (1,)), ((), ())),
                             precision=lax.Precision.HIGHEST,
                             preferred_element_type=F32) + br_ref[...]
    eidx = lax.broadcasted_iota(jnp.int32, logits.shape, 0)
    lg = logits
    sel = jnp.zeros(logits.shape, F32)
    vals, ids = [], []
    for _ in range(TOP_K):
        m = jnp.max(lg, axis=0, keepdims=True)
        idx = jnp.min(jnp.where(lg == m, eidx, N_EXPERTS), axis=0, keepdims=True)
        onehot = eidx == idx
        vals.append(m)
        ids.append(idx)
        lg = jnp.where(onehot, -jnp.inf, lg)
        sel = jnp.where(onehot, 1.0, sel)
    ex = [jnp.exp(v - vals[0]) for v in vals]
    den = ex[0] + ex[1] + ex[2] + ex[3]
    gates = [e / den for e in ex]
    r_io = lax.broadcasted_iota(jnp.int32, (tm, tm), 0)
    c_io = lax.broadcasted_iota(jnp.int32, (tm, tm), 1)
    before = jnp.where(r_io < c_io, 1.0, 0.0).astype(BF16)
    cum = jnp.dot(sel.astype(BF16), before, preferred_element_type=F32) + carry_ref[:, 0:1]
    for k in range(TOP_K):
        rk = jnp.sum(jnp.where(eidx == ids[k], cum, 0.0), axis=0, keepdims=True)
        rank_ref[k:k + 1, :] = rk.astype(jnp.int32)
        ids_ref[k:k + 1, :] = ids[k]
        gates_ref[k:k + 1, :] = gates[k]
    g8 = jnp.concatenate(gates + [jnp.zeros((LANES - TOP_K, tm), F32)], axis=0)
    gcol_ref[...] = g8.T
    carry_ref[...] = carry_ref[...] + jnp.sum(sel, axis=1, keepdims=True)
    cnt_ref[...] = carry_ref[...]


def _router_call(x, mod, g, w_r, b_r, seg):
    t, d = x.shape
    tm = TM_ROUTER
    e = w_r.shape[1]
    return pl.pallas_call(
        _router_kernel,
        grid=(t // tm,),
        in_specs=[
            pl.BlockSpec((tm, d), lambda i: (i, 0)),
            pl.BlockSpec((1, 6, d), lambda i: ((i * tm) // seg, 0, 0)),
            _const_spec((1, d)),
            _const_spec((e, d)),
            _const_spec((e, 1)),
        ],
        out_specs=[
            pl.BlockSpec((tm, d), lambda i: (i, 0)),
            pl.BlockSpec((TOP_K, tm), lambda i: (0, i)),
            pl.BlockSpec((TOP_K, tm), lambda i: (0, i)),
            pl.BlockSpec((TOP_K, tm), lambda i: (0, i)),
            pl.BlockSpec((tm, LANES), lambda i: (i, 0)),
            _const_spec((e, LANES)),
        ],
        out_shape=[
            jax.ShapeDtypeStruct((t, d), F32),
            jax.ShapeDtypeStruct((TOP_K, t), jnp.int32),
            jax.ShapeDtypeStruct((TOP_K, t), F32),
            jax.ShapeDtypeStruct((TOP_K, t), jnp.int32),
            jax.ShapeDtypeStruct((t, LANES), F32),
            jax.ShapeDtypeStruct((e, LANES), F32),
        ],
        scratch_shapes=[pltpu.VMEM((e, LANES), F32)],
        compiler_params=_params("arbitrary"),
        name="moe_router",
    )(x, mod, g.reshape(1, d), w_r.T, b_r.reshape(e, 1))


def _gather_kernel(idx_ref, src_ref, o_ref, sem):
    i = pl.program_id(0)
    rows = o_ref.shape[0]
    base = i * rows

    def row_copy(r, src_row):
        return pltpu.make_async_copy(src_ref.at[pl.ds(src_row, 1), :],
                                     o_ref.at[pl.ds(r, 1), :], sem)

    def issue(r, _):
        row_copy(r, idx_ref[base + r]).start()
        return 0

    lax.fori_loop(0, rows, issue, 0, unroll=8)

    def drain(r, _):
        row_copy(r, 0).wait()
        return 0

    lax.fori_loop(0, rows, drain, 0, unroll=8)


def _gather_rows(src, idx):
    n = idx.shape[0]
    d = src.shape[1]
    grid_spec = pltpu.PrefetchScalarGridSpec(
        num_scalar_prefetch=1,
        grid=(n // GATHER_ROWS,),
        in_specs=[pl.BlockSpec(memory_space=pl.ANY)],
        out_specs=pl.BlockSpec((GATHER_ROWS, d), lambda i, idx: (i, 0)),
        scratch_shapes=[pltpu.SemaphoreType.DMA],
    )
    return pl.pallas_call(
        _gather_kernel,
        grid_spec=grid_spec,
        out_shape=jax.ShapeDtypeStruct((n, d), src.dtype),
        compiler_params=_params("arbitrary"),
        name="row_gather",
    )(idx, src)


def _expert_kernel(blk_e_ref, nused_ref, x_ref, wgu_ref, bgu_ref, wdn_ref, bdn_ref,
                   o_ref, wgu_bf, wdn_bf):
    i = pl.program_id(0)
    e = blk_e_ref[i]
    e_prev = blk_e_ref[jnp.maximum(i - 1, 0)]
    active = i < nused_ref[0]
    d_e = wdn_ref.shape[2]

    @pl.when(active & ((i == 0) | (e != e_prev)))
    def _():
        step = 128
        for r in range(0, wgu_ref.shape[2], step):
            wgu_bf[r:r + step, :] = wgu_ref[0, 0, r:r + step, :].astype(BF16)
        for r in range(0, d_e, step):
            wdn_bf[r:r + step, :] = wdn_ref[0, 0, r:r + step, :].astype(BF16)

    @pl.when(active)
    def _():
        xb = x_ref[...].astype(BF16)
        gu = jnp.dot(xb, wgu_bf[...], preferred_element_type=F32) + bgu_ref[0, 0]
        gt = jnp.minimum(gu[:, :d_e], SWIGLU_LIMIT)
        up = jnp.clip(gu[:, d_e:], -SWIGLU_LIMIT, SWIGLU_LIMIT)
        a = (up + 1.0) * (gt * jax.nn.sigmoid(SWIGLU_ALPHA * gt))
        o_ref[...] = jnp.dot(a.astype(BF16), wdn_bf[...],
                             preferred_element_type=F32) + bdn_ref[0, 0]

    @pl.when(jnp.logical_not(active))
    def _():
        o_ref[...] = jnp.zeros_like(o_ref)


def _expert_call(xg, blk_e, n_used, w_gu, b_gu, w_dn, b_dn, layer):
    rows, d = xg.shape
    nb = rows // MOE_ROWS
    _, n_e, _, two_de = w_gu.shape
    d_e = two_de // 2
    grid_spec = pltpu.PrefetchScalarGridSpec(
        num_scalar_prefetch=2,
        grid=(nb,),
        in_specs=[
            pl.BlockSpec((MOE_ROWS, d), lambda i, be, nu: (i, 0)),
            pl.BlockSpec((1, 1, d, two_de), lambda i, be, nu: (layer, be[i], 0, 0)),
            pl.BlockSpec((1, 1, 1, two_de), lambda i, be, nu: (layer, be[i], 0, 0)),
            pl.BlockSpec((1, 1, d_e, d), lambda i, be, nu: (layer, be[i], 0, 0)),
            pl.BlockSpec((1, 1, 1, d), lambda i, be, nu: (layer, be[i], 0, 0)),
        ],
        out_specs=pl.BlockSpec((MOE_ROWS, d), lambda i, be, nu: (i, 0)),
        scratch_shapes=[pltpu.VMEM((d, two_de), BF16), pltpu.VMEM((d_e, d), BF16)],
    )
    return pl.pallas_call(
        _expert_kernel,
        grid_spec=grid_spec,
        out_shape=jax.ShapeDtypeStruct((rows, d), F32),
        compiler_params=_params("arbitrary"),
        name="moe_experts",
    )(blk_e, n_used, xg, w_gu, b_gu.reshape(b_gu.shape[0], n_e, 1, two_de),
      w_dn, b_dn.reshape(b_dn.shape[0], n_e, 1, d))


def _combine_kernel(x_ref, y_ref, gcol_ref, mod_ref, o_ref):
    gc = gcol_ref[...]
    acc = gc[:, 0:1] * y_ref[0]
    for k in range(1, TOP_K):
        acc = acc + gc[:, k:k + 1] * y_ref[k]
    o_ref[...] = x_ref[...] + mod_ref[0, 5:6, :] * acc


def _combine_call(x, yk, gcol, mod, seg):
    t, d = x.shape
    tm = TM_OUT
    return pl.pallas_call(
        _combine_kernel,
        grid=(t // tm,),
        in_specs=[
            pl.BlockSpec((tm, d), lambda i: (i, 0)),
            pl.BlockSpec((TOP_K, tm, d), lambda i: (0, i, 0)),
            pl.BlockSpec((tm, LANES), lambda i: (i, 0)),
            pl.BlockSpec((1, 6, d), lambda i: ((i * tm) // seg, 0, 0)),
        ],
        out_specs=pl.BlockSpec((tm, d), lambda i: (i, 0)),
        out_shape=jax.ShapeDtypeStruct((t, d), F32),
        compiler_params=_params("parallel"),
        name="moe_combine",
    )(x, yk, gcol, mod)


def _moe_layer(x, mod, g, w_r, b_r, w_gu, b_gu, w_dn, b_dn, layer, seg):
    t, d = x.shape
    h, ids, _, rank, gcol, cnt = _router_call(x, mod, g, w_r, b_r, seg)
    counts = cnt[:, 0].astype(jnp.int32)
    padded = (counts + MOE_ROWS - 1) // MOE_ROWS * MOE_ROWS
    pad_end = jnp.cumsum(padded)
    pad_start = pad_end - padded
    dest = pad_start[ids] + rank
    nb = -(-(t * TOP_K) // MOE_ROWS) + N_EXPERTS
    rows = nb * MOE_ROWS
    tok = jnp.broadcast_to(jnp.arange(t, dtype=jnp.int32)[None, :], dest.shape)
    row_tok = jnp.zeros((rows,), jnp.int32).at[dest.reshape(-1)].set(
        tok.reshape(-1), unique_indices=True)
    blk_e = jnp.minimum(
        jnp.searchsorted(pad_end, jnp.arange(nb, dtype=jnp.int32) * MOE_ROWS, side="right"),
        N_EXPERTS - 1).astype(jnp.int32)
    n_used = (pad_end[-1:] // MOE_ROWS).astype(jnp.int32)
    xg = _gather_rows(h, row_tok)
    yb = _expert_call(xg, blk_e, n_used, w_gu, b_gu, w_dn, b_dn, layer)
    yk = _gather_rows(yb, dest.reshape(-1)).reshape(TOP_K, t, d)
    return _combine_call(x, yk, gcol, mod, seg)


def kernel(x_prompt, x_sample, c_prompt, c_sample, g_mix, g_ffn, w_ada, b_ada, gm_w_in, gm_g_v, gm_w_s, gm_b_s, gm_w_out, sc_w_in, sc_w_conv, sc_w_out, mla_w_down, mla_g_q_lat, mla_g_kv_lat, mla_w_uq, mla_w_ukv, mla_g_qh, mla_g_kh, mla_w_o, moe_w_router, moe_b_router, moe_w_gu, moe_b_gu, moe_w_dn, moe_b_dn):
    bp, sp, d = x_prompt.shape
    bs, ss, _ = x_sample.shape
    tp, ts = bp * sp, bs * ss
    t = tp + ts
    depth = g_mix.shape[0]
    seg = min(sp, ss)
    assert sp % seg == 0 and ss % seg == 0

    x = jnp.concatenate([x_prompt.reshape(tp, d), x_sample.reshape(ts, d)], axis=0)

    n_seq = bp + bs
    c_all = jnp.concatenate([c_prompt, c_sample], axis=0)
    c_pad = jnp.pad(c_all, ((0, (-n_seq) % 8), (0, 0)))
    mod_all = _ada_call(c_pad, w_ada, b_ada)[:, :n_seq].reshape(depth, n_seq, 6, d)
    seg_seq = jnp.concatenate([jnp.repeat(jnp.arange(bp), sp // seg),
                               bp + jnp.repeat(jnp.arange(bs), ss // seg)])
    mod_seg = mod_all[:, seg_seq]

    def tile_meta(tm):
        t0 = jnp.arange(t // tm, dtype=jnp.int32) * tm
        pos = jnp.where(t0 < tp, t0 % sp, (t0 - tp) % ss)
        slen = jnp.where(t0 < tp, sp, ss)
        return pos, slen

    for i in range(depth):
        mod = mod_seg[i]
        kind, j = i % 3, i // 3
        if kind == 0:
            x = _gmlp_call(x, mod, g_mix[i], gm_w_in[j], gm_g_v[j], gm_w_s[j], gm_b_s[j],
                           gm_w_out[j], seg)
        elif kind == 1:
            pos, slen = tile_meta(TM_CONV)
            x = _conv_call(x, mod, g_mix[i], sc_w_in[j], sc_w_conv[j], sc_w_out[j], seg,
                           (pos != 0).astype(jnp.int32),
                           (pos + TM_CONV != slen).astype(jnp.int32))
        else:
            wd, uq, ukv, gh = _mla_prepare_weights(mla_w_down[j], mla_w_uq[j], mla_w_ukv[j],
                                                  mla_g_qh[j], mla_g_kh[j])
            cos, sin = _rope_tables(max(sp, ss))
            pos, _ = tile_meta(TM_MLA)
            q, k, v = _mla_pre_call(x, mod, g_mix[i], wd, mla_g_q_lat[j], mla_g_kv_lat[j],
                                    uq, ukv, gh, cos, sin, seg, pos // TM_MLA)
            o = jnp.concatenate([_attn_call(q, k, v, 0, bp, sp),
                                 _attn_call(q, k, v, tp, bs, ss)], axis=0)
            x = _proj_res_call(x, o, mod, mla_w_o[j], seg)
        x = _moe_layer(x, mod, g_ffn[i], moe_w_router[i], moe_b_router[i],
                       moe_w_gu, moe_b_gu, moe_w_dn, moe_b_dn, i, seg)

    return (x[:tp].reshape(bp, sp, d), x[tp:].reshape(bs, ss, d))
```

```python
import functools

import jax
import jax.numpy as jnp
from jax import lax
from jax.experimental import pallas as pl
from jax.experimental.pallas import tpu as pltpu
from jax.experimental.pallas import tpu_sc as plsc

F32 = jnp.float32
BF16 = jnp.bfloat16

NORM_EPS = 1e-6
GM_GROUPS = 8
GM_CHUNK = 128
MLA_HEADS = 8
MLA_Q_RANK = 384
MLA_KV_RANK = 256
MLA_NOPE = 128
MLA_ROPE = 64
MLA_V = 128
MLA_DK = MLA_NOPE + MLA_ROPE
ROPE_THETA = 10000.0
N_EXPERTS = 32
TOP_K = 4
SWIGLU_LIMIT = 7.0
SWIGLU_ALPHA = 1.702

LANES = 128
HALO = 16
VMEM_LIMIT = 56 * 1024 * 1024

TM_GMLP = 256
TM_CONV = 512
TM_MLA = 256
TM_ROUTER = 512
TM_OUT = 512
TQ_ATTN = 256
TK_ATTN = 512
MOE_ROWS = 256


def _params(*sem):
    return pltpu.CompilerParams(dimension_semantics=sem, vmem_limit_bytes=VMEM_LIMIT)


def _const_spec(shape):
    nd = len(shape)
    return pl.BlockSpec(shape, lambda *_: (0,) * nd)


def _norm_mod(x, g, shift, scale):
    ms = jnp.mean(x * x, axis=-1, keepdims=True)
    return (x * lax.rsqrt(ms + NORM_EPS)) * g * (1.0 + scale) + shift


def _ada_kernel(c_ref, w_ref, b_ref, o_ref):
    c = c_ref[...]
    a = (c * jax.nn.sigmoid(c)).astype(BF16)
    w = w_ref[0].astype(BF16)
    o_ref[0] = jnp.dot(a, w, preferred_element_type=F32) + b_ref[0]


def _ada_call(c_pad, w_ada, b_ada):
    depth, d, n = w_ada.shape
    tn = 1536
    rows = c_pad.shape[0]
    return pl.pallas_call(
        _ada_kernel,
        grid=(depth, n // tn),
        in_specs=[
            pl.BlockSpec((rows, d), lambda l, j: (0, 0)),
            pl.BlockSpec((1, d, tn), lambda l, j: (l, 0, j)),
            pl.BlockSpec((1, 1, tn), lambda l, j: (l, 0, j)),
        ],
        out_specs=pl.BlockSpec((1, rows, tn), lambda l, j: (l, 0, j)),
        out_shape=jax.ShapeDtypeStruct((depth, rows, n), F32),
        compiler_params=_params("arbitrary", "arbitrary"),
        name="adaln_mod",
    )(c_pad, w_ada, b_ada.reshape(depth, 1, n))


def _gmlp_kernel(x_ref, mod_ref, g_ref, win_ref, gv_ref, ws_ref, bs_ref, wout_ref,
                 o_ref, gated_ref):
    x = x_ref[...]
    tm = x.shape[0]
    width = gv_ref.shape[1]
    gw = width // GM_GROUPS
    h = _norm_mod(x, g_ref[...], mod_ref[0, 0:1, :], mod_ref[0, 1:2, :]).astype(BF16)
    uv = jax.nn.gelu(jnp.dot(h, win_ref[...], preferred_element_type=F32))
    u = uv[:, :width]
    v = uv[:, width:]
    v = v * lax.rsqrt(jnp.mean(v * v, axis=-1, keepdims=True) + NORM_EPS) * gv_ref[...]
    vb = v.astype(BF16)
    bs = bs_ref[...]
    for c in range(tm // GM_CHUNK):
        r0 = c * GM_CHUNK
        for g in range(GM_GROUPS):
            c0 = g * gw
            vm = jnp.dot(ws_ref[g], vb[r0:r0 + GM_CHUNK, c0:c0 + gw],
                         preferred_element_type=F32) + bs[:, g:g + 1]
            gated_ref[r0:r0 + GM_CHUNK, c0:c0 + gw] = (
                u[r0:r0 + GM_CHUNK, c0:c0 + gw] * vm).astype(BF16)
    out = jnp.dot(gated_ref[...], wout_ref[...], preferred_element_type=F32)
    o_ref[...] = x + mod_ref[0, 2:3, :] * out


def _gmlp_call(x, mod, g, w_in, g_v, w_s, b_s, w_out, seg):
    t, d = x.shape
    tm = TM_GMLP
    width = g_v.shape[-1]
    return pl.pallas_call(
        _gmlp_kernel,
        grid=(t // tm,),
        in_specs=[
            pl.BlockSpec((tm, d), lambda i: (i, 0)),
            pl.BlockSpec((1, 6, d), lambda i: ((i * tm) // seg, 0, 0)),
            _const_spec((1, d)),
            _const_spec(w_in.shape),
            _const_spec((1, width)),
            _const_spec(w_s.shape),
            _const_spec((GM_CHUNK, GM_GROUPS)),
            _const_spec(w_out.shape),
        ],
        out_specs=pl.BlockSpec((tm, d), lambda i: (i, 0)),
        out_shape=jax.ShapeDtypeStruct((t, d), F32),
        scratch_shapes=[pltpu.VMEM((tm, width), BF16)],
        compiler_params=_params("parallel"),
        name="gmlp_mixer",
    )(x, mod, g.reshape(1, d), w_in.astype(BF16), g_v.reshape(1, width),
      w_s.astype(BF16), b_s.T, w_out.astype(BF16))


def _conv_kernel(start_ref, end_ref, x_ref, xp_ref, xn_ref, mod_ref, g_ref, win_ref,
                 wc_ref, wout_ref, o_ref, h_ref, z_ref):
    i = pl.program_id(0)
    x = x_ref[...]
    tm, d = x.shape
    g = g_ref[...]
    sh = mod_ref[0, 0:1, :]
    sc = mod_ref[0, 1:2, :]
    h_ref[0:HALO, :] = _norm_mod(xp_ref[...], g, sh, sc).astype(BF16)
    h_ref[HALO:HALO + tm, :] = _norm_mod(x, g, sh, sc).astype(BF16)
    h_ref[HALO + tm:, :] = _norm_mod(xn_ref[...], g, sh, sc).astype(BF16)
    bcx = jnp.dot(h_ref[...], win_ref[...], preferred_element_type=F32)
    w = bcx.shape[1] // 3
    bg = bcx[HALO:HALO + tm, :w]
    z_ref[...] = bcx[:, w:2 * w] * bcx[:, 2 * w:]
    keep_prev = (start_ref[i] == 0).astype(F32)
    keep_next = (end_ref[i] == 0).astype(F32)
    z_ref[0:HALO, :] = z_ref[0:HALO, :] * keep_prev
    z_ref[HALO + tm:, :] = z_ref[HALO + tm:, :] * keep_next
    y = (wc_ref[0:1, :] * z_ref[HALO - 1:HALO - 1 + tm, :]
         + wc_ref[1:2, :] * z_ref[HALO:HALO + tm, :]
         + wc_ref[2:3, :] * z_ref[HALO + 1:HALO + 1 + tm, :])
    out = jnp.dot((bg * y).astype(BF16), wout_ref[...], preferred_element_type=F32)
    o_ref[...] = x + mod_ref[0, 2:3, :] * out


def _conv_call(x, mod, g, w_in, w_conv, w_out, seg, tile_start, tile_end):
    t, d = x.shape
    tm = TM_CONV
    w = w_conv.shape[-1]
    hb = tm // HALO
    last = t // HALO - 1
    grid_spec = pltpu.PrefetchScalarGridSpec(
        num_scalar_prefetch=2,
        grid=(t // tm,),
        in_specs=[
            pl.BlockSpec((tm, d), lambda i, *_: (i, 0)),
            pl.BlockSpec((HALO, d), lambda i, *_: (jnp.maximum(i * hb - 1, 0), 0)),
            pl.BlockSpec((HALO, d), lambda i, *_: (jnp.minimum((i + 1) * hb, last), 0)),
            pl.BlockSpec((1, 6, d), lambda i, *_: ((i * tm) // seg, 0, 0)),
            _const_spec((1, d)),
            _const_spec(w_in.shape),
            _const_spec(w_conv.shape),
            _const_spec(w_out.shape),
        ],
        out_specs=pl.BlockSpec((tm, d), lambda i, *_: (i, 0)),
        scratch_shapes=[pltpu.VMEM((tm + 2 * HALO, d), BF16),
                        pltpu.VMEM((tm + 2 * HALO, w), F32)],
    )
    return pl.pallas_call(
        _conv_kernel,
        grid_spec=grid_spec,
        out_shape=jax.ShapeDtypeStruct((t, d), F32),
        compiler_params=_params("parallel"),
        name="shortconv_mixer",
    )(tile_start, tile_end, x, x, x, mod, g.reshape(1, d), w_in.astype(BF16),
      w_conv, w_out.astype(BF16))


QH = 3 * LANES
KH = 2 * LANES


def _mla_pre_kernel(pos_ref, x_ref, mod_ref, g_ref, wd_ref, gq_ref, gkv_ref, wuq_ref,
                    wukv_ref, gh_ref, cos_ref, sin_ref, q_ref, k_ref, v_ref):
    del pos_ref
    x = x_ref[...]
    h = _norm_mod(x, g_ref[...], mod_ref[0, 0:1, :], mod_ref[0, 1:2, :]).astype(BF16)
    lat = jnp.dot(h, wd_ref[...], preferred_element_type=F32)
    ql = lat[:, :MLA_Q_RANK]
    kvl = lat[:, MLA_Q_RANK:MLA_Q_RANK + MLA_KV_RANK]
    pe = lat[:, MLA_Q_RANK + MLA_KV_RANK:MLA_Q_RANK + MLA_KV_RANK + LANES]
    pe_sw = lat[:, MLA_Q_RANK + MLA_KV_RANK + LANES:]
    ql = ql * lax.rsqrt(jnp.mean(ql * ql, axis=-1, keepdims=True) + NORM_EPS) * gq_ref[...]
    kvl = kvl * lax.rsqrt(jnp.mean(kvl * kvl, axis=-1, keepdims=True) + NORM_EPS) * gkv_ref[...]
    q = jnp.dot(ql.astype(BF16), wuq_ref[...], preferred_element_type=F32)
    kv = jnp.dot(kvl.astype(BF16), wukv_ref[...], preferred_element_type=F32)
    cos = cos_ref[...]
    sin = sin_ref[...]
    gqn, gqr, gqs = gh_ref[0:1, :], gh_ref[1:2, :], gh_ref[2:3, :]
    gkn, gkr, gks = gh_ref[3:4, :], gh_ref[4:5, :], gh_ref[5:6, :]
    pe_ss = jnp.sum(pe * pe, axis=-1, keepdims=True)
    qscale = MLA_DK ** -0.5
    for hd in range(MLA_HEADS):
        qn = q[:, hd * QH:hd * QH + LANES]
        qr = q[:, hd * QH + LANES:hd * QH + 2 * LANES]
        qs = q[:, hd * QH + 2 * LANES:(hd + 1) * QH]
        ss = jnp.sum(qn * qn, axis=-1, keepdims=True) + jnp.sum(qr * qr, axis=-1, keepdims=True)
        rinv = lax.rsqrt(ss * (1.0 / MLA_DK) + NORM_EPS) * qscale
        q_ref[hd, :, 0:LANES] = (qn * rinv * gqn).astype(BF16)
        q_ref[hd, :, LANES:] = ((qr * rinv * gqr) * cos + (qs * rinv * gqs) * sin).astype(BF16)
        kn = kv[:, hd * 2 * LANES:hd * 2 * LANES + LANES]
        ss = jnp.sum(kn * kn, axis=-1, keepdims=True) + pe_ss
        rinv = lax.rsqrt(ss * (1.0 / MLA_DK) + NORM_EPS)
        k_ref[hd, :, 0:LANES] = (kn * rinv * gkn).astype(BF16)
        k_ref[hd, :, LANES:] = ((pe * rinv * gkr) * cos + (pe_sw * rinv * gks) * sin).astype(BF16)
        v_ref[hd] = kv[:, hd * 2 * LANES + LANES:(hd + 1) * 2 * LANES].astype(BF16)


def _mla_prepare_weights(w_down, w_uq, w_ukv, g_qh, g_kh):
    half = MLA_ROPE // 2
    pad = LANES - MLA_ROPE

    def swap(a):
        return jnp.concatenate([a[..., half:], a[..., :half]], axis=-1)

    def pad_lanes(a):
        return jnp.pad(a, [(0, 0)] * (a.ndim - 1) + [(0, pad)])

    d = w_down.shape[0]
    pe_w = w_down[:, MLA_Q_RANK + MLA_KV_RANK:]
    wd = jnp.concatenate([w_down[:, :MLA_Q_RANK + MLA_KV_RANK],
                          pad_lanes(pe_w), pad_lanes(swap(pe_w))], axis=1)
    uq = w_uq.reshape(MLA_Q_RANK, MLA_HEADS, MLA_DK)
    uq_r = uq[..., MLA_NOPE:]
    uq = jnp.concatenate([uq[..., :MLA_NOPE], pad_lanes(uq_r), pad_lanes(swap(uq_r))], axis=-1)
    uq = uq.reshape(MLA_Q_RANK, MLA_HEADS * QH)

    def gains(gv):
        r = gv[MLA_NOPE:]
        return [gv[:MLA_NOPE], pad_lanes(r), pad_lanes(swap(r))]

    gh = jnp.stack(gains(g_qh) + gains(g_kh) + [jnp.zeros((LANES,), F32)] * 2)
    return wd.astype(BF16), uq.astype(BF16), w_ukv.astype(BF16), gh


def _rope_tables(s):
    half = MLA_ROPE // 2
    inv = 1.0 / (ROPE_THETA ** (jnp.arange(0, MLA_ROPE, 2, dtype=F32) / MLA_ROPE))
    ang = jnp.arange(s, dtype=F32)[:, None] * inv[None, :]
    cos, sin = jnp.cos(ang), jnp.sin(ang)
    zeros = jnp.zeros((s, LANES - MLA_ROPE), F32)
    del half
    return (jnp.concatenate([cos, cos, zeros], axis=1),
            jnp.concatenate([-sin, sin, zeros], axis=1))


def _mla_pre_call(x, mod, g, wd, g_q_lat, g_kv_lat, uq, ukv, gh, cos, sin, seg, tile_pos):
    t, d = x.shape
    tm = TM_MLA
    grid_spec = pltpu.PrefetchScalarGridSpec(
        num_scalar_prefetch=1,
        grid=(t // tm,),
        in_specs=[
            pl.BlockSpec((tm, d), lambda i, p: (i, 0)),
            pl.BlockSpec((1, 6, d), lambda i, p: ((i * tm) // seg, 0, 0)),
            _const_spec((1, d)),
            _const_spec(wd.shape),
            _const_spec((1, MLA_Q_RANK)),
            _const_spec((1, MLA_KV_RANK)),
            _const_spec(uq.shape),
            _const_spec(ukv.shape),
            _const_spec(gh.shape),
            pl.BlockSpec((tm, LANES), lambda i, p: (p[i], 0)),
            pl.BlockSpec((tm, LANES), lambda i, p: (p[i], 0)),
        ],
        out_specs=[
            pl.BlockSpec((MLA_HEADS, tm, KH), lambda i, p: (0, i, 0)),
            pl.BlockSpec((MLA_HEADS, tm, KH), lambda i, p: (0, i, 0)),
            pl.BlockSpec((MLA_HEADS, tm, MLA_V), lambda i, p: (0, i, 0)),
        ],
    )
    return pl.pallas_call(
        _mla_pre_kernel,
        grid_spec=grid_spec,
        out_shape=[jax.ShapeDtypeStruct((MLA_HEADS, t, KH), BF16),
                   jax.ShapeDtypeStruct((MLA_HEADS, t, KH), BF16),
                   jax.ShapeDtypeStruct((MLA_HEADS, t, MLA_V), BF16)],
        compiler_params=_params("parallel"),
        name="mla_qkv",
    )(tile_pos, x, mod, g.reshape(1, d), wd, g_q_lat.reshape(1, -1), g_kv_lat.reshape(1, -1),
      uq, ukv, gh, cos, sin)


def _attn_kernel(q_ref, k_ref, v_ref, o_ref):
    q = q_ref[0]
    tq = q.shape[0]
    s_len = k_ref.shape[1]

    def body(j, carry):
        m, l, acc = carry
        off = pl.multiple_of(j * TK_ATTN, TK_ATTN)
        k = k_ref[0, pl.ds(off, TK_ATTN), :]
        v = v_ref[0, pl.ds(off, TK_ATTN), :]
        s = lax.dot_general(q, k, (((1,), (1,)), ((), ())), preferred_element_type=F32)
        m_new = jnp.maximum(m, jnp.max(s, axis=-1, keepdims=True))
        alpha = jnp.exp(m - m_new)
        p = jnp.exp(s - m_new)
        l = alpha * l + jnp.sum(p, axis=-1, keepdims=True)
        acc = alpha * acc + jnp.dot(p.astype(BF16), v, preferred_element_type=F32)
        return m_new, l, acc

    m0 = jnp.full((tq, 1), -jnp.inf, F32)
    l0 = jnp.zeros((tq, 1), F32)
    a0 = jnp.zeros((tq, MLA_V), F32)
    _, l, acc = lax.fori_loop(0, s_len // TK_ATTN, body, (m0, l0, a0))
    o_ref[...] = (acc / l).astype(BF16)


def _attn_call(q, k, v, tok0, n_seq, s_len):
    tq = TQ_ATTN
    nq = s_len // tq
    qb0 = tok0 // tq
    sb0 = tok0 // s_len
    return pl.pallas_call(
        _attn_kernel,
        grid=(n_seq, MLA_HEADS, nq),
        in_specs=[
            pl.BlockSpec((1, tq, KH), lambda s, h, i: (h, qb0 + s * nq + i, 0)),
            pl.BlockSpec((1, s_len, KH), lambda s, h, i: (h, sb0 + s, 0)),
            pl.BlockSpec((1, s_len, MLA_V), lambda s, h, i: (h, sb0 + s, 0)),
        ],
        out_specs=pl.BlockSpec((tq, MLA_V), lambda s, h, i: (s * nq + i, h)),
        out_shape=jax.ShapeDtypeStruct((n_seq * s_len, MLA_HEADS * MLA_V), BF16),
        compiler_params=_params("parallel", "parallel", "arbitrary"),
        name="mla_attention",
    )(q, k, v)


def _proj_res_kernel(x_ref, o_ref, mod_ref, w_ref, out_ref):
    out = jnp.dot(o_ref[...], w_ref[...], preferred_element_type=F32)
    out_ref[...] = x_ref[...] + mod_ref[0, 2:3, :] * out


def _proj_res_call(x, o, mod, w_o, seg):
    t, d = x.shape
    tm = TM_OUT
    return pl.pallas_call(
        _proj_res_kernel,
        grid=(t // tm,),
        in_specs=[
            pl.BlockSpec((tm, d), lambda i: (i, 0)),
            pl.BlockSpec((tm, o.shape[1]), lambda i: (i, 0)),
            pl.BlockSpec((1, 6, d), lambda i: ((i * tm) // seg, 0, 0)),
            _const_spec(w_o.shape),
        ],
        out_specs=pl.BlockSpec((tm, d), lambda i: (i, 0)),
        out_shape=jax.ShapeDtypeStruct((t, d), F32),
        compiler_params=_params("parallel"),
        name="mla_out_proj",
    )(x, o, mod, w_o.astype(BF16))


def _router_kernel(x_ref, mod_ref, g_ref, wr_ref, br_ref,
                   h_ref, ids_ref, gates_ref, rank_ref, gcol_ref, cnt_ref, carry_ref):
    i = pl.program_id(0)

    @pl.when(i == 0)
    def _():
        carry_ref[...] = jnp.zeros_like(carry_ref)

    x = x_ref[...]
    tm = x.shape[0]
    h = _norm_mod(x, g_ref[...], mod_ref[0, 3:4, :], mod_ref[0, 4:5, :])
    h_ref[...] = h
    logits = lax.dot_general(wr_ref[...], h, (((1,), (1,)), ((), ())),
                             precision=lax.Precision.HIGHEST,
                             preferred_element_type=F32) + br_ref[...]
    eidx = lax.broadcasted_iota(jnp.int32, logits.shape, 0)
    lg = logits
    sel = jnp.zeros(logits.shape, F32)
    vals, ids = [], []
    for _ in range(TOP_K):
        m = jnp.max(lg, axis=0, keepdims=True)
        idx = jnp.min(jnp.where(lg == m, eidx, N_EXPERTS), axis=0, keepdims=True)
        onehot = eidx == idx
        vals.append(m)
        ids.append(idx)
        lg = jnp.where(onehot, -jnp.inf, lg)
        sel = jnp.where(onehot, 1.0, sel)
    ex = [jnp.exp(v - vals[0]) for v in vals]
    den = ex[0] + ex[1] + ex[2] + ex[3]
    gates = [e / den for e in ex]
    r_io = lax.broadcasted_iota(jnp.int32, (tm, tm), 0)
    c_io = lax.broadcasted_iota(jnp.int32, (tm, tm), 1)
    before = jnp.where(r_io < c_io, 1.0, 0.0).astype(BF16)
    cum = jnp.dot(sel.astype(BF16), before, preferred_element_type=F32) + carry_ref[:, 0:1]
    for k in range(TOP_K):
        rk = jnp.sum(jnp.where(eidx == ids[k], cum, 0.0), axis=0, keepdims=True)
        rank_ref[k:k + 1, :] = rk.astype(jnp.int32)
        ids_ref[k:k + 1, :] = ids[k]
        gates_ref[k:k + 1, :] = gates[k]
    g8 = jnp.concatenate(gates + [jnp.zeros((LANES - TOP_K, tm), F32)], axis=0)
    gcol_ref[...] = g8.T
    carry_ref[...] = carry_ref[...] + jnp.sum(sel, axis=1, keepdims=True)
    cnt_ref[...] = carry_ref[...]


def _router_call(x, mod, g, w_r, b_r, seg):
    t, d = x.shape
    tm = TM_ROUTER
    e = w_r.shape[1]
    return pl.pallas_call(
        _router_kernel,
        grid=(t // tm,),
        in_specs=[
            pl.BlockSpec((tm, d), lambda i: (i, 0)),
            pl.BlockSpec((1, 6, d), lambda i: ((i * tm) // seg, 0, 0)),
            _const_spec((1, d)),
            _const_spec((e, d)),
            _const_spec((e, 1)),
        ],
        out_specs=[
            pl.BlockSpec((tm, d), lambda i: (i, 0)),
            pl.BlockSpec((TOP_K, tm), lambda i: (0, i)),
            pl.BlockSpec((TOP_K, tm), lambda i: (0, i)),
            pl.BlockSpec((TOP_K, tm), lambda i: (0, i)),
            pl.BlockSpec((tm, LANES), lambda i: (i, 0)),
            _const_spec((e, LANES)),
        ],
        out_shape=[
            jax.ShapeDtypeStruct((t, d), F32),
            jax.ShapeDtypeStruct((TOP_K, t), jnp.int32),
            jax.ShapeDtypeStruct((TOP_K, t), F32),
            jax.ShapeDtypeStruct((TOP_K, t), jnp.int32),
            jax.ShapeDtypeStruct((t, LANES), F32),
            jax.ShapeDtypeStruct((e, LANES), F32),
        ],
        scratch_shapes=[pltpu.VMEM((e, LANES), F32)],
        compiler_params=_params("arbitrary"),
        name="moe_router",
    )(x, mod, g.reshape(1, d), w_r.T, b_r.reshape(e, 1))


SC_CORES = 2
SC_SUBCORES = 16
SC_WORKERS = SC_CORES * SC_SUBCORES
SC_CHUNK = 32


def _gather_rows(src, idx):
    n = idx.shape[0]
    d = src.shape[1]
    per_w = n // SC_WORKERS
    n_ch = per_w // SC_CHUNK
    assert per_w * SC_WORKERS == n and n_ch * SC_CHUNK == per_w
    mesh = plsc.VectorSubcoreMesh(core_axis_name="c", subcore_axis_name="s")

    @functools.partial(
        pl.kernel, mesh=mesh,
        out_type=jax.ShapeDtypeStruct((n, d), src.dtype),
        scratch_types=[pltpu.VMEM((n_ch, SC_CHUNK), jnp.int32),
                       pltpu.VMEM((SC_CHUNK, d), src.dtype),
                       pltpu.SemaphoreType.DMA],
        name="sc_row_gather",
    )
    def gather(src_hbm, idx_hbm, out_hbm, idx_v, rows_v, sem):
        wid = lax.axis_index("s") * SC_CORES + lax.axis_index("c")
        base = wid * per_w
        pltpu.sync_copy(idx_hbm.at[wid], idx_v)

        @pl.loop(0, n_ch)
        def _(j):
            pltpu.async_copy(src_hbm.at[idx_v.at[j]], rows_v, sem).wait()
            pltpu.sync_copy(rows_v, out_hbm.at[pl.ds(base + j * SC_CHUNK, SC_CHUNK)])

    return gather(src, idx.reshape(SC_WORKERS, n_ch, SC_CHUNK))


def _expert_kernel(blk_e_ref, nused_ref, x_ref, wgu_ref, bgu_ref, wdn_ref, bdn_ref,
                   o_ref, wgu_bf, wdn_bf):
    i = pl.program_id(0)
    e = blk_e_ref[i]
    e_prev = blk_e_ref[jnp.maximum(i - 1, 0)]
    active = i < nused_ref[0]
    d_e = wdn_ref.shape[2]

    @pl.when(active & ((i == 0) | (e != e_prev)))
    def _():
        step = 128
        for r in range(0, wgu_ref.shape[2], step):
            wgu_bf[r:r + step, :] = wgu_ref[0, 0, r:r + step, :].astype(BF16)
        for r in range(0, d_e, step):
            wdn_bf[r:r + step, :] = wdn_ref[0, 0, r:r + step, :].astype(BF16)

    @pl.when(active)
    def _():
        xb = x_ref[...].astype(BF16)
        gu = jnp.dot(xb, wgu_bf[...], preferred_element_type=F32) + bgu_ref[0, 0]
        gt = jnp.minimum(gu[:, :d_e], SWIGLU_LIMIT)
        up = jnp.clip(gu[:, d_e:], -SWIGLU_LIMIT, SWIGLU_LIMIT)
        a = (up + 1.0) * (gt * jax.nn.sigmoid(SWIGLU_ALPHA * gt))
        o_ref[...] = jnp.dot(a.astype(BF16), wdn_bf[...],
                             preferred_element_type=F32) + bdn_ref[0, 0]

    @pl.when(jnp.logical_not(active))
    def _():
        o_ref[...] = jnp.zeros_like(o_ref)


def _expert_call(xg, blk_e, n_used, w_gu, b_gu, w_dn, b_dn, layer):
    rows, d = xg.shape
    nb = rows // MOE_ROWS
    _, n_e, _, two_de = w_gu.shape
    d_e = two_de // 2
    grid_spec = pltpu.PrefetchScalarGridSpec(
        num_scalar_prefetch=2,
        grid=(nb,),
        in_specs=[
            pl.BlockSpec((MOE_ROWS, d), lambda i, be, nu: (i, 0)),
            pl.BlockSpec((1, 1, d, two_de), lambda i, be, nu: (layer, be[i], 0, 0)),
            pl.BlockSpec((1, 1, 1, two_de), lambda i, be, nu: (layer, be[i], 0, 0)),
            pl.BlockSpec((1, 1, d_e, d), lambda i, be, nu: (layer, be[i], 0, 0)),
            pl.BlockSpec((1, 1, 1, d), lambda i, be, nu: (layer, be[i], 0, 0)),
        ],
        out_specs=pl.BlockSpec((MOE_ROWS, d), lambda i, be, nu: (i, 0)),
        scratch_shapes=[pltpu.VMEM((d, two_de), BF16), pltpu.VMEM((d_e, d), BF16)],
    )
    return pl.pallas_call(
        _expert_kernel,
        grid_spec=grid_spec,
        out_shape=jax.ShapeDtypeStruct((rows, d), F32),
        compiler_params=_params("arbitrary"),
        name="moe_experts",
    )(blk_e, n_used, xg, w_gu, b_gu.reshape(b_gu.shape[0], n_e, 1, two_de),
      w_dn, b_dn.reshape(b_dn.shape[0], n_e, 1, d))


def _combine_kernel(x_ref, y_ref, gcol_ref, mod_ref, o_ref):
    gc = gcol_ref[...]
    acc = gc[:, 0:1] * y_ref[0]
    for k in range(1, TOP_K):
        acc = acc + gc[:, k:k + 1] * y_ref[k]
    o_ref[...] = x_ref[...] + mod_ref[0, 5:6, :] * acc


def _combine_call(x, yk, gcol, mod, seg):
    t, d = x.shape
    tm = TM_OUT
    return pl.pallas_call(
        _combine_kernel,
        grid=(t // tm,),
        in_specs=[
            pl.BlockSpec((tm, d), lambda i: (i, 0)),
            pl.BlockSpec((TOP_K, tm, d), lambda i: (0, i, 0)),
            pl.BlockSpec((tm, LANES), lambda i: (i, 0)),
            pl.BlockSpec((1, 6, d), lambda i: ((i * tm) // seg, 0, 0)),
        ],
        out_specs=pl.BlockSpec((tm, d), lambda i: (i, 0)),
        out_shape=jax.ShapeDtypeStruct((t, d), F32),
        compiler_params=_params("parallel"),
        name="moe_combine",
    )(x, yk, gcol, mod)


def _moe_layer(x, mod, g, w_r, b_r, w_gu, b_gu, w_dn, b_dn, layer, seg):
    t, d = x.shape
    h, ids, _, rank, gcol, cnt = _router_call(x, mod, g, w_r, b_r, seg)
    counts = cnt[:, 0].astype(jnp.int32)
    padded = (counts + MOE_ROWS - 1) // MOE_ROWS * MOE_ROWS
    pad_end = jnp.cumsum(padded)
    pad_start = pad_end - padded
    e_ar = jnp.arange(N_EXPERTS, dtype=jnp.int32)
    dest = rank + jnp.sum(jnp.where(ids[..., None] == e_ar, pad_start, 0), axis=-1)
    nb = -(-(t * TOP_K) // MOE_ROWS) + N_EXPERTS
    rows = nb * MOE_ROWS
    tok = jnp.broadcast_to(jnp.arange(t, dtype=jnp.int32)[None, :], dest.shape)
    row_tok = jnp.zeros((rows,), jnp.int32).at[dest.reshape(-1)].set(
        tok.reshape(-1), unique_indices=True)
    blk_row = jnp.arange(nb, dtype=jnp.int32) * MOE_ROWS
    blk_e = jnp.minimum(jnp.sum((pad_end[None, :] <= blk_row[:, None]).astype(jnp.int32), axis=1),
                        N_EXPERTS - 1)
    n_used = (pad_end[-1:] // MOE_ROWS).astype(jnp.int32)
    xg = _gather_rows(h, row_tok)
    yb = _expert_call(xg, blk_e, n_used, w_gu, b_gu, w_dn, b_dn, layer)
    yk = _gather_rows(yb, dest.reshape(-1)).reshape(TOP_K, t, d)
    return _combine_call(x, yk, gcol, mod, seg)


def kernel(x_prompt, x_sample, c_prompt, c_sample, g_mix, g_ffn, w_ada, b_ada, gm_w_in, gm_g_v, gm_w_s, gm_b_s, gm_w_out, sc_w_in, sc_w_conv, sc_w_out, mla_w_down, mla_g_q_lat, mla_g_kv_lat, mla_w_uq, mla_w_ukv, mla_g_qh, mla_g_kh, mla_w_o, moe_w_router, moe_b_router, moe_w_gu, moe_b_gu, moe_w_dn, moe_b_dn):
    bp, sp, d = x_prompt.shape
    bs, ss, _ = x_sample.shape
    tp, ts = bp * sp, bs * ss
    t = tp + ts
    depth = g_mix.shape[0]
    seg = min(sp, ss)
    assert sp % seg == 0 and ss % seg == 0

    x = jnp.concatenate([x_prompt.reshape(tp, d), x_sample.reshape(ts, d)], axis=0)

    n_seq = bp + bs
    c_all = jnp.concatenate([c_prompt, c_sample], axis=0)
    c_pad = jnp.pad(c_all, ((0, (-n_seq) % 8), (0, 0)))
    mod_all = _ada_call(c_pad, w_ada, b_ada)[:, :n_seq].reshape(depth, n_seq, 6, d)
    seg_seq = jnp.concatenate([jnp.repeat(jnp.arange(bp), sp // seg),
                               bp + jnp.repeat(jnp.arange(bs), ss // seg)])
    mod_seg = mod_all[:, seg_seq]

    def tile_meta(tm):
        t0 = jnp.arange(t // tm, dtype=jnp.int32) * tm
        pos = jnp.where(t0 < tp, t0 % sp, (t0 - tp) % ss)
        slen = jnp.where(t0 < tp, sp, ss)
        return pos, slen

    for i in range(depth):
        mod = mod_seg[i]
        kind, j = i % 3, i // 3
        if kind == 0:
            x = _gmlp_call(x, mod, g_mix[i], gm_w_in[j], gm_g_v[j], gm_w_s[j], gm_b_s[j],
                           gm_w_out[j], seg)
        elif kind == 1:
            pos, slen = tile_meta(TM_CONV)
            x = _conv_call(x, mod, g_mix[i], sc_w_in[j], sc_w_conv[j], sc_w_out[j], seg,
                           (pos != 0).astype(jnp.int32),
                           (pos + TM_CONV != slen).astype(jnp.int32))
        else:
            wd, uq, ukv, gh = _mla_prepare_weights(mla_w_down[j], mla_w_uq[j], mla_w_ukv[j],
                                                  mla_g_qh[j], mla_g_kh[j])
            cos, sin = _rope_tables(max(sp, ss))
            pos, _ = tile_meta(TM_MLA)
            q, k, v = _mla_pre_call(x, mod, g_mix[i], wd, mla_g_q_lat[j], mla_g_kv_lat[j],
                                    uq, ukv, gh, cos, sin, seg, pos // TM_MLA)
            o = jnp.concatenate([_attn_call(q, k, v, 0, bp, sp),
                                 _attn_call(q, k, v, tp, bs, ss)], axis=0)
            x = _proj_res_call(x, o, mod, mla_w_o[j], seg)
        x = _moe_layer(x, mod, g_ffn[i], moe_w_router[i], moe_b_router[i],
                       moe_w_gu, moe_b_gu, moe_w_dn, moe_b_dn, i, seg)

    return (x[:tp].reshape(bp, sp, d), x[tp:].reshape(bs, ss, d))
```

```python
import functools

import jax
import jax.numpy as jnp
from jax import lax
from jax.experimental import pallas as pl
from jax.experimental.pallas import tpu as pltpu
from jax.experimental.pallas import tpu_sc as plsc

F32 = jnp.float32
BF16 = jnp.bfloat16

NORM_EPS = 1e-6
GM_GROUPS = 8
GM_CHUNK = 128
MLA_HEADS = 8
MLA_Q_RANK = 384
MLA_KV_RANK = 256
MLA_NOPE = 128
MLA_ROPE = 64
MLA_V = 128
MLA_DK = MLA_NOPE + MLA_ROPE
ROPE_THETA = 10000.0
N_EXPERTS = 32
TOP_K = 4
SWIGLU_LIMIT = 7.0
SWIGLU_ALPHA = 1.702
LOG2_E = 1.4426950408889634

LANES = 128
HALO = 16
VMEM_LIMIT = 56 * 1024 * 1024

TM_GMLP = 256
TM_CONV = 512
TM_MLA = 256
TM_ROUTER = 512
TM_OUT = 512
TQ_ATTN = 256
TK_ATTN = 1024
MOE_ROWS = 256


def _params(*sem):
    return pltpu.CompilerParams(dimension_semantics=sem, vmem_limit_bytes=VMEM_LIMIT)


def _const_spec(shape):
    nd = len(shape)
    return pl.BlockSpec(shape, lambda *_: (0,) * nd)


def _norm_mod(x, g, shift, scale):
    ms = jnp.mean(x * x, axis=-1, keepdims=True)
    return (x * lax.rsqrt(ms + NORM_EPS)) * g * (1.0 + scale) + shift


def _ada_kernel(c_ref, w_ref, b_ref, o_ref):
    c = c_ref[...]
    a = (c * jax.nn.sigmoid(c)).astype(BF16)
    w = w_ref[0].astype(BF16)
    o_ref[0] = jnp.dot(a, w, preferred_element_type=F32) + b_ref[0]


def _ada_call(c_pad, w_ada, b_ada):
    depth, d, n = w_ada.shape
    tn = 1536
    rows = c_pad.shape[0]
    return pl.pallas_call(
        _ada_kernel,
        grid=(depth, n // tn),
        in_specs=[
            pl.BlockSpec((rows, d), lambda l, j: (0, 0)),
            pl.BlockSpec((1, d, tn), lambda l, j: (l, 0, j)),
            pl.BlockSpec((1, 1, tn), lambda l, j: (l, 0, j)),
        ],
        out_specs=pl.BlockSpec((1, rows, tn), lambda l, j: (l, 0, j)),
        out_shape=jax.ShapeDtypeStruct((depth, rows, n), F32),
        compiler_params=_params("arbitrary", "arbitrary"),
        name="adaln_mod",
    )(c_pad, w_ada, b_ada.reshape(depth, 1, n))


def _gmlp_kernel(x_ref, mod_ref, g_ref, win_ref, gv_ref, ws_ref, bs_ref, wout_ref,
                 o_ref, gated_ref):
    x = x_ref[...]
    tm = x.shape[0]
    width = gv_ref.shape[1]
    gw = width // GM_GROUPS
    h = _norm_mod(x, g_ref[...], mod_ref[0, 0:1, :], mod_ref[0, 1:2, :]).astype(BF16)
    uv = jax.nn.gelu(jnp.dot(h, win_ref[...], preferred_element_type=F32))
    u = uv[:, :width]
    v = uv[:, width:]
    v = v * lax.rsqrt(jnp.mean(v * v, axis=-1, keepdims=True) + NORM_EPS) * gv_ref[...]
    vb = v.astype(BF16)
    bs = bs_ref[...]
    for c in range(tm // GM_CHUNK):
        r0 = c * GM_CHUNK
        for g in range(GM_GROUPS):
            c0 = g * gw
            vm = jnp.dot(ws_ref[g], vb[r0:r0 + GM_CHUNK, c0:c0 + gw],
                         preferred_element_type=F32) + bs[:, g:g + 1]
            gated_ref[r0:r0 + GM_CHUNK, c0:c0 + gw] = (
                u[r0:r0 + GM_CHUNK, c0:c0 + gw] * vm).astype(BF16)
    out = jnp.dot(gated_ref[...], wout_ref[...], preferred_element_type=F32)
    o_ref[...] = x + mod_ref[0, 2:3, :] * out


def _gmlp_call(x, mod, g, w_in, g_v, w_s, b_s, w_out, seg):
    t, d = x.shape
    tm = TM_GMLP
    width = g_v.shape[-1]
    return pl.pallas_call(
        _gmlp_kernel,
        grid=(t // tm,),
        in_specs=[
            pl.BlockSpec((tm, d), lambda i: (i, 0)),
            pl.BlockSpec((1, 6, d), lambda i: ((i * tm) // seg, 0, 0)),
            _const_spec((1, d)),
            _const_spec(w_in.shape),
            _const_spec((1, width)),
            _const_spec(w_s.shape),
            _const_spec((GM_CHUNK, GM_GROUPS)),
            _const_spec(w_out.shape),
        ],
        out_specs=pl.BlockSpec((tm, d), lambda i: (i, 0)),
        out_shape=jax.ShapeDtypeStruct((t, d), F32),
        scratch_shapes=[pltpu.VMEM((tm, width), BF16)],
        compiler_params=_params("parallel"),
        name="gmlp_mixer",
    )(x, mod, g.reshape(1, d), w_in.astype(BF16), g_v.reshape(1, width),
      w_s.astype(BF16), b_s.T, w_out.astype(BF16))


def _conv_kernel(start_ref, end_ref, x_ref, xp_ref, xn_ref, mod_ref, g_ref, win_ref,
                 wc_ref, wout_ref, o_ref, h_ref, z_ref):
    i = pl.program_id(0)
    x = x_ref[...]
    tm, d = x.shape
    g = g_ref[...]
    sh = mod_ref[0, 0:1, :]
    sc = mod_ref[0, 1:2, :]
    h_ref[0:HALO, :] = _norm_mod(xp_ref[...], g, sh, sc).astype(BF16)
    h_ref[HALO:HALO + tm, :] = _norm_mod(x, g, sh, sc).astype(BF16)
    h_ref[HALO + tm:, :] = _norm_mod(xn_ref[...], g, sh, sc).astype(BF16)
    bcx = jnp.dot(h_ref[...], win_ref[...], preferred_element_type=F32)
    w = bcx.shape[1] // 3
    bg = bcx[HALO:HALO + tm, :w]
    z_ref[...] = bcx[:, w:2 * w] * bcx[:, 2 * w:]
    keep_prev = (start_ref[i] == 0).astype(F32)
    keep_next = (end_ref[i] == 0).astype(F32)
    z_ref[0:HALO, :] = z_ref[0:HALO, :] * keep_prev
    z_ref[HALO + tm:, :] = z_ref[HALO + tm:, :] * keep_next
    y = (wc_ref[0:1, :] * z_ref[HALO - 1:HALO - 1 + tm, :]
         + wc_ref[1:2, :] * z_ref[HALO:HALO + tm, :]
         + wc_ref[2:3, :] * z_ref[HALO + 1:HALO + 1 + tm, :])
    out = jnp.dot((bg * y).astype(BF16), wout_ref[...], preferred_element_type=F32)
    o_ref[...] = x + mod_ref[0, 2:3, :] * out


def _conv_call(x, mod, g, w_in, w_conv, w_out, seg, tile_start, tile_end):
    t, d = x.shape
    tm = TM_CONV
    w = w_conv.shape[-1]
    hb = tm // HALO
    last = t // HALO - 1
    grid_spec = pltpu.PrefetchScalarGridSpec(
        num_scalar_prefetch=2,
        grid=(t // tm,),
        in_specs=[
            pl.BlockSpec((tm, d), lambda i, *_: (i, 0)),
            pl.BlockSpec((HALO, d), lambda i, *_: (jnp.maximum(i * hb - 1, 0), 0)),
            pl.BlockSpec((HALO, d), lambda i, *_: (jnp.minimum((i + 1) * hb, last), 0)),
            pl.BlockSpec((1, 6, d), lambda i, *_: ((i * tm) // seg, 0, 0)),
            _const_spec((1, d)),
            _const_spec(w_in.shape),
            _const_spec(w_conv.shape),
            _const_spec(w_out.shape),
        ],
        out_specs=pl.BlockSpec((tm, d), lambda i, *_: (i, 0)),
        scratch_shapes=[pltpu.VMEM((tm + 2 * HALO, d), BF16),
                        pltpu.VMEM((tm + 2 * HALO, w), F32)],
    )
    return pl.pallas_call(
        _conv_kernel,
        grid_spec=grid_spec,
        out_shape=jax.ShapeDtypeStruct((t, d), F32),
        compiler_params=_params("parallel"),
        name="shortconv_mixer",
    )(tile_start, tile_end, x, x, x, mod, g.reshape(1, d), w_in.astype(BF16),
      w_conv, w_out.astype(BF16))


QH = 3 * LANES
KH = 2 * LANES


def _mla_pre_kernel(pos_ref, x_ref, mod_ref, g_ref, wd_ref, gq_ref, gkv_ref, wuq_ref,
                    wukv_ref, gh_ref, cos_ref, sin_ref, q_ref, k_ref, v_ref):
    del pos_ref
    x = x_ref[...]
    h = _norm_mod(x, g_ref[...], mod_ref[0, 0:1, :], mod_ref[0, 1:2, :]).astype(BF16)
    lat = jnp.dot(h, wd_ref[...], preferred_element_type=F32)
    ql = lat[:, :MLA_Q_RANK]
    kvl = lat[:, MLA_Q_RANK:MLA_Q_RANK + MLA_KV_RANK]
    pe = lat[:, MLA_Q_RANK + MLA_KV_RANK:MLA_Q_RANK + MLA_KV_RANK + LANES]
    pe_sw = lat[:, MLA_Q_RANK + MLA_KV_RANK + LANES:]
    ql = ql * lax.rsqrt(jnp.mean(ql * ql, axis=-1, keepdims=True) + NORM_EPS) * gq_ref[...]
    kvl = kvl * lax.rsqrt(jnp.mean(kvl * kvl, axis=-1, keepdims=True) + NORM_EPS) * gkv_ref[...]
    q = jnp.dot(ql.astype(BF16), wuq_ref[...], preferred_element_type=F32)
    kv = jnp.dot(kvl.astype(BF16), wukv_ref[...], preferred_element_type=F32)
    cos = cos_ref[...]
    sin = sin_ref[...]
    gqn, gqr, gqs = gh_ref[0:1, :], gh_ref[1:2, :], gh_ref[2:3, :]
    gkn, gkr, gks = gh_ref[3:4, :], gh_ref[4:5, :], gh_ref[5:6, :]
    pe_ss = jnp.sum(pe * pe, axis=-1, keepdims=True)
    qscale = MLA_DK ** -0.5 * LOG2_E
    for hd in range(MLA_HEADS):
        qn = q[:, hd * QH:hd * QH + LANES]
        qr = q[:, hd * QH + LANES:hd * QH + 2 * LANES]
        qs = q[:, hd * QH + 2 * LANES:(hd + 1) * QH]
        ss = jnp.sum(qn * qn, axis=-1, keepdims=True) + jnp.sum(qr * qr, axis=-1, keepdims=True)
        rinv = lax.rsqrt(ss * (1.0 / MLA_DK) + NORM_EPS) * qscale
        q_ref[hd, :, 0:LANES] = (qn * rinv * gqn).astype(BF16)
        q_ref[hd, :, LANES:] = ((qr * rinv * gqr) * cos + (qs * rinv * gqs) * sin).astype(BF16)
        kn = kv[:, hd * 2 * LANES:hd * 2 * LANES + LANES]
        ss = jnp.sum(kn * kn, axis=-1, keepdims=True) + pe_ss
        rinv = lax.rsqrt(ss * (1.0 / MLA_DK) + NORM_EPS)
        k_ref[hd, :, 0:LANES] = (kn * rinv * gkn).astype(BF16)
        k_ref[hd, :, LANES:] = ((pe * rinv * gkr) * cos + (pe_sw * rinv * gks) * sin).astype(BF16)
        v_ref[hd] = kv[:, hd * 2 * LANES + LANES:(hd + 1) * 2 * LANES].T.astype(BF16)


def _mla_prepare_weights(w_down, w_uq, w_ukv, g_qh, g_kh):
    half = MLA_ROPE // 2
    pad = LANES - MLA_ROPE

    def swap(a):
        return jnp.concatenate([a[..., half:], a[..., :half]], axis=-1)

    def pad_lanes(a):
        return jnp.pad(a, [(0, 0)] * (a.ndim - 1) + [(0, pad)])

    d = w_down.shape[0]
    pe_w = w_down[:, MLA_Q_RANK + MLA_KV_RANK:]
    wd = jnp.concatenate([w_down[:, :MLA_Q_RANK + MLA_KV_RANK],
                          pad_lanes(pe_w), pad_lanes(swap(pe_w))], axis=1)
    uq = w_uq.reshape(MLA_Q_RANK, MLA_HEADS, MLA_DK)
    uq_r = uq[..., MLA_NOPE:]
    uq = jnp.concatenate([uq[..., :MLA_NOPE], pad_lanes(uq_r), pad_lanes(swap(uq_r))], axis=-1)
    uq = uq.reshape(MLA_Q_RANK, MLA_HEADS * QH)

    def gains(gv):
        r = gv[MLA_NOPE:]
        return [gv[:MLA_NOPE], pad_lanes(r), pad_lanes(swap(r))]

    gh = jnp.stack(gains(g_qh) + gains(g_kh) + [jnp.zeros((LANES,), F32)] * 2)
    return wd.astype(BF16), uq.astype(BF16), w_ukv.astype(BF16), gh


def _rope_tables(s):
    half = MLA_ROPE // 2
    inv = 1.0 / (ROPE_THETA ** (jnp.arange(0, MLA_ROPE, 2, dtype=F32) / MLA_ROPE))
    ang = jnp.arange(s, dtype=F32)[:, None] * inv[None, :]
    cos, sin = jnp.cos(ang), jnp.sin(ang)
    zeros = jnp.zeros((s, LANES - MLA_ROPE), F32)
    del half
    return (jnp.concatenate([cos, cos, zeros], axis=1),
            jnp.concatenate([-sin, sin, zeros], axis=1))


def _mla_pre_call(x, mod, g, wd, g_q_lat, g_kv_lat, uq, ukv, gh, cos, sin, seg, tile_pos):
    t, d = x.shape
    tm = TM_MLA
    grid_spec = pltpu.PrefetchScalarGridSpec(
        num_scalar_prefetch=1,
        grid=(t // tm,),
        in_specs=[
            pl.BlockSpec((tm, d), lambda i, p: (i, 0)),
            pl.BlockSpec((1, 6, d), lambda i, p: ((i * tm) // seg, 0, 0)),
            _const_spec((1, d)),
            _const_spec(wd.shape),
            _const_spec((1, MLA_Q_RANK)),
            _const_spec((1, MLA_KV_RANK)),
            _const_spec(uq.shape),
            _const_spec(ukv.shape),
            _const_spec(gh.shape),
            pl.BlockSpec((tm, LANES), lambda i, p: (p[i], 0)),
            pl.BlockSpec((tm, LANES), lambda i, p: (p[i], 0)),
        ],
        out_specs=[
            pl.BlockSpec((MLA_HEADS, tm, KH), lambda i, p: (0, i, 0)),
            pl.BlockSpec((MLA_HEADS, tm, KH), lambda i, p: (0, i, 0)),
            pl.BlockSpec((MLA_HEADS, MLA_V, tm), lambda i, p: (0, 0, i)),
        ],
    )
    return pl.pallas_call(
        _mla_pre_kernel,
        grid_spec=grid_spec,
        out_shape=[jax.ShapeDtypeStruct((MLA_HEADS, t, KH), BF16),
                   jax.ShapeDtypeStruct((MLA_HEADS, t, KH), BF16),
                   jax.ShapeDtypeStruct((MLA_HEADS, MLA_V, t), BF16)],
        compiler_params=_params("parallel"),
        name="mla_qkv",
    )(tile_pos, x, mod, g.reshape(1, d), wd, g_q_lat.reshape(1, -1), g_kv_lat.reshape(1, -1),
      uq, ukv, gh, cos, sin)


def _attn_kernel(q_ref, k_ref, vt_ref, o_ref, s_buf, p_buf, m_ref, l_ref, a_ref, acc_ref):
    tk = TK_ATTN
    n = k_ref.shape[1] // tk
    nt = (((1,), (1,)), ((), ()))

    def scores(j):
        off = pl.multiple_of(j * tk, tk)
        return lax.dot_general(k_ref[0, pl.ds(off, tk), :], q_ref[0], nt,
                               preferred_element_type=F32)

    def values(j, slot):
        off = pl.multiple_of(j * tk, tk)
        return jnp.dot(vt_ref[0, :, pl.ds(off, tk)], p_buf[slot],
                       preferred_element_type=F32)

    def step(j, cur, nxt):
        s_buf[nxt] = scores(jnp.minimum(j + 1, n - 1))
        acc_ref[...] = a_ref[...] * acc_ref[...] + values(jnp.maximum(j - 1, 0), nxt)
        s = s_buf[cur]
        m_old = m_ref[...]
        m_new = jnp.maximum(m_old, jnp.max(s, axis=0, keepdims=True))
        alpha = jnp.exp2(m_old - m_new)
        p = jnp.exp2(s - m_new)
        l_ref[...] = alpha * l_ref[...] + jnp.sum(p, axis=0, keepdims=True)
        m_ref[...] = m_new
        a_ref[...] = alpha
        p_buf[cur] = p.astype(BF16)

    s_buf[0] = scores(0)
    p_buf[1] = jnp.zeros(p_buf.shape[1:], BF16)
    m_ref[...] = jnp.full(m_ref.shape, -jnp.inf, F32)
    l_ref[...] = jnp.zeros(l_ref.shape, F32)
    a_ref[...] = jnp.ones(a_ref.shape, F32)
    acc_ref[...] = jnp.zeros(acc_ref.shape, F32)

    def pair(jj, _):
        step(2 * jj, 0, 1)
        step(2 * jj + 1, 1, 0)
        return 0

    lax.fori_loop(0, n // 2, pair, 0)
    acc = a_ref[...] * acc_ref[...] + values(n - 1, 1)
    o_ref[...] = (acc / l_ref[...]).T.astype(BF16)


def _attn_call(q, k, vt, tok0, n_seq, s_len):
    tq, tk = TQ_ATTN, TK_ATTN
    assert s_len % (2 * tk) == 0
    nq = s_len // tq
    qb0 = tok0 // tq
    sb0 = tok0 // s_len
    return pl.pallas_call(
        _attn_kernel,
        grid=(n_seq, MLA_HEADS, nq),
        in_specs=[
            pl.BlockSpec((1, tq, KH), lambda s, h, i: (h, qb0 + s * nq + i, 0)),
            pl.BlockSpec((1, s_len, KH), lambda s, h, i: (h, sb0 + s, 0)),
            pl.BlockSpec((1, MLA_V, s_len), lambda s, h, i: (h, 0, sb0 + s)),
        ],
        out_specs=pl.BlockSpec((tq, MLA_V), lambda s, h, i: (s * nq + i, h)),
        out_shape=jax.ShapeDtypeStruct((n_seq * s_len, MLA_HEADS * MLA_V), BF16),
        scratch_shapes=[pltpu.VMEM((2, tk, tq), F32), pltpu.VMEM((2, tk, tq), BF16),
                        pltpu.VMEM((1, tq), F32), pltpu.VMEM((1, tq), F32),
                        pltpu.VMEM((1, tq), F32), pltpu.VMEM((MLA_V, tq), F32)],
        compiler_params=_params("parallel", "parallel", "arbitrary"),
        name="mla_attention",
    )(q, k, vt)


def _proj_res_kernel(x_ref, o_ref, mod_ref, w_ref, out_ref):
    out = jnp.dot(o_ref[...], w_ref[...], preferred_element_type=F32)
    out_ref[...] = x_ref[...] + mod_ref[0, 2:3, :] * out


def _proj_res_call(x, o, mod, w_o, seg):
    t, d = x.shape
    tm = TM_OUT
    return pl.pallas_call(
        _proj_res_kernel,
        grid=(t // tm,),
        in_specs=[
            pl.BlockSpec((tm, d), lambda i: (i, 0)),
            pl.BlockSpec((tm, o.shape[1]), lambda i: (i, 0)),
            pl.BlockSpec((1, 6, d), lambda i: ((i * tm) // seg, 0, 0)),
            _const_spec(w_o.shape),
        ],
        out_specs=pl.BlockSpec((tm, d), lambda i: (i, 0)),
        out_shape=jax.ShapeDtypeStruct((t, d), F32),
        compiler_params=_params("parallel"),
        name="mla_out_proj",
    )(x, o, mod, w_o.astype(BF16))


def _router_kernel(x_ref, mod_ref, g_ref, wr_ref, br_ref,
                   h_ref, ids_ref, gates_ref, rank_ref, gcol_ref, cnt_ref, carry_ref):
    i = pl.program_id(0)

    @pl.when(i == 0)
    def _():
        carry_ref[...] = jnp.zeros_like(carry_ref)

    x = x_ref[...]
    tm = x.shape[0]
    h = _norm_mod(x, g_ref[...], mod_ref[0, 3:4, :], mod_ref[0, 4:5, :])
    h_ref[...] = h
    logits = lax.dot_general(wr_ref[...], h, (((1,), (1,)), ((), ())),
                             precision=lax.Precision.HIGHEST,
                             preferred_element_type=F32) + br_ref[...]
    eidx = lax.broadcasted_iota(jnp.int32, logits.shape, 0)
    lg = logits
    sel = jnp.zeros(logits.shape, F32)
    vals, ids = [], []
    for _ in range(TOP_K):
        m = jnp.max(lg, axis=0, keepdims=True)
        idx = jnp.min(jnp.where(lg == m, eidx, N_EXPERTS), axis=0, keepdims=True)
        onehot = eidx == idx
        vals.append(m)
        ids.append(idx)
        lg = jnp.where(onehot, -jnp.inf, lg)
        sel = jnp.where(onehot, 1.0, sel)
    ex = [jnp.exp(v - vals[0]) for v in vals]
    den = ex[0] + ex[1] + ex[2] + ex[3]
    gates = [e / den for e in ex]
    r_io = lax.broadcasted_iota(jnp.int32, (tm, tm), 0)
    c_io = lax.broadcasted_iota(jnp.int32, (tm, tm), 1)
    before = jnp.where(r_io < c_io, 1.0, 0.0).astype(BF16)
    cum = jnp.dot(sel.astype(BF16), before, preferred_element_type=F32) + carry_ref[:, 0:1]
    for k in range(TOP_K):
        rk = jnp.sum(jnp.where(eidx == ids[k], cum, 0.0), axis=0, keepdims=True)
        rank_ref[k:k + 1, :] = rk.astype(jnp.int32)
        ids_ref[k:k + 1, :] = ids[k]
        gates_ref[k:k + 1, :] = gates[k]
    g8 = jnp.concatenate(gates + [jnp.zeros((LANES - TOP_K, tm), F32)], axis=0)
    gcol_ref[...] = g8.T
    carry_ref[...] = carry_ref[...] + jnp.sum(sel, axis=1, keepdims=True)
    cnt_ref[...] = carry_ref[...]


def _router_call(x, mod, g, w_r, b_r, seg):
    t, d = x.shape
    tm = TM_ROUTER
    e = w_r.shape[1]
    return pl.pallas_call(
        _router_kernel,
        grid=(t // tm,),
        in_specs=[
            pl.BlockSpec((tm, d), lambda i: (i, 0)),
            pl.BlockSpec((1, 6, d), lambda i: ((i * tm) // seg, 0, 0)),
            _const_spec((1, d)),
            _const_spec((e, d)),
            _const_spec((e, 1)),
        ],
        out_specs=[
            pl.BlockSpec((tm, d), lambda i: (i, 0)),
            pl.BlockSpec((TOP_K, tm), lambda i: (0, i)),
            pl.BlockSpec((TOP_K, tm), lambda i: (0, i)),
            pl.BlockSpec((TOP_K, tm), lambda i: (0, i)),
            pl.BlockSpec((tm, LANES), lambda i: (i, 0)),
            _const_spec((e, LANES)),
        ],
        out_shape=[
            jax.ShapeDtypeStruct((t, d), F32),
            jax.ShapeDtypeStruct((TOP_K, t), jnp.int32),
            jax.ShapeDtypeStruct((TOP_K, t), F32),
            jax.ShapeDtypeStruct((TOP_K, t), jnp.int32),
            jax.ShapeDtypeStruct((t, LANES), F32),
            jax.ShapeDtypeStruct((e, LANES), F32),
        ],
        scratch_shapes=[pltpu.VMEM((e, LANES), F32)],
        compiler_params=_params("arbitrary"),
        name="moe_router",
    )(x, mod, g.reshape(1, d), w_r.T, b_r.reshape(e, 1))


SC_CORES = 2
SC_SUBCORES = 16
SC_WORKERS = SC_CORES * SC_SUBCORES
SC_CHUNK = 32


def _dispatch_rows(h, dest, pad_idx, rows):
    t, d = h.shape
    top_k = dest.shape[0]
    per_w = t // SC_WORKERS
    n_ch = per_w // SC_CHUNK
    pad_ch = pad_idx.shape[0] // (SC_WORKERS * SC_CHUNK)
    assert n_ch * SC_CHUNK * SC_WORKERS == t
    assert pad_ch * SC_CHUNK * SC_WORKERS == pad_idx.shape[0]
    assert top_k * t + pad_idx.shape[0] == rows
    mesh = plsc.VectorSubcoreMesh(core_axis_name="c", subcore_axis_name="s")

    @functools.partial(
        pl.kernel, mesh=mesh,
        out_type=jax.ShapeDtypeStruct((rows, d), h.dtype),
        scratch_types=[pltpu.VMEM((top_k * n_ch, SC_CHUNK), jnp.int32),
                       pltpu.VMEM((pad_ch, SC_CHUNK), jnp.int32),
                       pltpu.VMEM((SC_CHUNK, d), h.dtype),
                       pltpu.SemaphoreType.DMA],
        name="sc_row_dispatch",
    )
    def dispatch(h_hbm, dest_hbm, pad_hbm, zero_hbm, out_hbm, idx_v, pad_v, rows_v, sem):
        wid = lax.axis_index("s") * SC_CORES + lax.axis_index("c")
        base = wid * per_w
        pltpu.sync_copy(dest_hbm.at[wid], idx_v)
        pltpu.sync_copy(pad_hbm.at[wid], pad_v)

        @pl.loop(0, n_ch)
        def _(j):
            pltpu.sync_copy(h_hbm.at[pl.ds(base + j * SC_CHUNK, SC_CHUNK)], rows_v)
            copies = [pltpu.async_copy(rows_v, out_hbm.at[idx_v.at[k * n_ch + j]], sem)
                      for k in range(top_k)]
            for c in copies:
                c.wait()

        pltpu.sync_copy(zero_hbm, rows_v)

        @pl.loop(0, pad_ch)
        def _(p):
            pltpu.sync_copy(rows_v, out_hbm.at[pad_v.at[p]])

    dest_w = dest.reshape(top_k, SC_WORKERS, n_ch, SC_CHUNK).transpose(1, 0, 2, 3)
    return dispatch(h, dest_w.reshape(SC_WORKERS, top_k * n_ch, SC_CHUNK),
                    pad_idx.reshape(SC_WORKERS, pad_ch, SC_CHUNK),
                    jnp.zeros((SC_CHUNK, d), h.dtype))


def _gather_rows(src, idx):
    n = idx.shape[0]
    d = src.shape[1]
    per_w = n // SC_WORKERS
    n_ch = per_w // SC_CHUNK
    assert per_w * SC_WORKERS == n and n_ch * SC_CHUNK == per_w
    mesh = plsc.VectorSubcoreMesh(core_axis_name="c", subcore_axis_name="s")

    @functools.partial(
        pl.kernel, mesh=mesh,
        out_type=jax.ShapeDtypeStruct((n, d), src.dtype),
        scratch_types=[pltpu.VMEM((n_ch, SC_CHUNK), jnp.int32),
                       pltpu.VMEM((SC_CHUNK, d), src.dtype),
                       pltpu.SemaphoreType.DMA],
        name="sc_row_gather",
    )
    def gather(src_hbm, idx_hbm, out_hbm, idx_v, rows_v, sem):
        wid = lax.axis_index("s") * SC_CORES + lax.axis_index("c")
        base = wid * per_w
        pltpu.sync_copy(idx_hbm.at[wid], idx_v)

        @pl.loop(0, n_ch)
        def _(j):
            pltpu.async_copy(src_hbm.at[idx_v.at[j]], rows_v, sem).wait()
            pltpu.sync_copy(rows_v, out_hbm.at[pl.ds(base + j * SC_CHUNK, SC_CHUNK)])

    return gather(src, idx.reshape(SC_WORKERS, n_ch, SC_CHUNK))


def _expert_kernel(blk_e_ref, nused_ref, x_ref, wgu_ref, bgu_ref, wdn_ref, bdn_ref,
                   o_ref, wgu_bf, wdn_bf):
    i = pl.program_id(0)
    e = blk_e_ref[i]
    e_prev = blk_e_ref[jnp.maximum(i - 1, 0)]
    active = i < nused_ref[0]
    d_e = wdn_ref.shape[2]

    @pl.when(active & ((i == 0) | (e != e_prev)))
    def _():
        step = 128
        for r in range(0, wgu_ref.shape[2], step):
            wgu_bf[r:r + step, :] = wgu_ref[0, 0, r:r + step, :].astype(BF16)
        for r in range(0, d_e, step):
            wdn_bf[r:r + step, :] = wdn_ref[0, 0, r:r + step, :].astype(BF16)

    @pl.when(active)
    def _():
        xb = x_ref[...].astype(BF16)
        gu = jnp.dot(xb, wgu_bf[...], preferred_element_type=F32) + bgu_ref[0, 0]
        gt = jnp.minimum(gu[:, :d_e], SWIGLU_LIMIT)
        up = jnp.clip(gu[:, d_e:], -SWIGLU_LIMIT, SWIGLU_LIMIT)
        a = (up + 1.0) * (gt * jax.nn.sigmoid(SWIGLU_ALPHA * gt))
        o_ref[...] = jnp.dot(a.astype(BF16), wdn_bf[...],
                             preferred_element_type=F32) + bdn_ref[0, 0]

    @pl.when(jnp.logical_not(active))
    def _():
        o_ref[...] = jnp.zeros_like(o_ref)


def _expert_call(xg, blk_e, n_used, w_gu, b_gu, w_dn, b_dn, layer):
    rows, d = xg.shape
    nb = rows // MOE_ROWS
    _, n_e, _, two_de = w_gu.shape
    d_e = two_de // 2
    grid_spec = pltpu.PrefetchScalarGridSpec(
        num_scalar_prefetch=2,
        grid=(nb,),
        in_specs=[
            pl.BlockSpec((MOE_ROWS, d), lambda i, be, nu: (i, 0)),
            pl.BlockSpec((1, 1, d, two_de), lambda i, be, nu: (layer, be[i], 0, 0)),
            pl.BlockSpec((1, 1, 1, two_de), lambda i, be, nu: (layer, be[i], 0, 0)),
            pl.BlockSpec((1, 1, d_e, d), lambda i, be, nu: (layer, be[i], 0, 0)),
            pl.BlockSpec((1, 1, 1, d), lambda i, be, nu: (layer, be[i], 0, 0)),
        ],
        out_specs=pl.BlockSpec((MOE_ROWS, d), lambda i, be, nu: (i, 0)),
        scratch_shapes=[pltpu.VMEM((d, two_de), BF16), pltpu.VMEM((d_e, d), BF16)],
    )
    return pl.pallas_call(
        _expert_kernel,
        grid_spec=grid_spec,
        out_shape=jax.ShapeDtypeStruct((rows, d), F32),
        compiler_params=_params("arbitrary"),
        name="moe_experts",
    )(blk_e, n_used, xg, w_gu, b_gu.reshape(b_gu.shape[0], n_e, 1, two_de),
      w_dn, b_dn.reshape(b_dn.shape[0], n_e, 1, d))


def _combine_kernel(x_ref, y_ref, gcol_ref, mod_ref, o_ref):
    gc = gcol_ref[...]
    acc = gc[:, 0:1] * y_ref[0]
    for k in range(1, TOP_K):
        acc = acc + gc[:, k:k + 1] * y_ref[k]
    o_ref[...] = x_ref[...] + mod_ref[0, 5:6, :] * acc


def _combine_call(x, yk, gcol, mod, seg):
    t, d = x.shape
    tm = TM_OUT
    return pl.pallas_call(
        _combine_kernel,
        grid=(t // tm,),
        in_specs=[
            pl.BlockSpec((tm, d), lambda i: (i, 0)),
            pl.BlockSpec((TOP_K, tm, d), lambda i: (0, i, 0)),
            pl.BlockSpec((tm, LANES), lambda i: (i, 0)),
            pl.BlockSpec((1, 6, d), lambda i: ((i * tm) // seg, 0, 0)),
        ],
        out_specs=pl.BlockSpec((tm, d), lambda i: (i, 0)),
        out_shape=jax.ShapeDtypeStruct((t, d), F32),
        compiler_params=_params("parallel"),
        name="moe_combine",
    )(x, yk, gcol, mod)


def _moe_layer(x, mod, g, w_r, b_r, w_gu, b_gu, w_dn, b_dn, layer, seg):
    t, d = x.shape
    h, ids, _, rank, gcol, cnt = _router_call(x, mod, g, w_r, b_r, seg)
    counts = cnt[:, 0].astype(jnp.int32)
    padded = (counts + MOE_ROWS - 1) // MOE_ROWS * MOE_ROWS
    pad_end = jnp.cumsum(padded)
    pad_start = pad_end - padded
    e_ar = jnp.arange(N_EXPERTS, dtype=jnp.int32)
    dest = rank + jnp.sum(jnp.where(ids[..., None] == e_ar, pad_start, 0), axis=-1)
    nb = -(-(t * TOP_K) // MOE_ROWS) + N_EXPERTS
    rows = nb * MOE_ROWS
    slack = padded - counts
    slack_end = jnp.cumsum(slack)
    r = jnp.arange(rows - t * TOP_K, dtype=jnp.int32)
    owner = jnp.sum((slack_end[None, :] <= r[:, None]).astype(jnp.int32), axis=1)
    first = pad_start + counts - (slack_end - slack)
    in_expert = r + jnp.sum(jnp.where(owner[:, None] == e_ar, first, 0), axis=-1)
    pad_idx = jnp.where(owner < N_EXPERTS, in_expert, pad_end[-1] + r - slack_end[-1])
    blk_row = jnp.arange(nb, dtype=jnp.int32) * MOE_ROWS
    blk_e = jnp.minimum(jnp.sum((pad_end[None, :] <= blk_row[:, None]).astype(jnp.int32), axis=1),
                        N_EXPERTS - 1)
    n_used = (pad_end[-1:] // MOE_ROWS).astype(jnp.int32)
    xg = _dispatch_rows(h, dest, pad_idx, rows)
    yb = _expert_call(xg, blk_e, n_used, w_gu, b_gu, w_dn, b_dn, layer)
    yk = _gather_rows(yb, dest.reshape(-1)).reshape(TOP_K, t, d)
    return _combine_call(x, yk, gcol, mod, seg)


def kernel(x_prompt, x_sample, c_prompt, c_sample, g_mix, g_ffn, w_ada, b_ada, gm_w_in, gm_g_v, gm_w_s, gm_b_s, gm_w_out, sc_w_in, sc_w_conv, sc_w_out, mla_w_down, mla_g_q_lat, mla_g_kv_lat, mla_w_uq, mla_w_ukv, mla_g_qh, mla_g_kh, mla_w_o, moe_w_router, moe_b_router, moe_w_gu, moe_b_gu, moe_w_dn, moe_b_dn):
    bp, sp, d = x_prompt.shape
    bs, ss, _ = x_sample.shape
    tp, ts = bp * sp, bs * ss
    t = tp + ts
    depth = g_mix.shape[0]
    seg = min(sp, ss)
    assert sp % seg == 0 and ss % seg == 0

    x = jnp.concatenate([x_prompt.reshape(tp, d), x_sample.reshape(ts, d)], axis=0)

    n_seq = bp + bs
    c_all = jnp.concatenate([c_prompt, c_sample], axis=0)
    c_pad = jnp.pad(c_all, ((0, (-n_seq) % 8), (0, 0)))
    mod_all = _ada_call(c_pad, w_ada, b_ada)[:, :n_seq].reshape(depth, n_seq, 6, d)
    seg_seq = jnp.concatenate([jnp.repeat(jnp.arange(bp), sp // seg),
                               bp + jnp.repeat(jnp.arange(bs), ss // seg)])
    mod_seg = mod_all[:, seg_seq]

    def tile_meta(tm):
        t0 = jnp.arange(t // tm, dtype=jnp.int32) * tm
        pos = jnp.where(t0 < tp, t0 % sp, (t0 - tp) % ss)
        slen = jnp.where(t0 < tp, sp, ss)
        return pos, slen

    for i in range(depth):
        mod = mod_seg[i]
        kind, j = i % 3, i // 3
        if kind == 0:
            x = _gmlp_call(x, mod, g_mix[i], gm_w_in[j], gm_g_v[j], gm_w_s[j], gm_b_s[j],
                           gm_w_out[j], seg)
        elif kind == 1:
            pos, slen = tile_meta(TM_CONV)
            x = _conv_call(x, mod, g_mix[i], sc_w_in[j], sc_w_conv[j], sc_w_out[j], seg,
                           (pos != 0).astype(jnp.int32),
                           (pos + TM_CONV != slen).astype(jnp.int32))
        else:
            wd, uq, ukv, gh = _mla_prepare_weights(mla_w_down[j], mla_w_uq[j], mla_w_ukv[j],
                                                  mla_g_qh[j], mla_g_kh[j])
            cos, sin = _rope_tables(max(sp, ss))
            pos, _ = tile_meta(TM_MLA)
            q, k, v = _mla_pre_call(x, mod, g_mix[i], wd, mla_g_q_lat[j], mla_g_kv_lat[j],
                                    uq, ukv, gh, cos, sin, seg, pos // TM_MLA)
            o = jnp.concatenate([_attn_call(q, k, v, 0, bp, sp),
                                 _attn_call(q, k, v, tp, bs, ss)], axis=0)
            x = _proj_res_call(x, o, mod, mla_w_o[j], seg)
        x = _moe_layer(x, mod, g_ffn[i], moe_w_router[i], moe_b_router[i],
                       moe_w_gu, moe_b_gu, moe_w_dn, moe_b_dn, i, seg)

    return (x[:tp].reshape(bp, sp, d), x[tp:].reshape(bs, ss, d))
```

```python
import functools

import jax
import jax.numpy as jnp
from jax import lax
from jax.experimental import pallas as pl
from jax.experimental.pallas import tpu as pltpu
from jax.experimental.pallas import tpu_sc as plsc

F32 = jnp.float32
BF16 = jnp.bfloat16

NORM_EPS = 1e-6
GM_GROUPS = 8
GM_CHUNK = 128
MLA_HEADS = 8
MLA_Q_RANK = 384
MLA_KV_RANK = 256
MLA_NOPE = 128
MLA_ROPE = 64
MLA_V = 128
MLA_DK = MLA_NOPE + MLA_ROPE
ROPE_THETA = 10000.0
N_EXPERTS = 32
TOP_K = 4
SWIGLU_LIMIT = 7.0
SWIGLU_ALPHA = 1.702
LOG2_E = 1.4426950408889634

LANES = 128
HALO = 16
VMEM_LIMIT = 56 * 1024 * 1024

TM_GMLP = 256
TM_CONV = 512
TM_MLA = 256
TM_ROUTER = 512
TM_OUT = 512
TQ_ATTN = 256
TK_ATTN = 1024
MOE_ROWS = 256


def _params(*sem):
    return pltpu.CompilerParams(dimension_semantics=sem, vmem_limit_bytes=VMEM_LIMIT)


def _const_spec(shape):
    nd = len(shape)
    return pl.BlockSpec(shape, lambda *_: (0,) * nd)


def _norm_mod(x, g, shift, scale):
    ms = jnp.mean(x * x, axis=-1, keepdims=True)
    return (x * lax.rsqrt(ms + NORM_EPS)) * g * (1.0 + scale) + shift


def _pack_rows(x):
    half = x.shape[1] // 2
    lo = pltpu.bitcast(x[:, :half].astype(BF16).astype(F32), jnp.uint32)
    hi = pltpu.bitcast(x[:, half:].astype(BF16).astype(F32), jnp.uint32)
    word = lax.shift_right_logical(lo, jnp.uint32(16)) | (hi & jnp.uint32(0xFFFF0000))
    return pltpu.bitcast(word, jnp.int32)


def _unpack_rows(w):
    u = pltpu.bitcast(w, jnp.uint32)
    lo = pltpu.bitcast(lax.shift_left(u, jnp.uint32(16)), F32)
    hi = pltpu.bitcast(u & jnp.uint32(0xFFFF0000), F32)
    return lo, hi


def _ada_kernel(c_ref, w_ref, b_ref, o_ref):
    c = c_ref[...]
    a = (c * jax.nn.sigmoid(c)).astype(BF16)
    w = w_ref[0].astype(BF16)
    o_ref[0] = jnp.dot(a, w, preferred_element_type=F32) + b_ref[0]


def _ada_call(c_pad, w_ada, b_ada):
    depth, d, n = w_ada.shape
    tn = 1536
    rows = c_pad.shape[0]
    return pl.pallas_call(
        _ada_kernel,
        grid=(depth, n // tn),
        in_specs=[
            pl.BlockSpec((rows, d), lambda l, j: (0, 0)),
            pl.BlockSpec((1, d, tn), lambda l, j: (l, 0, j)),
            pl.BlockSpec((1, 1, tn), lambda l, j: (l, 0, j)),
        ],
        out_specs=pl.BlockSpec((1, rows, tn), lambda l, j: (l, 0, j)),
        out_shape=jax.ShapeDtypeStruct((depth, rows, n), F32),
        compiler_params=_params("arbitrary", "arbitrary"),
        name="adaln_mod",
    )(c_pad, w_ada, b_ada.reshape(depth, 1, n))


def _gmlp_kernel(x_ref, mod_ref, g_ref, win_ref, gv_ref, ws_ref, bs_ref, wout_ref,
                 o_ref, gated_ref):
    x = x_ref[...]
    tm = x.shape[0]
    width = gv_ref.shape[1]
    gw = width // GM_GROUPS
    h = _norm_mod(x, g_ref[...], mod_ref[0, 0:1, :], mod_ref[0, 1:2, :]).astype(BF16)
    uv = jax.nn.gelu(jnp.dot(h, win_ref[...], preferred_element_type=F32))
    u = uv[:, :width]
    v = uv[:, width:]
    v = v * lax.rsqrt(jnp.mean(v * v, axis=-1, keepdims=True) + NORM_EPS) * gv_ref[...]
    vb = v.astype(BF16)
    bs = bs_ref[...]
    for c in range(tm // GM_CHUNK):
        r0 = c * GM_CHUNK
        for g in range(GM_GROUPS):
            c0 = g * gw
            vm = jnp.dot(ws_ref[g], vb[r0:r0 + GM_CHUNK, c0:c0 + gw],
                         preferred_element_type=F32) + bs[:, g:g + 1]
            gated_ref[r0:r0 + GM_CHUNK, c0:c0 + gw] = (
                u[r0:r0 + GM_CHUNK, c0:c0 + gw] * vm).astype(BF16)
    out = jnp.dot(gated_ref[...], wout_ref[...], preferred_element_type=F32)
    o_ref[...] = x + mod_ref[0, 2:3, :] * out


def _gmlp_call(x, mod, g, w_in, g_v, w_s, b_s, w_out, seg):
    t, d = x.shape
    tm = TM_GMLP
    width = g_v.shape[-1]
    return pl.pallas_call(
        _gmlp_kernel,
        grid=(t // tm,),
        in_specs=[
            pl.BlockSpec((tm, d), lambda i: (i, 0)),
            pl.BlockSpec((1, 6, d), lambda i: ((i * tm) // seg, 0, 0)),
            _const_spec((1, d)),
            _const_spec(w_in.shape),
            _const_spec((1, width)),
            _const_spec(w_s.shape),
            _const_spec((GM_CHUNK, GM_GROUPS)),
            _const_spec(w_out.shape),
        ],
        out_specs=pl.BlockSpec((tm, d), lambda i: (i, 0)),
        out_shape=jax.ShapeDtypeStruct((t, d), F32),
        scratch_shapes=[pltpu.VMEM((tm, width), BF16)],
        compiler_params=_params("parallel"),
        name="gmlp_mixer",
    )(x, mod, g.reshape(1, d), w_in.astype(BF16), g_v.reshape(1, width),
      w_s.astype(BF16), b_s.T, w_out.astype(BF16))


def _conv_kernel(start_ref, end_ref, x_ref, xp_ref, xn_ref, mod_ref, g_ref, win_ref,
                 wc_ref, wout_ref, o_ref, h_ref, z_ref):
    i = pl.program_id(0)
    x = x_ref[...]
    tm, d = x.shape
    g = g_ref[...]
    sh = mod_ref[0, 0:1, :]
    sc = mod_ref[0, 1:2, :]
    h_ref[0:HALO, :] = _norm_mod(xp_ref[...], g, sh, sc).astype(BF16)
    h_ref[HALO:HALO + tm, :] = _norm_mod(x, g, sh, sc).astype(BF16)
    h_ref[HALO + tm:, :] = _norm_mod(xn_ref[...], g, sh, sc).astype(BF16)
    bcx = jnp.dot(h_ref[...], win_ref[...], preferred_element_type=F32)
    w = bcx.shape[1] // 3
    bg = bcx[HALO:HALO + tm, :w]
    z_ref[...] = bcx[:, w:2 * w] * bcx[:, 2 * w:]
    keep_prev = (start_ref[i] == 0).astype(F32)
    keep_next = (end_ref[i] == 0).astype(F32)
    z_ref[0:HALO, :] = z_ref[0:HALO, :] * keep_prev
    z_ref[HALO + tm:, :] = z_ref[HALO + tm:, :] * keep_next
    y = (wc_ref[0:1, :] * z_ref[HALO - 1:HALO - 1 + tm, :]
         + wc_ref[1:2, :] * z_ref[HALO:HALO + tm, :]
         + wc_ref[2:3, :] * z_ref[HALO + 1:HALO + 1 + tm, :])
    out = jnp.dot((bg * y).astype(BF16), wout_ref[...], preferred_element_type=F32)
    o_ref[...] = x + mod_ref[0, 2:3, :] * out


def _conv_call(x, mod, g, w_in, w_conv, w_out, seg, tile_start, tile_end):
    t, d = x.shape
    tm = TM_CONV
    w = w_conv.shape[-1]
    hb = tm // HALO
    last = t // HALO - 1
    grid_spec = pltpu.PrefetchScalarGridSpec(
        num_scalar_prefetch=2,
        grid=(t // tm,),
        in_specs=[
            pl.BlockSpec((tm, d), lambda i, *_: (i, 0)),
            pl.BlockSpec((HALO, d), lambda i, *_: (jnp.maximum(i * hb - 1, 0), 0)),
            pl.BlockSpec((HALO, d), lambda i, *_: (jnp.minimum((i + 1) * hb, last), 0)),
            pl.BlockSpec((1, 6, d), lambda i, *_: ((i * tm) // seg, 0, 0)),
            _const_spec((1, d)),
            _const_spec(w_in.shape),
            _const_spec(w_conv.shape),
            _const_spec(w_out.shape),
        ],
        out_specs=pl.BlockSpec((tm, d), lambda i, *_: (i, 0)),
        scratch_shapes=[pltpu.VMEM((tm + 2 * HALO, d), BF16),
                        pltpu.VMEM((tm + 2 * HALO, w), F32)],
    )
    return pl.pallas_call(
        _conv_kernel,
        grid_spec=grid_spec,
        out_shape=jax.ShapeDtypeStruct((t, d), F32),
        compiler_params=_params("parallel"),
        name="shortconv_mixer",
    )(tile_start, tile_end, x, x, x, mod, g.reshape(1, d), w_in.astype(BF16),
      w_conv, w_out.astype(BF16))


QH = 3 * LANES
KH = 2 * LANES


def _mla_pre_kernel(pos_ref, x_ref, mod_ref, g_ref, wd_ref, gq_ref, gkv_ref, wuq_ref,
                    wukv_ref, gh_ref, cos_ref, sin_ref, q_ref, k_ref, v_ref):
    del pos_ref
    x = x_ref[...]
    h = _norm_mod(x, g_ref[...], mod_ref[0, 0:1, :], mod_ref[0, 1:2, :]).astype(BF16)
    lat = jnp.dot(h, wd_ref[...], preferred_element_type=F32)
    ql = lat[:, :MLA_Q_RANK]
    kvl = lat[:, MLA_Q_RANK:MLA_Q_RANK + MLA_KV_RANK]
    pe = lat[:, MLA_Q_RANK + MLA_KV_RANK:MLA_Q_RANK + MLA_KV_RANK + LANES]
    pe_sw = lat[:, MLA_Q_RANK + MLA_KV_RANK + LANES:]
    ql = ql * lax.rsqrt(jnp.mean(ql * ql, axis=-1, keepdims=True) + NORM_EPS) * gq_ref[...]
    kvl = kvl * lax.rsqrt(jnp.mean(kvl * kvl, axis=-1, keepdims=True) + NORM_EPS) * gkv_ref[...]
    q = jnp.dot(ql.astype(BF16), wuq_ref[...], preferred_element_type=F32)
    kv = jnp.dot(kvl.astype(BF16), wukv_ref[...], preferred_element_type=F32)
    cos = cos_ref[...]
    sin = sin_ref[...]
    gqn, gqr, gqs = gh_ref[0:1, :], gh_ref[1:2, :], gh_ref[2:3, :]
    gkn, gkr, gks = gh_ref[3:4, :], gh_ref[4:5, :], gh_ref[5:6, :]
    pe_ss = jnp.sum(pe * pe, axis=-1, keepdims=True)
    qscale = MLA_DK ** -0.5 * LOG2_E
    for hd in range(MLA_HEADS):
        qn = q[:, hd * QH:hd * QH + LANES]
        qr = q[:, hd * QH + LANES:hd * QH + 2 * LANES]
        qs = q[:, hd * QH + 2 * LANES:(hd + 1) * QH]
        ss = jnp.sum(qn * qn, axis=-1, keepdims=True) + jnp.sum(qr * qr, axis=-1, keepdims=True)
        rinv = lax.rsqrt(ss * (1.0 / MLA_DK) + NORM_EPS) * qscale
        q_ref[hd, :, 0:LANES] = (qn * rinv * gqn).astype(BF16)
        q_ref[hd, :, LANES:] = ((qr * rinv * gqr) * cos + (qs * rinv * gqs) * sin).astype(BF16)
        kn = kv[:, hd * 2 * LANES:hd * 2 * LANES + LANES]
        ss = jnp.sum(kn * kn, axis=-1, keepdims=True) + pe_ss
        rinv = lax.rsqrt(ss * (1.0 / MLA_DK) + NORM_EPS)
        k_ref[hd, :, 0:LANES] = (kn * rinv * gkn).astype(BF16)
        k_ref[hd, :, LANES:] = ((pe * rinv * gkr) * cos + (pe_sw * rinv * gks) * sin).astype(BF16)
        v_ref[hd] = kv[:, hd * 2 * LANES + LANES:(hd + 1) * 2 * LANES].T.astype(BF16)


def _mla_prepare_weights(w_down, w_uq, w_ukv, g_qh, g_kh):
    half = MLA_ROPE // 2
    pad = LANES - MLA_ROPE

    def swap(a):
        return jnp.concatenate([a[..., half:], a[..., :half]], axis=-1)

    def pad_lanes(a):
        return jnp.pad(a, [(0, 0)] * (a.ndim - 1) + [(0, pad)])

    d = w_down.shape[0]
    pe_w = w_down[:, MLA_Q_RANK + MLA_KV_RANK:]
    wd = jnp.concatenate([w_down[:, :MLA_Q_RANK + MLA_KV_RANK],
                          pad_lanes(pe_w), pad_lanes(swap(pe_w))], axis=1)
    uq = w_uq.reshape(MLA_Q_RANK, MLA_HEADS, MLA_DK)
    uq_r = uq[..., MLA_NOPE:]
    uq = jnp.concatenate([uq[..., :MLA_NOPE], pad_lanes(uq_r), pad_lanes(swap(uq_r))], axis=-1)
    uq = uq.reshape(MLA_Q_RANK, MLA_HEADS * QH)

    def gains(gv):
        r = gv[MLA_NOPE:]
        return [gv[:MLA_NOPE], pad_lanes(r), pad_lanes(swap(r))]

    gh = jnp.stack(gains(g_qh) + gains(g_kh) + [jnp.zeros((LANES,), F32)] * 2)
    return wd.astype(BF16), uq.astype(BF16), w_ukv.astype(BF16), gh


def _rope_tables(s):
    half = MLA_ROPE // 2
    inv = 1.0 / (ROPE_THETA ** (jnp.arange(0, MLA_ROPE, 2, dtype=F32) / MLA_ROPE))
    ang = jnp.arange(s, dtype=F32)[:, None] * inv[None, :]
    cos, sin = jnp.cos(ang), jnp.sin(ang)
    zeros = jnp.zeros((s, LANES - MLA_ROPE), F32)
    del half
    return (jnp.concatenate([cos, cos, zeros], axis=1),
            jnp.concatenate([-sin, sin, zeros], axis=1))


def _mla_pre_call(x, mod, g, wd, g_q_lat, g_kv_lat, uq, ukv, gh, cos, sin, seg, tile_pos):
    t, d = x.shape
    tm = TM_MLA
    grid_spec = pltpu.PrefetchScalarGridSpec(
        num_scalar_prefetch=1,
        grid=(t // tm,),
        in_specs=[
            pl.BlockSpec((tm, d), lambda i, p: (i, 0)),
            pl.BlockSpec((1, 6, d), lambda i, p: ((i * tm) // seg, 0, 0)),
            _const_spec((1, d)),
            _const_spec(wd.shape),
            _const_spec((1, MLA_Q_RANK)),
            _const_spec((1, MLA_KV_RANK)),
            _const_spec(uq.shape),
            _const_spec(ukv.shape),
            _const_spec(gh.shape),
            pl.BlockSpec((tm, LANES), lambda i, p: (p[i], 0)),
            pl.BlockSpec((tm, LANES), lambda i, p: (p[i], 0)),
        ],
        out_specs=[
            pl.BlockSpec((MLA_HEADS, tm, KH), lambda i, p: (0, i, 0)),
            pl.BlockSpec((MLA_HEADS, tm, KH), lambda i, p: (0, i, 0)),
            pl.BlockSpec((MLA_HEADS, MLA_V, tm), lambda i, p: (0, 0, i)),
        ],
    )
    return pl.pallas_call(
        _mla_pre_kernel,
        grid_spec=grid_spec,
        out_shape=[jax.ShapeDtypeStruct((MLA_HEADS, t, KH), BF16),
                   jax.ShapeDtypeStruct((MLA_HEADS, t, KH), BF16),
                   jax.ShapeDtypeStruct((MLA_HEADS, MLA_V, t), BF16)],
        compiler_params=_params("parallel"),
        name="mla_qkv",
    )(tile_pos, x, mod, g.reshape(1, d), wd, g_q_lat.reshape(1, -1), g_kv_lat.reshape(1, -1),
      uq, ukv, gh, cos, sin)


def _attn_kernel(q_ref, k_ref, vt_ref, o_ref, s_buf, p_buf, m_ref, l_ref, a_ref, acc_ref):
    tk = TK_ATTN
    n = k_ref.shape[1] // tk
    nt = (((1,), (1,)), ((), ()))

    def scores(j):
        off = pl.multiple_of(j * tk, tk)
        return lax.dot_general(k_ref[0, pl.ds(off, tk), :], q_ref[0], nt,
                               preferred_element_type=F32)

    def values(j, slot):
        off = pl.multiple_of(j * tk, tk)
        return jnp.dot(vt_ref[0, :, pl.ds(off, tk)], p_buf[slot],
                       preferred_element_type=F32)

    def step(j, cur, nxt):
        s_buf[nxt] = scores(jnp.minimum(j + 1, n - 1))
        acc_ref[...] = a_ref[...] * acc_ref[...] + values(jnp.maximum(j - 1, 0), nxt)
        s = s_buf[cur]
        m_old = m_ref[...]
        m_new = jnp.maximum(m_old, jnp.max(s, axis=0, keepdims=True))
        alpha = jnp.exp2(m_old - m_new)
        p = jnp.exp2(s - m_new)
        l_ref[...] = alpha * l_ref[...] + jnp.sum(p, axis=0, keepdims=True)
        m_ref[...] = m_new
        a_ref[...] = alpha
        p_buf[cur] = p.astype(BF16)

    s_buf[0] = scores(0)
    p_buf[1] = jnp.zeros(p_buf.shape[1:], BF16)
    m_ref[...] = jnp.full(m_ref.shape, -jnp.inf, F32)
    l_ref[...] = jnp.zeros(l_ref.shape, F32)
    a_ref[...] = jnp.ones(a_ref.shape, F32)
    acc_ref[...] = jnp.zeros(acc_ref.shape, F32)

    def pair(jj, _):
        step(2 * jj, 0, 1)
        step(2 * jj + 1, 1, 0)
        return 0

    lax.fori_loop(0, n // 2, pair, 0)
    acc = a_ref[...] * acc_ref[...] + values(n - 1, 1)
    o_ref[...] = (acc / l_ref[...]).T.astype(BF16)


def _attn_call(q, k, vt, tok0, n_seq, s_len):
    tq, tk = TQ_ATTN, TK_ATTN
    assert s_len % (2 * tk) == 0
    nq = s_len // tq
    qb0 = tok0 // tq
    sb0 = tok0 // s_len
    return pl.pallas_call(
        _attn_kernel,
        grid=(n_seq, MLA_HEADS, nq),
        in_specs=[
            pl.BlockSpec((1, tq, KH), lambda s, h, i: (h, qb0 + s * nq + i, 0)),
            pl.BlockSpec((1, s_len, KH), lambda s, h, i: (h, sb0 + s, 0)),
            pl.BlockSpec((1, MLA_V, s_len), lambda s, h, i: (h, 0, sb0 + s)),
        ],
        out_specs=pl.BlockSpec((tq, MLA_V), lambda s, h, i: (s * nq + i, h)),
        out_shape=jax.ShapeDtypeStruct((n_seq * s_len, MLA_HEADS * MLA_V), BF16),
        scratch_shapes=[pltpu.VMEM((2, tk, tq), F32), pltpu.VMEM((2, tk, tq), BF16),
                        pltpu.VMEM((1, tq), F32), pltpu.VMEM((1, tq), F32),
                        pltpu.VMEM((1, tq), F32), pltpu.VMEM((MLA_V, tq), F32)],
        compiler_params=_params("parallel", "parallel", "arbitrary"),
        name="mla_attention",
    )(q, k, vt)


def _proj_res_kernel(x_ref, o_ref, mod_ref, w_ref, out_ref):
    out = jnp.dot(o_ref[...], w_ref[...], preferred_element_type=F32)
    out_ref[...] = x_ref[...] + mod_ref[0, 2:3, :] * out


def _proj_res_call(x, o, mod, w_o, seg):
    t, d = x.shape
    tm = TM_OUT
    return pl.pallas_call(
        _proj_res_kernel,
        grid=(t // tm,),
        in_specs=[
            pl.BlockSpec((tm, d), lambda i: (i, 0)),
            pl.BlockSpec((tm, o.shape[1]), lambda i: (i, 0)),
            pl.BlockSpec((1, 6, d), lambda i: ((i * tm) // seg, 0, 0)),
            _const_spec(w_o.shape),
        ],
        out_specs=pl.BlockSpec((tm, d), lambda i: (i, 0)),
        out_shape=jax.ShapeDtypeStruct((t, d), F32),
        compiler_params=_params("parallel"),
        name="mla_out_proj",
    )(x, o, mod, w_o.astype(BF16))


def _router_kernel(x_ref, mod_ref, g_ref, wr_ref, br_ref,
                   h_ref, ids_ref, gates_ref, rank_ref, gcol_ref, cnt_ref, carry_ref):
    i = pl.program_id(0)

    @pl.when(i == 0)
    def _():
        carry_ref[...] = jnp.zeros_like(carry_ref)

    x = x_ref[...]
    tm = x.shape[0]
    h = _norm_mod(x, g_ref[...], mod_ref[0, 3:4, :], mod_ref[0, 4:5, :])
    h_ref[...] = _pack_rows(h)
    logits = lax.dot_general(wr_ref[...], h, (((1,), (1,)), ((), ())),
                             precision=lax.Precision.HIGHEST,
                             preferred_element_type=F32) + br_ref[...]
    eidx = lax.broadcasted_iota(jnp.int32, logits.shape, 0)
    lg = logits
    sel = jnp.zeros(logits.shape, F32)
    vals, ids = [], []
    for _ in range(TOP_K):
        m = jnp.max(lg, axis=0, keepdims=True)
        idx = jnp.min(jnp.where(lg == m, eidx, N_EXPERTS), axis=0, keepdims=True)
        onehot = eidx == idx
        vals.append(m)
        ids.append(idx)
        lg = jnp.where(onehot, -jnp.inf, lg)
        sel = jnp.where(onehot, 1.0, sel)
    ex = [jnp.exp(v - vals[0]) for v in vals]
    den = ex[0] + ex[1] + ex[2] + ex[3]
    gates = [e / den for e in ex]
    r_io = lax.broadcasted_iota(jnp.int32, (tm, tm), 0)
    c_io = lax.broadcasted_iota(jnp.int32, (tm, tm), 1)
    before = jnp.where(r_io < c_io, 1.0, 0.0).astype(BF16)
    cum = jnp.dot(sel.astype(BF16), before, preferred_element_type=F32) + carry_ref[:, 0:1]
    for k in range(TOP_K):
        rk = jnp.sum(jnp.where(eidx == ids[k], cum, 0.0), axis=0, keepdims=True)
        rank_ref[k:k + 1, :] = rk.astype(jnp.int32)
        ids_ref[k:k + 1, :] = ids[k]
        gates_ref[k:k + 1, :] = gates[k]
    g8 = jnp.concatenate(gates + [jnp.zeros((LANES - TOP_K, tm), F32)], axis=0)
    gcol_ref[...] = g8.T
    carry_ref[...] = carry_ref[...] + jnp.sum(sel, axis=1, keepdims=True)
    cnt_ref[...] = carry_ref[...]


def _router_call(x, mod, g, w_r, b_r, seg):
    t, d = x.shape
    tm = TM_ROUTER
    e = w_r.shape[1]
    return pl.pallas_call(
        _router_kernel,
        grid=(t // tm,),
        in_specs=[
            pl.BlockSpec((tm, d), lambda i: (i, 0)),
            pl.BlockSpec((1, 6, d), lambda i: ((i * tm) // seg, 0, 0)),
            _const_spec((1, d)),
            _const_spec((e, d)),
            _const_spec((e, 1)),
        ],
        out_specs=[
            pl.BlockSpec((tm, d // 2), lambda i: (i, 0)),
            pl.BlockSpec((TOP_K, tm), lambda i: (0, i)),
            pl.BlockSpec((TOP_K, tm), lambda i: (0, i)),
            pl.BlockSpec((TOP_K, tm), lambda i: (0, i)),
            pl.BlockSpec((tm, LANES), lambda i: (i, 0)),
            _const_spec((e, LANES)),
        ],
        out_shape=[
            jax.ShapeDtypeStruct((t, d // 2), jnp.int32),
            jax.ShapeDtypeStruct((TOP_K, t), jnp.int32),
            jax.ShapeDtypeStruct((TOP_K, t), F32),
            jax.ShapeDtypeStruct((TOP_K, t), jnp.int32),
            jax.ShapeDtypeStruct((t, LANES), F32),
            jax.ShapeDtypeStruct((e, LANES), F32),
        ],
        scratch_shapes=[pltpu.VMEM((e, LANES), F32)],
        compiler_params=_params("arbitrary"),
        name="moe_router",
    )(x, mod, g.reshape(1, d), w_r.T, b_r.reshape(e, 1))


SC_CORES = 2
SC_SUBCORES = 16
SC_WORKERS = SC_CORES * SC_SUBCORES
SC_CHUNK = 64


def _dispatch_rows(h, dest, pad_idx, rows):
    t, d = h.shape
    top_k = dest.shape[0]
    per_w = t // SC_WORKERS
    n_ch = per_w // SC_CHUNK
    pad_ch = pad_idx.shape[0] // (SC_WORKERS * SC_CHUNK)
    assert n_ch * SC_CHUNK * SC_WORKERS == t
    assert pad_ch * SC_CHUNK * SC_WORKERS == pad_idx.shape[0]
    assert top_k * t + pad_idx.shape[0] == rows
    mesh = plsc.VectorSubcoreMesh(core_axis_name="c", subcore_axis_name="s")

    @functools.partial(
        pl.kernel, mesh=mesh,
        out_type=jax.ShapeDtypeStruct((rows, d), h.dtype),
        scratch_types=[pltpu.VMEM((top_k * n_ch, SC_CHUNK), jnp.int32),
                       pltpu.VMEM((pad_ch, SC_CHUNK), jnp.int32),
                       pltpu.VMEM((SC_CHUNK, d), h.dtype),
                       pltpu.SemaphoreType.DMA],
        name="sc_row_dispatch",
    )
    def dispatch(h_hbm, dest_hbm, pad_hbm, zero_hbm, out_hbm, idx_v, pad_v, rows_v, sem):
        wid = lax.axis_index("s") * SC_CORES + lax.axis_index("c")
        base = wid * per_w
        pltpu.sync_copy(dest_hbm.at[wid], idx_v)
        pltpu.sync_copy(pad_hbm.at[wid], pad_v)

        @pl.loop(0, n_ch)
        def _(j):
            pltpu.sync_copy(h_hbm.at[pl.ds(base + j * SC_CHUNK, SC_CHUNK)], rows_v)
            copies = [pltpu.async_copy(rows_v, out_hbm.at[idx_v.at[k * n_ch + j]], sem)
                      for k in range(top_k)]
            for c in copies:
                c.wait()

        pltpu.sync_copy(zero_hbm, rows_v)

        @pl.loop(0, pad_ch)
        def _(p):
            pltpu.sync_copy(rows_v, out_hbm.at[pad_v.at[p]])

    dest_w = dest.reshape(top_k, SC_WORKERS, n_ch, SC_CHUNK).transpose(1, 0, 2, 3)
    return dispatch(h, dest_w.reshape(SC_WORKERS, top_k * n_ch, SC_CHUNK),
                    pad_idx.reshape(SC_WORKERS, pad_ch, SC_CHUNK),
                    jnp.zeros((SC_CHUNK, d), h.dtype))


def _gather_rows(src, idx):
    n = idx.shape[0]
    d = src.shape[1]
    per_w = n // SC_WORKERS
    n_ch = per_w // SC_CHUNK
    assert per_w * SC_WORKERS == n and n_ch * SC_CHUNK == per_w
    mesh = plsc.VectorSubcoreMesh(core_axis_name="c", subcore_axis_name="s")

    @functools.partial(
        pl.kernel, mesh=mesh,
        out_type=jax.ShapeDtypeStruct((n, d), src.dtype),
        scratch_types=[pltpu.VMEM((n_ch, SC_CHUNK), jnp.int32),
                       pltpu.VMEM((SC_CHUNK, d), src.dtype),
                       pltpu.SemaphoreType.DMA],
        name="sc_row_gather",
    )
    def gather(src_hbm, idx_hbm, out_hbm, idx_v, rows_v, sem):
        wid = lax.axis_index("s") * SC_CORES + lax.axis_index("c")
        base = wid * per_w
        pltpu.sync_copy(idx_hbm.at[wid], idx_v)

        @pl.loop(0, n_ch)
        def _(j):
            pltpu.async_copy(src_hbm.at[idx_v.at[j]], rows_v, sem).wait()
            pltpu.sync_copy(rows_v, out_hbm.at[pl.ds(base + j * SC_CHUNK, SC_CHUNK)])

    return gather(src, idx.reshape(SC_WORKERS, n_ch, SC_CHUNK))


def _expert_kernel(blk_e_ref, nused_ref, x_ref, wgu_ref, bgu_ref, wdn_ref, bdn_ref,
                   o_ref, wgu_bf, wdn_bf):
    i = pl.program_id(0)
    e = blk_e_ref[i]
    e_prev = blk_e_ref[jnp.maximum(i - 1, 0)]
    active = i < nused_ref[0]
    d_e = wdn_ref.shape[2]

    @pl.when(active & ((i == 0) | (e != e_prev)))
    def _():
        step = 128
        for r in range(0, wgu_ref.shape[2], step):
            wgu_bf[r:r + step, :] = wgu_ref[0, 0, r:r + step, :].astype(BF16)
        for r in range(0, d_e, step):
            wdn_bf[r:r + step, :] = wdn_ref[0, 0, r:r + step, :].astype(BF16)

    @pl.when(active)
    def _():
        lo, hi = _unpack_rows(x_ref[...])
        xb = jnp.concatenate([lo.astype(BF16), hi.astype(BF16)], axis=1)
        gu = jnp.dot(xb, wgu_bf[...], preferred_element_type=F32) + bgu_ref[0, 0]
        gt = jnp.minimum(gu[:, :d_e], SWIGLU_LIMIT)
        up = jnp.clip(gu[:, d_e:], -SWIGLU_LIMIT, SWIGLU_LIMIT)
        a = (up + 1.0) * (gt * jax.nn.sigmoid(SWIGLU_ALPHA * gt))
        o_ref[...] = _pack_rows(jnp.dot(a.astype(BF16), wdn_bf[...],
                                        preferred_element_type=F32) + bdn_ref[0, 0])

    @pl.when(jnp.logical_not(active))
    def _():
        o_ref[...] = jnp.zeros_like(o_ref)


def _expert_call(xg, blk_e, n_used, w_gu, b_gu, w_dn, b_dn, layer):
    rows, half = xg.shape
    d = 2 * half
    nb = rows // MOE_ROWS
    _, n_e, _, two_de = w_gu.shape
    d_e = two_de // 2
    grid_spec = pltpu.PrefetchScalarGridSpec(
        num_scalar_prefetch=2,
        grid=(nb,),
        in_specs=[
            pl.BlockSpec((MOE_ROWS, half), lambda i, be, nu: (i, 0)),
            pl.BlockSpec((1, 1, d, two_de), lambda i, be, nu: (layer, be[i], 0, 0)),
            pl.BlockSpec((1, 1, 1, two_de), lambda i, be, nu: (layer, be[i], 0, 0)),
            pl.BlockSpec((1, 1, d_e, d), lambda i, be, nu: (layer, be[i], 0, 0)),
            pl.BlockSpec((1, 1, 1, d), lambda i, be, nu: (layer, be[i], 0, 0)),
        ],
        out_specs=pl.BlockSpec((MOE_ROWS, half), lambda i, be, nu: (i, 0)),
        scratch_shapes=[pltpu.VMEM((d, two_de), BF16), pltpu.VMEM((d_e, d), BF16)],
    )
    return pl.pallas_call(
        _expert_kernel,
        grid_spec=grid_spec,
        out_shape=jax.ShapeDtypeStruct((rows, half), jnp.int32),
        compiler_params=_params("arbitrary"),
        name="moe_experts",
    )(blk_e, n_used, xg, w_gu, b_gu.reshape(b_gu.shape[0], n_e, 1, two_de),
      w_dn, b_dn.reshape(b_dn.shape[0], n_e, 1, d))


def _combine_kernel(x_ref, y_ref, gcol_ref, mod_ref, o_ref):
    gc = gcol_ref[...]
    half = y_ref.shape[2]
    acc_lo = acc_hi = None
    for k in range(TOP_K):
        lo, hi = _unpack_rows(y_ref[k])
        gate = gc[:, k:k + 1]
        acc_lo = gate * lo if k == 0 else acc_lo + gate * lo
        acc_hi = gate * hi if k == 0 else acc_hi + gate * hi
    o_ref[:, :half] = x_ref[:, :half] + mod_ref[0, 5:6, :half] * acc_lo
    o_ref[:, half:] = x_ref[:, half:] + mod_ref[0, 5:6, half:] * acc_hi


def _combine_call(x, yk, gcol, mod, seg):
    t, d = x.shape
    tm = TM_OUT
    return pl.pallas_call(
        _combine_kernel,
        grid=(t // tm,),
        in_specs=[
            pl.BlockSpec((tm, d), lambda i: (i, 0)),
            pl.BlockSpec((TOP_K, tm, d // 2), lambda i: (0, i, 0)),
            pl.BlockSpec((tm, LANES), lambda i: (i, 0)),
            pl.BlockSpec((1, 6, d), lambda i: ((i * tm) // seg, 0, 0)),
        ],
        out_specs=pl.BlockSpec((tm, d), lambda i: (i, 0)),
        out_shape=jax.ShapeDtypeStruct((t, d), F32),
        compiler_params=_params("parallel"),
        name="moe_combine",
    )(x, yk, gcol, mod)


def _moe_layer(x, mod, g, w_r, b_r, w_gu, b_gu, w_dn, b_dn, layer, seg):
    t, d = x.shape
    h, ids, _, rank, gcol, cnt = _router_call(x, mod, g, w_r, b_r, seg)
    counts = cnt[:, 0].astype(jnp.int32)
    padded = (counts + MOE_ROWS - 1) // MOE_ROWS * MOE_ROWS
    pad_end = jnp.cumsum(padded)
    pad_start = pad_end - padded
    e_ar = jnp.arange(N_EXPERTS, dtype=jnp.int32)
    dest = rank + jnp.sum(jnp.where(ids[..., None] == e_ar, pad_start, 0), axis=-1)
    nb = -(-(t * TOP_K) // MOE_ROWS) + N_EXPERTS
    rows = nb * MOE_ROWS
    slack = padded - counts
    slack_end = jnp.cumsum(slack)
    r = jnp.arange(rows - t * TOP_K, dtype=jnp.int32)
    owner = jnp.sum((slack_end[None, :] <= r[:, None]).astype(jnp.int32), axis=1)
    first = pad_start + counts - (slack_end - slack)
    in_expert = r + jnp.sum(jnp.where(owner[:, None] == e_ar, first, 0), axis=-1)
    pad_idx = jnp.where(owner < N_EXPERTS, in_expert, pad_end[-1] + r - slack_end[-1])
    blk_row = jnp.arange(nb, dtype=jnp.int32) * MOE_ROWS
    blk_e = jnp.minimum(jnp.sum((pad_end[None, :] <= blk_row[:, None]).astype(jnp.int32), axis=1),
                        N_EXPERTS - 1)
    n_used = (pad_end[-1:] // MOE_ROWS).astype(jnp.int32)
    xg = _dispatch_rows(h, dest, pad_idx, rows)
    yb = _expert_call(xg, blk_e, n_used, w_gu, b_gu, w_dn, b_dn, layer)
    yk = _gather_rows(yb, dest.reshape(-1)).reshape(TOP_K, t, d // 2)
    return _combine_call(x, yk, gcol, mod, seg)


def kernel(x_prompt, x_sample, c_prompt, c_sample, g_mix, g_ffn, w_ada, b_ada, gm_w_in, gm_g_v, gm_w_s, gm_b_s, gm_w_out, sc_w_in, sc_w_conv, sc_w_out, mla_w_down, mla_g_q_lat, mla_g_kv_lat, mla_w_uq, mla_w_ukv, mla_g_qh, mla_g_kh, mla_w_o, moe_w_router, moe_b_router, moe_w_gu, moe_b_gu, moe_w_dn, moe_b_dn):
    bp, sp, d = x_prompt.shape
    bs, ss, _ = x_sample.shape
    tp, ts = bp * sp, bs * ss
    t = tp + ts
    depth = g_mix.shape[0]
    seg = min(sp, ss)
    assert sp % seg == 0 and ss % seg == 0

    x = jnp.concatenate([x_prompt.reshape(tp, d), x_sample.reshape(ts, d)], axis=0)

    n_seq = bp + bs
    c_all = jnp.concatenate([c_prompt, c_sample], axis=0)
    c_pad = jnp.pad(c_all, ((0, (-n_seq) % 8), (0, 0)))
    mod_all = _ada_call(c_pad, w_ada, b_ada)[:, :n_seq].reshape(depth, n_seq, 6, d)
    seg_seq = jnp.concatenate([jnp.repeat(jnp.arange(bp), sp // seg),
                               bp + jnp.repeat(jnp.arange(bs), ss // seg)])
    mod_seg = mod_all[:, seg_seq]

    def tile_meta(tm):
        t0 = jnp.arange(t // tm, dtype=jnp.int32) * tm
        pos = jnp.where(t0 < tp, t0 % sp, (t0 - tp) % ss)
        slen = jnp.where(t0 < tp, sp, ss)
        return pos, slen

    for i in range(depth):
        mod = mod_seg[i]
        kind, j = i % 3, i // 3
        if kind == 0:
            x = _gmlp_call(x, mod, g_mix[i], gm_w_in[j], gm_g_v[j], gm_w_s[j], gm_b_s[j],
                           gm_w_out[j], seg)
        elif kind == 1:
            pos, slen = tile_meta(TM_CONV)
            x = _conv_call(x, mod, g_mix[i], sc_w_in[j], sc_w_conv[j], sc_w_out[j], seg,
                           (pos != 0).astype(jnp.int32),
                           (pos + TM_CONV != slen).astype(jnp.int32))
        else:
            wd, uq, ukv, gh = _mla_prepare_weights(mla_w_down[j], mla_w_uq[j], mla_w_ukv[j],
                                                  mla_g_qh[j], mla_g_kh[j])
            cos, sin = _rope_tables(max(sp, ss))
            pos, _ = tile_meta(TM_MLA)
            q, k, v = _mla_pre_call(x, mod, g_mix[i], wd, mla_g_q_lat[j], mla_g_kv_lat[j],
                                    uq, ukv, gh, cos, sin, seg, pos // TM_MLA)
            o = jnp.concatenate([_attn_call(q, k, v, 0, bp, sp),
                                 _attn_call(q, k, v, tp, bs, ss)], axis=0)
            x = _proj_res_call(x, o, mod, mla_w_o[j], seg)
        x = _moe_layer(x, mod, g_ffn[i], moe_w_router[i], moe_b_router[i],
                       moe_w_gu, moe_b_gu, moe_w_dn, moe_b_dn, i, seg)

    return (x[:tp].reshape(bp, sp, d), x[tp:].reshape(bs, ss, d))
```

```python
import functools

import jax
import jax.numpy as jnp
from jax import lax
from jax.experimental import pallas as pl
from jax.experimental.pallas import tpu as pltpu
from jax.experimental.pallas import tpu_sc as plsc

F32 = jnp.float32
BF16 = jnp.bfloat16

NORM_EPS = 1e-6
GM_GROUPS = 8
GM_CHUNK = 128
MLA_HEADS = 8
MLA_Q_RANK = 384
MLA_KV_RANK = 256
MLA_NOPE = 128
MLA_ROPE = 64
MLA_V = 128
MLA_DK = MLA_NOPE + MLA_ROPE
ROPE_THETA = 10000.0
N_EXPERTS = 32
TOP_K = 4
SWIGLU_LIMIT = 7.0
SWIGLU_ALPHA = 1.702
LOG2_E = 1.4426950408889634

LANES = 128
HALO = 16
VMEM_LIMIT = 56 * 1024 * 1024

TM_GMLP = 256
TM_CONV = 512
TM_MLA = 256
TM_ROUTER = 512
TM_OUT = 512
TQ_ATTN = 256
ATTN_SUBTILES = 2
TK_ATTN = 2048
MOE_ROWS = 512


def _params(*sem):
    return pltpu.CompilerParams(dimension_semantics=sem, vmem_limit_bytes=VMEM_LIMIT)


def _const_spec(shape):
    nd = len(shape)
    return pl.BlockSpec(shape, lambda *_: (0,) * nd)


def _norm_mod(x, g, shift, scale):
    ms = jnp.mean(x * x, axis=-1, keepdims=True)
    return (x * lax.rsqrt(ms + NORM_EPS)) * g * (1.0 + scale) + shift


def _pack_rows(x):
    half = x.shape[1] // 2
    lo = pltpu.bitcast(x[:, :half].astype(BF16).astype(F32), jnp.uint32)
    hi = pltpu.bitcast(x[:, half:].astype(BF16).astype(F32), jnp.uint32)
    word = lax.shift_right_logical(lo, jnp.uint32(16)) | (hi & jnp.uint32(0xFFFF0000))
    return pltpu.bitcast(word, jnp.int32)


def _unpack_rows(w):
    u = pltpu.bitcast(w, jnp.uint32)
    lo = pltpu.bitcast(lax.shift_left(u, jnp.uint32(16)), F32)
    hi = pltpu.bitcast(u & jnp.uint32(0xFFFF0000), F32)
    return lo, hi


def _ada_kernel(c_ref, w_ref, b_ref, o_ref):
    c = c_ref[...]
    a = (c * jax.nn.sigmoid(c)).astype(BF16)
    w = w_ref[0].astype(BF16)
    o_ref[0] = jnp.dot(a, w, preferred_element_type=F32) + b_ref[0]


def _ada_call(c_pad, w_ada, b_ada):
    depth, d, n = w_ada.shape
    tn = 1536
    rows = c_pad.shape[0]
    return pl.pallas_call(
        _ada_kernel,
        grid=(depth, n // tn),
        in_specs=[
            pl.BlockSpec((rows, d), lambda l, j: (0, 0)),
            pl.BlockSpec((1, d, tn), lambda l, j: (l, 0, j)),
            pl.BlockSpec((1, 1, tn), lambda l, j: (l, 0, j)),
        ],
        out_specs=pl.BlockSpec((1, rows, tn), lambda l, j: (l, 0, j)),
        out_shape=jax.ShapeDtypeStruct((depth, rows, n), F32),
        compiler_params=_params("arbitrary", "arbitrary"),
        name="adaln_mod",
    )(c_pad, w_ada, b_ada.reshape(depth, 1, n))


def _gmlp_kernel(x_ref, mod_ref, g_ref, win_ref, gv_ref, ws_ref, bs_ref, wout_ref,
                 o_ref, gated_ref):
    x = x_ref[...]
    tm = x.shape[0]
    width = gv_ref.shape[1]
    gw = width // GM_GROUPS
    h = _norm_mod(x, g_ref[...], mod_ref[0, 0:1, :], mod_ref[0, 1:2, :]).astype(BF16)
    uv = jax.nn.gelu(jnp.dot(h, win_ref[...], preferred_element_type=F32))
    u = uv[:, :width]
    v = uv[:, width:]
    v = v * lax.rsqrt(jnp.mean(v * v, axis=-1, keepdims=True) + NORM_EPS) * gv_ref[...]
    vb = v.astype(BF16)
    bs = bs_ref[...]
    for c in range(tm // GM_CHUNK):
        r0 = c * GM_CHUNK
        for g in range(GM_GROUPS):
            c0 = g * gw
            vm = jnp.dot(ws_ref[g], vb[r0:r0 + GM_CHUNK, c0:c0 + gw],
                         preferred_element_type=F32) + bs[:, g:g + 1]
            gated_ref[r0:r0 + GM_CHUNK, c0:c0 + gw] = (
                u[r0:r0 + GM_CHUNK, c0:c0 + gw] * vm).astype(BF16)
    out = jnp.dot(gated_ref[...], wout_ref[...], preferred_element_type=F32)
    o_ref[...] = x + mod_ref[0, 2:3, :] * out


def _gmlp_call(x, mod, g, w_in, g_v, w_s, b_s, w_out, seg):
    t, d = x.shape
    tm = TM_GMLP
    width = g_v.shape[-1]
    return pl.pallas_call(
        _gmlp_kernel,
        grid=(t // tm,),
        in_specs=[
            pl.BlockSpec((tm, d), lambda i: (i, 0)),
            pl.BlockSpec((1, 6, d), lambda i: ((i * tm) // seg, 0, 0)),
            _const_spec((1, d)),
            _const_spec(w_in.shape),
            _const_spec((1, width)),
            _const_spec(w_s.shape),
            _const_spec((GM_CHUNK, GM_GROUPS)),
            _const_spec(w_out.shape),
        ],
        out_specs=pl.BlockSpec((tm, d), lambda i: (i, 0)),
        out_shape=jax.ShapeDtypeStruct((t, d), F32),
        scratch_shapes=[pltpu.VMEM((tm, width), BF16)],
        compiler_params=_params("parallel"),
        name="gmlp_mixer",
    )(x, mod, g.reshape(1, d), w_in.astype(BF16), g_v.reshape(1, width),
      w_s.astype(BF16), b_s.T, w_out.astype(BF16))


def _conv_kernel(start_ref, end_ref, x_ref, xp_ref, xn_ref, mod_ref, g_ref, win_ref,
                 wc_ref, wout_ref, o_ref, h_ref, z_ref):
    i = pl.program_id(0)
    x = x_ref[...]
    tm, d = x.shape
    g = g_ref[...]
    sh = mod_ref[0, 0:1, :]
    sc = mod_ref[0, 1:2, :]
    h_ref[0:HALO, :] = _norm_mod(xp_ref[...], g, sh, sc).astype(BF16)
    h_ref[HALO:HALO + tm, :] = _norm_mod(x, g, sh, sc).astype(BF16)
    h_ref[HALO + tm:, :] = _norm_mod(xn_ref[...], g, sh, sc).astype(BF16)
    bcx = jnp.dot(h_ref[...], win_ref[...], preferred_element_type=F32)
    w = bcx.shape[1] // 3
    bg = bcx[HALO:HALO + tm, :w]
    z_ref[...] = bcx[:, w:2 * w] * bcx[:, 2 * w:]
    keep_prev = (start_ref[i] == 0).astype(F32)
    keep_next = (end_ref[i] == 0).astype(F32)
    z_ref[0:HALO, :] = z_ref[0:HALO, :] * keep_prev
    z_ref[HALO + tm:, :] = z_ref[HALO + tm:, :] * keep_next
    y = (wc_ref[0:1, :] * z_ref[HALO - 1:HALO - 1 + tm, :]
         + wc_ref[1:2, :] * z_ref[HALO:HALO + tm, :]
         + wc_ref[2:3, :] * z_ref[HALO + 1:HALO + 1 + tm, :])
    out = jnp.dot((bg * y).astype(BF16), wout_ref[...], preferred_element_type=F32)
    o_ref[...] = x + mod_ref[0, 2:3, :] * out


def _conv_call(x, mod, g, w_in, w_conv, w_out, seg, tile_start, tile_end):
    t, d = x.shape
    tm = TM_CONV
    w = w_conv.shape[-1]
    hb = tm // HALO
    last = t // HALO - 1
    grid_spec = pltpu.PrefetchScalarGridSpec(
        num_scalar_prefetch=2,
        grid=(t // tm,),
        in_specs=[
            pl.BlockSpec((tm, d), lambda i, *_: (i, 0)),
            pl.BlockSpec((HALO, d), lambda i, *_: (jnp.maximum(i * hb - 1, 0), 0)),
            pl.BlockSpec((HALO, d), lambda i, *_: (jnp.minimum((i + 1) * hb, last), 0)),
            pl.BlockSpec((1, 6, d), lambda i, *_: ((i * tm) // seg, 0, 0)),
            _const_spec((1, d)),
            _const_spec(w_in.shape),
            _const_spec(w_conv.shape),
            _const_spec(w_out.shape),
        ],
        out_specs=pl.BlockSpec((tm, d), lambda i, *_: (i, 0)),
        scratch_shapes=[pltpu.VMEM((tm + 2 * HALO, d), BF16),
                        pltpu.VMEM((tm + 2 * HALO, w), F32)],
    )
    return pl.pallas_call(
        _conv_kernel,
        grid_spec=grid_spec,
        out_shape=jax.ShapeDtypeStruct((t, d), F32),
        compiler_params=_params("parallel"),
        name="shortconv_mixer",
    )(tile_start, tile_end, x, x, x, mod, g.reshape(1, d), w_in.astype(BF16),
      w_conv, w_out.astype(BF16))


QH = 3 * LANES
KH = 2 * LANES


def _mla_pre_kernel(pos_ref, x_ref, mod_ref, g_ref, wd_ref, gq_ref, gkv_ref, wuq_ref,
                    wukv_ref, gh_ref, cos_ref, sin_ref, q_ref, k_ref, v_ref):
    del pos_ref
    x = x_ref[...]
    h = _norm_mod(x, g_ref[...], mod_ref[0, 0:1, :], mod_ref[0, 1:2, :]).astype(BF16)
    lat = jnp.dot(h, wd_ref[...], preferred_element_type=F32)
    ql = lat[:, :MLA_Q_RANK]
    kvl = lat[:, MLA_Q_RANK:MLA_Q_RANK + MLA_KV_RANK]
    pe = lat[:, MLA_Q_RANK + MLA_KV_RANK:MLA_Q_RANK + MLA_KV_RANK + LANES]
    pe_sw = lat[:, MLA_Q_RANK + MLA_KV_RANK + LANES:]
    ql = ql * lax.rsqrt(jnp.mean(ql * ql, axis=-1, keepdims=True) + NORM_EPS) * gq_ref[...]
    kvl = kvl * lax.rsqrt(jnp.mean(kvl * kvl, axis=-1, keepdims=True) + NORM_EPS) * gkv_ref[...]
    q = jnp.dot(ql.astype(BF16), wuq_ref[...], preferred_element_type=F32)
    kv = jnp.dot(kvl.astype(BF16), wukv_ref[...], preferred_element_type=F32)
    cos = cos_ref[...]
    sin = sin_ref[...]
    gqn, gqr, gqs = gh_ref[0:1, :], gh_ref[1:2, :], gh_ref[2:3, :]
    gkn, gkr, gks = gh_ref[3:4, :], gh_ref[4:5, :], gh_ref[5:6, :]
    pe_ss = jnp.sum(pe * pe, axis=-1, keepdims=True)
    qscale = MLA_DK ** -0.5 * LOG2_E
    for hd in range(MLA_HEADS):
        qn = q[:, hd * QH:hd * QH + LANES]
        qr = q[:, hd * QH + LANES:hd * QH + 2 * LANES]
        qs = q[:, hd * QH + 2 * LANES:(hd + 1) * QH]
        ss = jnp.sum(qn * qn, axis=-1, keepdims=True) + jnp.sum(qr * qr, axis=-1, keepdims=True)
        rinv = lax.rsqrt(ss * (1.0 / MLA_DK) + NORM_EPS) * qscale
        q_ref[hd, :, 0:LANES] = (qn * rinv * gqn).astype(BF16)
        q_ref[hd, :, LANES:] = ((qr * rinv * gqr) * cos + (qs * rinv * gqs) * sin).astype(BF16)
        kn = kv[:, hd * 2 * LANES:hd * 2 * LANES + LANES]
        ss = jnp.sum(kn * kn, axis=-1, keepdims=True) + pe_ss
        rinv = lax.rsqrt(ss * (1.0 / MLA_DK) + NORM_EPS)
        k_ref[hd, :, 0:LANES] = (kn * rinv * gkn).astype(BF16)
        k_ref[hd, :, LANES:] = ((pe * rinv * gkr) * cos + (pe_sw * rinv * gks) * sin).astype(BF16)
        v_ref[hd] = kv[:, hd * 2 * LANES + LANES:(hd + 1) * 2 * LANES].T.astype(BF16)


def _mla_prepare_weights(w_down, w_uq, w_ukv, g_qh, g_kh):
    half = MLA_ROPE // 2
    pad = LANES - MLA_ROPE

    def swap(a):
        return jnp.concatenate([a[..., half:], a[..., :half]], axis=-1)

    def pad_lanes(a):
        return jnp.pad(a, [(0, 0)] * (a.ndim - 1) + [(0, pad)])

    d = w_down.shape[0]
    pe_w = w_down[:, MLA_Q_RANK + MLA_KV_RANK:]
    wd = jnp.concatenate([w_down[:, :MLA_Q_RANK + MLA_KV_RANK],
                          pad_lanes(pe_w), pad_lanes(swap(pe_w))], axis=1)
    uq = w_uq.reshape(MLA_Q_RANK, MLA_HEADS, MLA_DK)
    uq_r = uq[..., MLA_NOPE:]
    uq = jnp.concatenate([uq[..., :MLA_NOPE], pad_lanes(uq_r), pad_lanes(swap(uq_r))], axis=-1)
    uq = uq.reshape(MLA_Q_RANK, MLA_HEADS * QH)

    def gains(gv):
        r = gv[MLA_NOPE:]
        return [gv[:MLA_NOPE], pad_lanes(r), pad_lanes(swap(r))]

    gh = jnp.stack(gains(g_qh) + gains(g_kh) + [jnp.zeros((LANES,), F32)] * 2)
    return wd.astype(BF16), uq.astype(BF16), w_ukv.astype(BF16), gh


def _rope_tables(s):
    half = MLA_ROPE // 2
    inv = 1.0 / (ROPE_THETA ** (jnp.arange(0, MLA_ROPE, 2, dtype=F32) / MLA_ROPE))
    ang = jnp.arange(s, dtype=F32)[:, None] * inv[None, :]
    cos, sin = jnp.cos(ang), jnp.sin(ang)
    zeros = jnp.zeros((s, LANES - MLA_ROPE), F32)
    del half
    return (jnp.concatenate([cos, cos, zeros], axis=1),
            jnp.concatenate([-sin, sin, zeros], axis=1))


def _mla_pre_call(x, mod, g, wd, g_q_lat, g_kv_lat, uq, ukv, gh, cos, sin, seg, tile_pos):
    t, d = x.shape
    tm = TM_MLA
    grid_spec = pltpu.PrefetchScalarGridSpec(
        num_scalar_prefetch=1,
        grid=(t // tm,),
        in_specs=[
            pl.BlockSpec((tm, d), lambda i, p: (i, 0)),
            pl.BlockSpec((1, 6, d), lambda i, p: ((i * tm) // seg, 0, 0)),
            _const_spec((1, d)),
            _const_spec(wd.shape),
            _const_spec((1, MLA_Q_RANK)),
            _const_spec((1, MLA_KV_RANK)),
            _const_spec(uq.shape),
            _const_spec(ukv.shape),
            _const_spec(gh.shape),
            pl.BlockSpec((tm, LANES), lambda i, p: (p[i], 0)),
            pl.BlockSpec((tm, LANES), lambda i, p: (p[i], 0)),
        ],
        out_specs=[
            pl.BlockSpec((MLA_HEADS, tm, KH), lambda i, p: (0, i, 0)),
            pl.BlockSpec((MLA_HEADS, tm, KH), lambda i, p: (0, i, 0)),
            pl.BlockSpec((MLA_HEADS, MLA_V, tm), lambda i, p: (0, 0, i)),
        ],
    )
    return pl.pallas_call(
        _mla_pre_kernel,
        grid_spec=grid_spec,
        out_shape=[jax.ShapeDtypeStruct((MLA_HEADS, t, KH), BF16),
                   jax.ShapeDtypeStruct((MLA_HEADS, t, KH), BF16),
                   jax.ShapeDtypeStruct((MLA_HEADS, MLA_V, t), BF16)],
        compiler_params=_params("parallel"),
        name="mla_qkv",
    )(tile_pos, x, mod, g.reshape(1, d), wd, g_q_lat.reshape(1, -1), g_kv_lat.reshape(1, -1),
      uq, ukv, gh, cos, sin)


def _attn_kernel(q_ref, k_ref, vt_ref, o_ref, s_buf, p_buf):
    tk = TK_ATTN
    n = k_ref.shape[1] // tk
    nt = (((1,), (1,)), ((), ()))
    tq = q_ref.shape[1] // ATTN_SUBTILES
    subs = range(ATTN_SUBTILES)
    qs = [q_ref[0, sub * tq:(sub + 1) * tq, :] for sub in subs]

    def scores(sub, j):
        return lax.dot_general(k_ref[0, j * tk:(j + 1) * tk, :], qs[sub], nt,
                               preferred_element_type=F32)

    def values(sub, j):
        return jnp.dot(vt_ref[0, :, j * tk:(j + 1) * tk], p_buf[sub, j % 2],
                       preferred_element_type=F32)

    m = [jnp.full((1, tq), -jnp.inf, F32) for _ in subs]
    l = [jnp.zeros((1, tq), F32) for _ in subs]
    acc = [jnp.zeros((MLA_V, tq), F32) for _ in subs]
    alpha = [None for _ in subs]
    for sub in subs:
        s_buf[sub, 0] = scores(sub, 0)
    for j in range(n):
        cur = j % 2
        for sub in subs:
            if j + 1 < n:
                s_buf[sub, 1 - cur] = scores(sub, j + 1)
            if j > 0:
                acc[sub] = alpha[sub] * acc[sub] + values(sub, j - 1)
            s = s_buf[sub, cur]
            m_new = jnp.maximum(m[sub], jnp.max(s, axis=0, keepdims=True))
            alpha[sub] = jnp.exp2(m[sub] - m_new)
            p = jnp.exp2(s - m_new)
            l[sub] = alpha[sub] * l[sub] + jnp.sum(p, axis=0, keepdims=True)
            m[sub] = m_new
            p_buf[sub, cur] = p.astype(BF16)
    for sub in subs:
        a = alpha[sub] * acc[sub] + values(sub, n - 1)
        o_ref[sub * tq:(sub + 1) * tq, :] = (a / l[sub]).T.astype(BF16)


def _attn_call(q, k, vt, tok0, n_seq, s_len):
    tq, tk = TQ_ATTN * ATTN_SUBTILES, TK_ATTN
    assert s_len % tk == 0
    nq = s_len // tq
    qb0 = tok0 // tq
    sb0 = tok0 // s_len
    return pl.pallas_call(
        _attn_kernel,
        grid=(n_seq, MLA_HEADS, nq),
        in_specs=[
            pl.BlockSpec((1, tq, KH), lambda s, h, i: (h, qb0 + s * nq + i, 0)),
            pl.BlockSpec((1, s_len, KH), lambda s, h, i: (h, sb0 + s, 0)),
            pl.BlockSpec((1, MLA_V, s_len), lambda s, h, i: (h, 0, sb0 + s)),
        ],
        out_specs=pl.BlockSpec((tq, MLA_V), lambda s, h, i: (s * nq + i, h)),
        out_shape=jax.ShapeDtypeStruct((n_seq * s_len, MLA_HEADS * MLA_V), BF16),
        scratch_shapes=[pltpu.VMEM((ATTN_SUBTILES, 2, tk, TQ_ATTN), F32),
                        pltpu.VMEM((ATTN_SUBTILES, 2, tk, TQ_ATTN), BF16)],
        compiler_params=_params("parallel", "parallel", "arbitrary"),
        name="mla_attention",
    )(q, k, vt)


def _proj_res_kernel(x_ref, o_ref, mod_ref, w_ref, out_ref):
    out = jnp.dot(o_ref[...], w_ref[...], preferred_element_type=F32)
    out_ref[...] = x_ref[...] + mod_ref[0, 2:3, :] * out


def _proj_res_call(x, o, mod, w_o, seg):
    t, d = x.shape
    tm = TM_OUT
    return pl.pallas_call(
        _proj_res_kernel,
        grid=(t // tm,),
        in_specs=[
            pl.BlockSpec((tm, d), lambda i: (i, 0)),
            pl.BlockSpec((tm, o.shape[1]), lambda i: (i, 0)),
            pl.BlockSpec((1, 6, d), lambda i: ((i * tm) // seg, 0, 0)),
            _const_spec(w_o.shape),
        ],
        out_specs=pl.BlockSpec((tm, d), lambda i: (i, 0)),
        out_shape=jax.ShapeDtypeStruct((t, d), F32),
        compiler_params=_params("parallel"),
        name="mla_out_proj",
    )(x, o, mod, w_o.astype(BF16))


def _router_kernel(x_ref, mod_ref, g_ref, wr_ref, br_ref,
                   h_ref, ids_ref, gates_ref, rank_ref, gcol_ref, cnt_ref, carry_ref):
    i = pl.program_id(0)

    @pl.when(i == 0)
    def _():
        carry_ref[...] = jnp.zeros_like(carry_ref)

    x = x_ref[...]
    tm = x.shape[0]
    h = _norm_mod(x, g_ref[...], mod_ref[0, 3:4, :], mod_ref[0, 4:5, :])
    h_ref[...] = _pack_rows(h)
    logits = lax.dot_general(wr_ref[...], h, (((1,), (1,)), ((), ())),
                             precision=lax.Precision.HIGHEST,
                             preferred_element_type=F32) + br_ref[...]
    eidx = lax.broadcasted_iota(jnp.int32, logits.shape, 0)
    lg = logits
    sel = jnp.zeros(logits.shape, F32)
    vals, ids = [], []
    for _ in range(TOP_K):
        m = jnp.max(lg, axis=0, keepdims=True)
        idx = jnp.min(jnp.where(lg == m, eidx, N_EXPERTS), axis=0, keepdims=True)
        onehot = eidx == idx
        vals.append(m)
        ids.append(idx)
        lg = jnp.where(onehot, -jnp.inf, lg)
        sel = jnp.where(onehot, 1.0, sel)
    ex = [jnp.exp(v - vals[0]) for v in vals]
    den = ex[0] + ex[1] + ex[2] + ex[3]
    gates = [e / den for e in ex]
    r_io = lax.broadcasted_iota(jnp.int32, (tm, tm), 0)
    c_io = lax.broadcasted_iota(jnp.int32, (tm, tm), 1)
    before = jnp.where(r_io < c_io, 1.0, 0.0).astype(BF16)
    cum = jnp.dot(sel.astype(BF16), before, preferred_element_type=F32) + carry_ref[:, 0:1]
    for k in range(TOP_K):
        rk = jnp.sum(jnp.where(eidx == ids[k], cum, 0.0), axis=0, keepdims=True)
        rank_ref[k:k + 1, :] = rk.astype(jnp.int32)
        ids_ref[k:k + 1, :] = ids[k]
        gates_ref[k:k + 1, :] = gates[k]
    g8 = jnp.concatenate(gates + [jnp.zeros((LANES - TOP_K, tm), F32)], axis=0)
    gcol_ref[...] = g8.T
    carry_ref[...] = carry_ref[...] + jnp.sum(sel, axis=1, keepdims=True)
    cnt_ref[...] = carry_ref[...]


def _router_call(x, mod, g, w_r, b_r, seg):
    t, d = x.shape
    tm = TM_ROUTER
    e = w_r.shape[1]
    return pl.pallas_call(
        _router_kernel,
        grid=(t // tm,),
        in_specs=[
            pl.BlockSpec((tm, d), lambda i: (i, 0)),
            pl.BlockSpec((1, 6, d), lambda i: ((i * tm) // seg, 0, 0)),
            _const_spec((1, d)),
            _const_spec((e, d)),
            _const_spec((e, 1)),
        ],
        out_specs=[
            pl.BlockSpec((tm, d // 2), lambda i: (i, 0)),
            pl.BlockSpec((TOP_K, tm), lambda i: (0, i)),
            pl.BlockSpec((TOP_K, tm), lambda i: (0, i)),
            pl.BlockSpec((TOP_K, tm), lambda i: (0, i)),
            pl.BlockSpec((tm, LANES), lambda i: (i, 0)),
            _const_spec((e, LANES)),
        ],
        out_shape=[
            jax.ShapeDtypeStruct((t, d // 2), jnp.int32),
            jax.ShapeDtypeStruct((TOP_K, t), jnp.int32),
            jax.ShapeDtypeStruct((TOP_K, t), F32),
            jax.ShapeDtypeStruct((TOP_K, t), jnp.int32),
            jax.ShapeDtypeStruct((t, LANES), F32),
            jax.ShapeDtypeStruct((e, LANES), F32),
        ],
        scratch_shapes=[pltpu.VMEM((e, LANES), F32)],
        compiler_params=_params("arbitrary"),
        name="moe_router",
    )(x, mod, g.reshape(1, d), w_r.T, b_r.reshape(e, 1))


SC_CORES = 2
SC_SUBCORES = 16
SC_WORKERS = SC_CORES * SC_SUBCORES
SC_CHUNK = 64


def _dispatch_rows(h, dest, pad_idx, rows):
    t, d = h.shape
    top_k = dest.shape[0]
    per_w = t // SC_WORKERS
    n_ch = per_w // SC_CHUNK
    pad_ch = pad_idx.shape[0] // (SC_WORKERS * SC_CHUNK)
    assert n_ch * SC_CHUNK * SC_WORKERS == t
    assert pad_ch * SC_CHUNK * SC_WORKERS == pad_idx.shape[0]
    assert top_k * t + pad_idx.shape[0] == rows
    mesh = plsc.VectorSubcoreMesh(core_axis_name="c", subcore_axis_name="s")

    @functools.partial(
        pl.kernel, mesh=mesh,
        out_type=jax.ShapeDtypeStruct((rows, d), h.dtype),
        scratch_types=[pltpu.VMEM((top_k * n_ch, SC_CHUNK), jnp.int32),
                       pltpu.VMEM((pad_ch, SC_CHUNK), jnp.int32),
                       pltpu.VMEM((SC_CHUNK, d), h.dtype),
                       pltpu.SemaphoreType.DMA],
        name="sc_row_dispatch",
    )
    def dispatch(h_hbm, dest_hbm, pad_hbm, zero_hbm, out_hbm, idx_v, pad_v, rows_v, sem):
        wid = lax.axis_index("s") * SC_CORES + lax.axis_index("c")
        base = wid * per_w
        pltpu.sync_copy(dest_hbm.at[wid], idx_v)
        pltpu.sync_copy(pad_hbm.at[wid], pad_v)

        @pl.loop(0, n_ch)
        def _(j):
            pltpu.sync_copy(h_hbm.at[pl.ds(base + j * SC_CHUNK, SC_CHUNK)], rows_v)
            copies = [pltpu.async_copy(rows_v, out_hbm.at[idx_v.at[k * n_ch + j]], sem)
                      for k in range(top_k)]
            for c in copies:
                c.wait()

        pltpu.sync_copy(zero_hbm, rows_v)

        @pl.loop(0, pad_ch)
        def _(p):
            pltpu.sync_copy(rows_v, out_hbm.at[pad_v.at[p]])

    dest_w = dest.reshape(top_k, SC_WORKERS, n_ch, SC_CHUNK).transpose(1, 0, 2, 3)
    return dispatch(h, dest_w.reshape(SC_WORKERS, top_k * n_ch, SC_CHUNK),
                    pad_idx.reshape(SC_WORKERS, pad_ch, SC_CHUNK),
                    jnp.zeros((SC_CHUNK, d), h.dtype))


def _gather_rows(src, idx):
    n = idx.shape[0]
    d = src.shape[1]
    per_w = n // SC_WORKERS
    n_ch = per_w // SC_CHUNK
    assert per_w * SC_WORKERS == n and n_ch * SC_CHUNK == per_w
    mesh = plsc.VectorSubcoreMesh(core_axis_name="c", subcore_axis_name="s")

    @functools.partial(
        pl.kernel, mesh=mesh,
        out_type=jax.ShapeDtypeStruct((n, d), src.dtype),
        scratch_types=[pltpu.VMEM((n_ch, SC_CHUNK), jnp.int32),
                       pltpu.VMEM((SC_CHUNK, d), src.dtype),
                       pltpu.SemaphoreType.DMA],
        name="sc_row_gather",
    )
    def gather(src_hbm, idx_hbm, out_hbm, idx_v, rows_v, sem):
        wid = lax.axis_index("s") * SC_CORES + lax.axis_index("c")
        base = wid * per_w
        pltpu.sync_copy(idx_hbm.at[wid], idx_v)

        @pl.loop(0, n_ch)
        def _(j):
            pltpu.async_copy(src_hbm.at[idx_v.at[j]], rows_v, sem).wait()
            pltpu.sync_copy(rows_v, out_hbm.at[pl.ds(base + j * SC_CHUNK, SC_CHUNK)])

    return gather(src, idx.reshape(SC_WORKERS, n_ch, SC_CHUNK))


def _expert_kernel(blk_e_ref, nused_ref, x_ref, wgu_ref, bgu_ref, wdn_ref, bdn_ref,
                   o_ref, wgu_bf, wdn_bf):
    i = pl.program_id(0)
    e = blk_e_ref[i]
    e_prev = blk_e_ref[jnp.maximum(i - 1, 0)]
    active = i < nused_ref[0]
    d_e = wdn_ref.shape[2]

    @pl.when(active & ((i == 0) | (e != e_prev)))
    def _():
        step = 128
        for r in range(0, wgu_ref.shape[2], step):
            wgu_bf[r:r + step, :] = wgu_ref[0, 0, r:r + step, :].astype(BF16)
        for r in range(0, d_e, step):
            wdn_bf[r:r + step, :] = wdn_ref[0, 0, r:r + step, :].astype(BF16)

    @pl.when(active)
    def _():
        lo, hi = _unpack_rows(x_ref[...])
        xb = jnp.concatenate([lo.astype(BF16), hi.astype(BF16)], axis=1)
        gu = jnp.dot(xb, wgu_bf[...], preferred_element_type=F32) + bgu_ref[0, 0]
        gt = jnp.minimum(gu[:, :d_e], SWIGLU_LIMIT)
        up = jnp.clip(gu[:, d_e:], -SWIGLU_LIMIT, SWIGLU_LIMIT)
        a = (up + 1.0) * (gt * jax.nn.sigmoid(SWIGLU_ALPHA * gt))
        o_ref[...] = _pack_rows(jnp.dot(a.astype(BF16), wdn_bf[...],
                                        preferred_element_type=F32) + bdn_ref[0, 0])

    @pl.when(jnp.logical_not(active))
    def _():
        o_ref[...] = jnp.zeros_like(o_ref)


def _expert_call(xg, blk_e, n_used, w_gu, b_gu, w_dn, b_dn, layer):
    rows, half = xg.shape
    d = 2 * half
    nb = rows // MOE_ROWS
    _, n_e, _, two_de = w_gu.shape
    d_e = two_de // 2
    grid_spec = pltpu.PrefetchScalarGridSpec(
        num_scalar_prefetch=2,
        grid=(nb,),
        in_specs=[
            pl.BlockSpec((MOE_ROWS, half), lambda i, be, nu: (i, 0)),
            pl.BlockSpec((1, 1, d, two_de), lambda i, be, nu: (layer, be[i], 0, 0)),
            pl.BlockSpec((1, 1, 1, two_de), lambda i, be, nu: (layer, be[i], 0, 0)),
            pl.BlockSpec((1, 1, d_e, d), lambda i, be, nu: (layer, be[i], 0, 0)),
            pl.BlockSpec((1, 1, 1, d), lambda i, be, nu: (layer, be[i], 0, 0)),
        ],
        out_specs=pl.BlockSpec((MOE_ROWS, half), lambda i, be, nu: (i, 0)),
        scratch_shapes=[pltpu.VMEM((d, two_de), BF16), pltpu.VMEM((d_e, d), BF16)],
    )
    return pl.pallas_call(
        _expert_kernel,
        grid_spec=grid_spec,
        out_shape=jax.ShapeDtypeStruct((rows, half), jnp.int32),
        compiler_params=_params("arbitrary"),
        name="moe_experts",
    )(blk_e, n_used, xg, w_gu, b_gu.reshape(b_gu.shape[0], n_e, 1, two_de),
      w_dn, b_dn.reshape(b_dn.shape[0], n_e, 1, d))


def _combine_kernel(x_ref, y_ref, gcol_ref, mod_ref, o_ref):
    gc = gcol_ref[...]
    half = y_ref.shape[2]
    acc_lo = acc_hi = None
    for k in range(TOP_K):
        lo, hi = _unpack_rows(y_ref[k])
        gate = gc[:, k:k + 1]
        acc_lo = gate * lo if k == 0 else acc_lo + gate * lo
        acc_hi = gate * hi if k == 0 else acc_hi + gate * hi
    o_ref[:, :half] = x_ref[:, :half] + mod_ref[0, 5:6, :half] * acc_lo
    o_ref[:, half:] = x_ref[:, half:] + mod_ref[0, 5:6, half:] * acc_hi


def _combine_call(x, yk, gcol, mod, seg):
    t, d = x.shape
    tm = TM_OUT
    return pl.pallas_call(
        _combine_kernel,
        grid=(t // tm,),
        in_specs=[
            pl.BlockSpec((tm, d), lambda i: (i, 0)),
            pl.BlockSpec((TOP_K, tm, d // 2), lambda i: (0, i, 0)),
            pl.BlockSpec((tm, LANES), lambda i: (i, 0)),
            pl.BlockSpec((1, 6, d), lambda i: ((i * tm) // seg, 0, 0)),
        ],
        out_specs=pl.BlockSpec((tm, d), lambda i: (i, 0)),
        out_shape=jax.ShapeDtypeStruct((t, d), F32),
        compiler_params=_params("parallel"),
        name="moe_combine",
    )(x, yk, gcol, mod)


def _moe_layer(x, mod, g, w_r, b_r, w_gu, b_gu, w_dn, b_dn, layer, seg):
    t, d = x.shape
    h, ids, _, rank, gcol, cnt = _router_call(x, mod, g, w_r, b_r, seg)
    counts = cnt[:, 0].astype(jnp.int32)
    padded = (counts + MOE_ROWS - 1) // MOE_ROWS * MOE_ROWS
    pad_end = jnp.cumsum(padded)
    pad_start = pad_end - padded
    e_ar = jnp.arange(N_EXPERTS, dtype=jnp.int32)
    dest = rank + jnp.sum(jnp.where(ids[..., None] == e_ar, pad_start, 0), axis=-1)
    nb = -(-(t * TOP_K) // MOE_ROWS) + N_EXPERTS
    rows = nb * MOE_ROWS
    slack = padded - counts
    slack_end = jnp.cumsum(slack)
    r = jnp.arange(rows - t * TOP_K, dtype=jnp.int32)
    owner = jnp.sum((slack_end[None, :] <= r[:, None]).astype(jnp.int32), axis=1)
    first = pad_start + counts - (slack_end - slack)
    in_expert = r + jnp.sum(jnp.where(owner[:, None] == e_ar, first, 0), axis=-1)
    pad_idx = jnp.where(owner < N_EXPERTS, in_expert, pad_end[-1] + r - slack_end[-1])
    blk_row = jnp.arange(nb, dtype=jnp.int32) * MOE_ROWS
    blk_e = jnp.minimum(jnp.sum((pad_end[None, :] <= blk_row[:, None]).astype(jnp.int32), axis=1),
                        N_EXPERTS - 1)
    n_used = (pad_end[-1:] // MOE_ROWS).astype(jnp.int32)
    xg = _dispatch_rows(h, dest, pad_idx, rows)
    yb = _expert_call(xg, blk_e, n_used, w_gu, b_gu, w_dn, b_dn, layer)
    yk = _gather_rows(yb, dest.reshape(-1)).reshape(TOP_K, t, d // 2)
    return _combine_call(x, yk, gcol, mod, seg)


def kernel(x_prompt, x_sample, c_prompt, c_sample, g_mix, g_ffn, w_ada, b_ada, gm_w_in, gm_g_v, gm_w_s, gm_b_s, gm_w_out, sc_w_in, sc_w_conv, sc_w_out, mla_w_down, mla_g_q_lat, mla_g_kv_lat, mla_w_uq, mla_w_ukv, mla_g_qh, mla_g_kh, mla_w_o, moe_w_router, moe_b_router, moe_w_gu, moe_b_gu, moe_w_dn, moe_b_dn):
    bp, sp, d = x_prompt.shape
    bs, ss, _ = x_sample.shape
    tp, ts = bp * sp, bs * ss
    t = tp + ts
    depth = g_mix.shape[0]
    seg = min(sp, ss)
    assert sp % seg == 0 and ss % seg == 0

    x = jnp.concatenate([x_prompt.reshape(tp, d), x_sample.reshape(ts, d)], axis=0)

    n_seq = bp + bs
    c_all = jnp.concatenate([c_prompt, c_sample], axis=0)
    c_pad = jnp.pad(c_all, ((0, (-n_seq) % 8), (0, 0)))
    mod_all = _ada_call(c_pad, w_ada, b_ada)[:, :n_seq].reshape(depth, n_seq, 6, d)
    seg_seq = jnp.concatenate([jnp.repeat(jnp.arange(bp), sp // seg),
                               bp + jnp.repeat(jnp.arange(bs), ss // seg)])
    mod_seg = mod_all[:, seg_seq]

    def tile_meta(tm):
        t0 = jnp.arange(t // tm, dtype=jnp.int32) * tm
        pos = jnp.where(t0 < tp, t0 % sp, (t0 - tp) % ss)
        slen = jnp.where(t0 < tp, sp, ss)
        return pos, slen

    for i in range(depth):
        mod = mod_seg[i]
        kind, j = i % 3, i // 3
        if kind == 0:
            x = _gmlp_call(x, mod, g_mix[i], gm_w_in[j], gm_g_v[j], gm_w_s[j], gm_b_s[j],
                           gm_w_out[j], seg)
        elif kind == 1:
            pos, slen = tile_meta(TM_CONV)
            x = _conv_call(x, mod, g_mix[i], sc_w_in[j], sc_w_conv[j], sc_w_out[j], seg,
                           (pos != 0).astype(jnp.int32),
                           (pos + TM_CONV != slen).astype(jnp.int32))
        else:
            wd, uq, ukv, gh = _mla_prepare_weights(mla_w_down[j], mla_w_uq[j], mla_w_ukv[j],
                                                  mla_g_qh[j], mla_g_kh[j])
            cos, sin = _rope_tables(max(sp, ss))
            pos, _ = tile_meta(TM_MLA)
            q, k, v = _mla_pre_call(x, mod, g_mix[i], wd, mla_g_q_lat[j], mla_g_kv_lat[j],
                                    uq, ukv, gh, cos, sin, seg, pos // TM_MLA)
            o = jnp.concatenate([_attn_call(q, k, v, 0, bp, sp),
                                 _attn_call(q, k, v, tp, bs, ss)], axis=0)
            x = _proj_res_call(x, o, mod, mla_w_o[j], seg)
        x = _moe_layer(x, mod, g_ffn[i], moe_w_router[i], moe_b_router[i],
                       moe_w_gu, moe_b_gu, moe_w_dn, moe_b_dn, i, seg)

    return (x[:tp].reshape(bp, sp, d), x[tp:].reshape(bs, ss, d))
```

```python
import functools

import jax
import jax.numpy as jnp
from jax import lax
from jax.experimental import pallas as pl
from jax.experimental.pallas import tpu as pltpu
from jax.experimental.pallas import tpu_sc as plsc

F32 = jnp.float32
BF16 = jnp.bfloat16

NORM_EPS = 1e-6
GM_GROUPS = 8
GM_CHUNK = 128
MLA_HEADS = 8
MLA_Q_RANK = 384
MLA_KV_RANK = 256
MLA_NOPE = 128
MLA_ROPE = 64
MLA_V = 128
MLA_DK = MLA_NOPE + MLA_ROPE
ROPE_THETA = 10000.0
N_EXPERTS = 32
TOP_K = 4
SWIGLU_LIMIT = 7.0
SWIGLU_ALPHA = 1.702
LOG2_E = 1.4426950408889634

LANES = 128
HALO = 16
VMEM_LIMIT = 56 * 1024 * 1024

TM_GMLP = 256
TM_CONV = 512
TM_MLA = 256
TM_ROUTER = 512
TM_OUT = 512
TQ_ATTN = 256
ATTN_SUBTILES = 2
TK_ATTN = 2048
MOE_ROWS = 512


def _params(*sem):
    return pltpu.CompilerParams(dimension_semantics=sem, vmem_limit_bytes=VMEM_LIMIT)


def _const_spec(shape):
    nd = len(shape)
    return pl.BlockSpec(shape, lambda *_: (0,) * nd)


def _norm_mod(x, g, shift, scale):
    ms = jnp.mean(x * x, axis=-1, keepdims=True)
    return (x * lax.rsqrt(ms + NORM_EPS)) * g * (1.0 + scale) + shift


def _pack_rows(x):
    half = x.shape[1] // 2
    lo = pltpu.bitcast(x[:, :half].astype(BF16).astype(F32), jnp.uint32)
    hi = pltpu.bitcast(x[:, half:].astype(BF16).astype(F32), jnp.uint32)
    word = lax.shift_right_logical(lo, jnp.uint32(16)) | (hi & jnp.uint32(0xFFFF0000))
    return pltpu.bitcast(word, jnp.int32)


def _unpack_rows(w):
    u = pltpu.bitcast(w, jnp.uint32)
    lo = pltpu.bitcast(lax.shift_left(u, jnp.uint32(16)), F32)
    hi = pltpu.bitcast(u & jnp.uint32(0xFFFF0000), F32)
    return lo, hi


def _ada_kernel(c_ref, w_ref, b_ref, o_ref):
    c = c_ref[...]
    a = (c * jax.nn.sigmoid(c)).astype(BF16)
    w = w_ref[0].astype(BF16)
    o_ref[0] = jnp.dot(a, w, preferred_element_type=F32) + b_ref[0]


def _ada_call(c_pad, w_ada, b_ada):
    depth, d, n = w_ada.shape
    tn = 1536
    rows = c_pad.shape[0]
    return pl.pallas_call(
        _ada_kernel,
        grid=(depth, n // tn),
        in_specs=[
            pl.BlockSpec((rows, d), lambda l, j: (0, 0)),
            pl.BlockSpec((1, d, tn), lambda l, j: (l, 0, j)),
            pl.BlockSpec((1, 1, tn), lambda l, j: (l, 0, j)),
        ],
        out_specs=pl.BlockSpec((1, rows, tn), lambda l, j: (l, 0, j)),
        out_shape=jax.ShapeDtypeStruct((depth, rows, n), F32),
        compiler_params=_params("arbitrary", "arbitrary"),
        name="adaln_mod",
    )(c_pad, w_ada, b_ada.reshape(depth, 1, n))


def _gmlp_kernel(x_ref, mod_ref, g_ref, win_ref, gv_ref, ws_ref, bs_ref, wout_ref,
                 o_ref, gated_ref):
    x = x_ref[...]
    tm = x.shape[0]
    width = gv_ref.shape[1]
    gw = width // GM_GROUPS
    h = _norm_mod(x, g_ref[...], mod_ref[0, 0:1, :], mod_ref[0, 1:2, :]).astype(BF16)
    uv = jax.nn.gelu(jnp.dot(h, win_ref[...], preferred_element_type=F32))
    u = uv[:, :width]
    v = uv[:, width:]
    v = v * lax.rsqrt(jnp.mean(v * v, axis=-1, keepdims=True) + NORM_EPS) * gv_ref[...]
    vb = v.astype(BF16)
    bs = bs_ref[...]
    for c in range(tm // GM_CHUNK):
        r0 = c * GM_CHUNK
        for g in range(GM_GROUPS):
            c0 = g * gw
            vm = jnp.dot(ws_ref[g], vb[r0:r0 + GM_CHUNK, c0:c0 + gw],
                         preferred_element_type=F32) + bs[:, g:g + 1]
            gated_ref[r0:r0 + GM_CHUNK, c0:c0 + gw] = (
                u[r0:r0 + GM_CHUNK, c0:c0 + gw] * vm).astype(BF16)
    out = jnp.dot(gated_ref[...], wout_ref[...], preferred_element_type=F32)
    o_ref[...] = x + mod_ref[0, 2:3, :] * out


def _gmlp_call(x, mod, g, w_in, g_v, w_s, b_s, w_out, seg):
    t, d = x.shape
    tm = TM_GMLP
    width = g_v.shape[-1]
    return pl.pallas_call(
        _gmlp_kernel,
        grid=(t // tm,),
        in_specs=[
            pl.BlockSpec((tm, d), lambda i: (i, 0)),
            pl.BlockSpec((1, 6, d), lambda i: ((i * tm) // seg, 0, 0)),
            _const_spec((1, d)),
            _const_spec(w_in.shape),
            _const_spec((1, width)),
            _const_spec(w_s.shape),
            _const_spec((GM_CHUNK, GM_GROUPS)),
            _const_spec(w_out.shape),
        ],
        out_specs=pl.BlockSpec((tm, d), lambda i: (i, 0)),
        out_shape=jax.ShapeDtypeStruct((t, d), F32),
        scratch_shapes=[pltpu.VMEM((tm, width), BF16)],
        compiler_params=_params("parallel"),
        name="gmlp_mixer",
    )(x, mod, g.reshape(1, d), w_in.astype(BF16), g_v.reshape(1, width),
      w_s.astype(BF16), b_s.T, w_out.astype(BF16))


def _conv_kernel(start_ref, end_ref, x_ref, xp_ref, xn_ref, mod_ref, g_ref, win_ref,
                 wc_ref, wout_ref, o_ref, h_ref, z_ref):
    i = pl.program_id(0)
    x = x_ref[...]
    tm, d = x.shape
    g = g_ref[...]
    sh = mod_ref[0, 0:1, :]
    sc = mod_ref[0, 1:2, :]
    h_ref[0:HALO, :] = _norm_mod(xp_ref[...], g, sh, sc).astype(BF16)
    h_ref[HALO:HALO + tm, :] = _norm_mod(x, g, sh, sc).astype(BF16)
    h_ref[HALO + tm:, :] = _norm_mod(xn_ref[...], g, sh, sc).astype(BF16)
    bcx = jnp.dot(h_ref[...], win_ref[...], preferred_element_type=F32)
    w = bcx.shape[1] // 3
    bg = bcx[HALO:HALO + tm, :w]
    z_ref[...] = bcx[:, w:2 * w] * bcx[:, 2 * w:]
    keep_prev = (start_ref[i] == 0).astype(F32)
    keep_next = (end_ref[i] == 0).astype(F32)
    z_ref[0:HALO, :] = z_ref[0:HALO, :] * keep_prev
    z_ref[HALO + tm:, :] = z_ref[HALO + tm:, :] * keep_next
    y = (wc_ref[0:1, :] * z_ref[HALO - 1:HALO - 1 + tm, :]
         + wc_ref[1:2, :] * z_ref[HALO:HALO + tm, :]
         + wc_ref[2:3, :] * z_ref[HALO + 1:HALO + 1 + tm, :])
    out = jnp.dot((bg * y).astype(BF16), wout_ref[...], preferred_element_type=F32)
    o_ref[...] = x + mod_ref[0, 2:3, :] * out


def _conv_call(x, mod, g, w_in, w_conv, w_out, seg, tile_start, tile_end):
    t, d = x.shape
    tm = TM_CONV
    w = w_conv.shape[-1]
    hb = tm // HALO
    last = t // HALO - 1
    grid_spec = pltpu.PrefetchScalarGridSpec(
        num_scalar_prefetch=2,
        grid=(t // tm,),
        in_specs=[
            pl.BlockSpec((tm, d), lambda i, *_: (i, 0)),
            pl.BlockSpec((HALO, d), lambda i, *_: (jnp.maximum(i * hb - 1, 0), 0)),
            pl.BlockSpec((HALO, d), lambda i, *_: (jnp.minimum((i + 1) * hb, last), 0)),
            pl.BlockSpec((1, 6, d), lambda i, *_: ((i * tm) // seg, 0, 0)),
            _const_spec((1, d)),
            _const_spec(w_in.shape),
            _const_spec(w_conv.shape),
            _const_spec(w_out.shape),
        ],
        out_specs=pl.BlockSpec((tm, d), lambda i, *_: (i, 0)),
        scratch_shapes=[pltpu.VMEM((tm + 2 * HALO, d), BF16),
                        pltpu.VMEM((tm + 2 * HALO, w), F32)],
    )
    return pl.pallas_call(
        _conv_kernel,
        grid_spec=grid_spec,
        out_shape=jax.ShapeDtypeStruct((t, d), F32),
        compiler_params=_params("parallel"),
        name="shortconv_mixer",
    )(tile_start, tile_end, x, x, x, mod, g.reshape(1, d), w_in.astype(BF16),
      w_conv, w_out.astype(BF16))


QH = 3 * LANES
KH = 2 * LANES


def _mla_pre_kernel(pos_ref, x_ref, mod_ref, g_ref, wd_ref, gq_ref, gkv_ref, wuq_ref,
                    wukv_ref, gh_ref, cos_ref, sin_ref, q_ref, k_ref, v_ref):
    del pos_ref
    x = x_ref[...]
    h = _norm_mod(x, g_ref[...], mod_ref[0, 0:1, :], mod_ref[0, 1:2, :]).astype(BF16)
    lat = jnp.dot(h, wd_ref[...], preferred_element_type=F32)
    ql = lat[:, :MLA_Q_RANK]
    kvl = lat[:, MLA_Q_RANK:MLA_Q_RANK + MLA_KV_RANK]
    pe = lat[:, MLA_Q_RANK + MLA_KV_RANK:MLA_Q_RANK + MLA_KV_RANK + LANES]
    pe_sw = lat[:, MLA_Q_RANK + MLA_KV_RANK + LANES:]
    ql = ql * lax.rsqrt(jnp.mean(ql * ql, axis=-1, keepdims=True) + NORM_EPS) * gq_ref[...]
    kvl = kvl * lax.rsqrt(jnp.mean(kvl * kvl, axis=-1, keepdims=True) + NORM_EPS) * gkv_ref[...]
    q = jnp.dot(ql.astype(BF16), wuq_ref[...], preferred_element_type=F32)
    kv = jnp.dot(kvl.astype(BF16), wukv_ref[...], preferred_element_type=F32)
    cos = cos_ref[...]
    sin = sin_ref[...]
    gqn, gqr, gqs = gh_ref[0:1, :], gh_ref[1:2, :], gh_ref[2:3, :]
    gkn, gkr, gks = gh_ref[3:4, :], gh_ref[4:5, :], gh_ref[5:6, :]
    pe_ss = jnp.sum(pe * pe, axis=-1, keepdims=True)
    qscale = MLA_DK ** -0.5 * LOG2_E
    for hd in range(MLA_HEADS):
        qn = q[:, hd * QH:hd * QH + LANES]
        qr = q[:, hd * QH + LANES:hd * QH + 2 * LANES]
        qs = q[:, hd * QH + 2 * LANES:(hd + 1) * QH]
        ss = jnp.sum(qn * qn, axis=-1, keepdims=True) + jnp.sum(qr * qr, axis=-1, keepdims=True)
        rinv = lax.rsqrt(ss * (1.0 / MLA_DK) + NORM_EPS) * qscale
        q_ref[hd, :, 0:LANES] = (qn * rinv * gqn).astype(BF16)
        q_ref[hd, :, LANES:] = ((qr * rinv * gqr) * cos + (qs * rinv * gqs) * sin).astype(BF16)
        kn = kv[:, hd * 2 * LANES:hd * 2 * LANES + LANES]
        ss = jnp.sum(kn * kn, axis=-1, keepdims=True) + pe_ss
        rinv = lax.rsqrt(ss * (1.0 / MLA_DK) + NORM_EPS)
        k_ref[hd, :, 0:LANES] = (kn * rinv * gkn).astype(BF16)
        k_ref[hd, :, LANES:] = ((pe * rinv * gkr) * cos + (pe_sw * rinv * gks) * sin).astype(BF16)
        v_ref[hd] = kv[:, hd * 2 * LANES + LANES:(hd + 1) * 2 * LANES].T.astype(BF16)


def _mla_prepare_weights(w_down, w_uq, w_ukv, g_qh, g_kh):
    half = MLA_ROPE // 2
    pad = LANES - MLA_ROPE

    def swap(a):
        return jnp.concatenate([a[..., half:], a[..., :half]], axis=-1)

    def pad_lanes(a):
        return jnp.pad(a, [(0, 0)] * (a.ndim - 1) + [(0, pad)])

    d = w_down.shape[0]
    pe_w = w_down[:, MLA_Q_RANK + MLA_KV_RANK:]
    wd = jnp.concatenate([w_down[:, :MLA_Q_RANK + MLA_KV_RANK],
                          pad_lanes(pe_w), pad_lanes(swap(pe_w))], axis=1)
    uq = w_uq.reshape(MLA_Q_RANK, MLA_HEADS, MLA_DK)
    uq_r = uq[..., MLA_NOPE:]
    uq = jnp.concatenate([uq[..., :MLA_NOPE], pad_lanes(uq_r), pad_lanes(swap(uq_r))], axis=-1)
    uq = uq.reshape(MLA_Q_RANK, MLA_HEADS * QH)

    def gains(gv):
        r = gv[MLA_NOPE:]
        return [gv[:MLA_NOPE], pad_lanes(r), pad_lanes(swap(r))]

    gh = jnp.stack(gains(g_qh) + gains(g_kh) + [jnp.zeros((LANES,), F32)] * 2)
    return wd.astype(BF16), uq.astype(BF16), w_ukv.astype(BF16), gh


def _rope_tables(s):
    half = MLA_ROPE // 2
    inv = 1.0 / (ROPE_THETA ** (jnp.arange(0, MLA_ROPE, 2, dtype=F32) / MLA_ROPE))
    ang = jnp.arange(s, dtype=F32)[:, None] * inv[None, :]
    cos, sin = jnp.cos(ang), jnp.sin(ang)
    zeros = jnp.zeros((s, LANES - MLA_ROPE), F32)
    del half
    return (jnp.concatenate([cos, cos, zeros], axis=1),
            jnp.concatenate([-sin, sin, zeros], axis=1))


def _mla_pre_call(x, mod, g, wd, g_q_lat, g_kv_lat, uq, ukv, gh, cos, sin, seg, tile_pos):
    t, d = x.shape
    tm = TM_MLA
    grid_spec = pltpu.PrefetchScalarGridSpec(
        num_scalar_prefetch=1,
        grid=(t // tm,),
        in_specs=[
            pl.BlockSpec((tm, d), lambda i, p: (i, 0)),
            pl.BlockSpec((1, 6, d), lambda i, p: ((i * tm) // seg, 0, 0)),
            _const_spec((1, d)),
            _const_spec(wd.shape),
            _const_spec((1, MLA_Q_RANK)),
            _const_spec((1, MLA_KV_RANK)),
            _const_spec(uq.shape),
            _const_spec(ukv.shape),
            _const_spec(gh.shape),
            pl.BlockSpec((tm, LANES), lambda i, p: (p[i], 0)),
            pl.BlockSpec((tm, LANES), lambda i, p: (p[i], 0)),
        ],
        out_specs=[
            pl.BlockSpec((MLA_HEADS, tm, KH), lambda i, p: (0, i, 0)),
            pl.BlockSpec((MLA_HEADS, tm, KH), lambda i, p: (0, i, 0)),
            pl.BlockSpec((MLA_HEADS, MLA_V, tm), lambda i, p: (0, 0, i)),
        ],
    )
    return pl.pallas_call(
        _mla_pre_kernel,
        grid_spec=grid_spec,
        out_shape=[jax.ShapeDtypeStruct((MLA_HEADS, t, KH), BF16),
                   jax.ShapeDtypeStruct((MLA_HEADS, t, KH), BF16),
                   jax.ShapeDtypeStruct((MLA_HEADS, MLA_V, t), BF16)],
        compiler_params=_params("parallel"),
        name="mla_qkv",
    )(tile_pos, x, mod, g.reshape(1, d), wd, g_q_lat.reshape(1, -1), g_kv_lat.reshape(1, -1),
      uq, ukv, gh, cos, sin)


def _attn_kernel(q_ref, k_ref, vt_ref, o_ref, s_buf, p_buf):
    tk = TK_ATTN
    n = k_ref.shape[1] // tk
    nt = (((1,), (1,)), ((), ()))
    tq = q_ref.shape[1] // ATTN_SUBTILES
    subs = range(ATTN_SUBTILES)
    qs = [q_ref[0, sub * tq:(sub + 1) * tq, :] for sub in subs]

    def scores(sub, j):
        return lax.dot_general(k_ref[0, j * tk:(j + 1) * tk, :], qs[sub], nt,
                               preferred_element_type=F32)

    def values(sub, j):
        return jnp.dot(vt_ref[0, :, j * tk:(j + 1) * tk], p_buf[sub, j % 2],
                       preferred_element_type=F32)

    m = [jnp.full((1, tq), -jnp.inf, F32) for _ in subs]
    l = [jnp.zeros((1, tq), F32) for _ in subs]
    acc = [jnp.zeros((MLA_V, tq), F32) for _ in subs]
    alpha = [None for _ in subs]
    for sub in subs:
        s_buf[sub, 0] = scores(sub, 0)
    for j in range(n):
        cur = j % 2
        for sub in subs:
            if j + 1 < n:
                s_buf[sub, 1 - cur] = scores(sub, j + 1)
            if j > 0:
                acc[sub] = alpha[sub] * acc[sub] + values(sub, j - 1)
            s = s_buf[sub, cur]
            m_new = jnp.maximum(m[sub], jnp.max(s, axis=0, keepdims=True))
            alpha[sub] = jnp.exp2(m[sub] - m_new)
            p = jnp.exp2(s - m_new)
            l[sub] = alpha[sub] * l[sub] + jnp.sum(p, axis=0, keepdims=True)
            m[sub] = m_new
            p_buf[sub, cur] = p.astype(BF16)
    for sub in subs:
        a = alpha[sub] * acc[sub] + values(sub, n - 1)
        o_ref[sub * tq:(sub + 1) * tq, :] = (a / l[sub]).T.astype(BF16)


def _attn_call(q, k, vt, tok0, n_seq, s_len):
    tq, tk = TQ_ATTN * ATTN_SUBTILES, TK_ATTN
    assert s_len % tk == 0
    nq = s_len // tq
    qb0 = tok0 // tq
    sb0 = tok0 // s_len
    return pl.pallas_call(
        _attn_kernel,
        grid=(n_seq, MLA_HEADS, nq),
        in_specs=[
            pl.BlockSpec((1, tq, KH), lambda s, h, i: (h, qb0 + s * nq + i, 0)),
            pl.BlockSpec((1, s_len, KH), lambda s, h, i: (h, sb0 + s, 0)),
            pl.BlockSpec((1, MLA_V, s_len), lambda s, h, i: (h, 0, sb0 + s)),
        ],
        out_specs=pl.BlockSpec((tq, MLA_V), lambda s, h, i: (s * nq + i, h)),
        out_shape=jax.ShapeDtypeStruct((n_seq * s_len, MLA_HEADS * MLA_V), BF16),
        scratch_shapes=[pltpu.VMEM((ATTN_SUBTILES, 2, tk, TQ_ATTN), F32),
                        pltpu.VMEM((ATTN_SUBTILES, 2, tk, TQ_ATTN), BF16)],
        compiler_params=_params("parallel", "parallel", "arbitrary"),
        name="mla_attention",
    )(q, k, vt)


def _proj_res_kernel(x_ref, o_ref, mod_ref, w_ref, out_ref):
    out = jnp.dot(o_ref[...], w_ref[...], preferred_element_type=F32)
    out_ref[...] = x_ref[...] + mod_ref[0, 2:3, :] * out


def _proj_res_call(x, o, mod, w_o, seg):
    t, d = x.shape
    tm = TM_OUT
    return pl.pallas_call(
        _proj_res_kernel,
        grid=(t // tm,),
        in_specs=[
            pl.BlockSpec((tm, d), lambda i: (i, 0)),
            pl.BlockSpec((tm, o.shape[1]), lambda i: (i, 0)),
            pl.BlockSpec((1, 6, d), lambda i: ((i * tm) // seg, 0, 0)),
            _const_spec(w_o.shape),
        ],
        out_specs=pl.BlockSpec((tm, d), lambda i: (i, 0)),
        out_shape=jax.ShapeDtypeStruct((t, d), F32),
        compiler_params=_params("parallel"),
        name="mla_out_proj",
    )(x, o, mod, w_o.astype(BF16))


def _router_kernel(x_ref, mod_ref, g_ref, wr_ref, br_ref,
                   h_ref, ids_ref, gates_ref, rank_ref, gcol_ref, cnt_ref, carry_ref):
    i = pl.program_id(0)

    @pl.when(i == 0)
    def _():
        carry_ref[...] = jnp.zeros_like(carry_ref)

    x = x_ref[...]
    tm = x.shape[0]
    h = _norm_mod(x, g_ref[...], mod_ref[0, 3:4, :], mod_ref[0, 4:5, :])
    h_ref[...] = _pack_rows(h)
    logits = lax.dot_general(wr_ref[...], h, (((1,), (1,)), ((), ())),
                             precision=lax.Precision.HIGHEST,
                             preferred_element_type=F32) + br_ref[...]
    eidx = lax.broadcasted_iota(jnp.int32, logits.shape, 0)
    lg = logits
    sel = jnp.zeros(logits.shape, F32)
    vals, ids = [], []
    for _ in range(TOP_K):
        m = jnp.max(lg, axis=0, keepdims=True)
        idx = jnp.min(jnp.where(lg == m, eidx, N_EXPERTS), axis=0, keepdims=True)
        onehot = eidx == idx
        vals.append(m)
        ids.append(idx)
        lg = jnp.where(onehot, -jnp.inf, lg)
        sel = jnp.where(onehot, 1.0, sel)
    ex = [jnp.exp(v - vals[0]) for v in vals]
    den = ex[0] + ex[1] + ex[2] + ex[3]
    gates = [e / den for e in ex]
    r_io = lax.broadcasted_iota(jnp.int32, (tm, tm), 0)
    c_io = lax.broadcasted_iota(jnp.int32, (tm, tm), 1)
    before = jnp.where(r_io < c_io, 1.0, 0.0).astype(BF16)
    cum = jnp.dot(sel.astype(BF16), before, preferred_element_type=F32) + carry_ref[:, 0:1]
    for k in range(TOP_K):
        rk = jnp.sum(jnp.where(eidx == ids[k], cum, 0.0), axis=0, keepdims=True)
        rank_ref[k:k + 1, :] = rk.astype(jnp.int32)
        ids_ref[k:k + 1, :] = ids[k]
        gates_ref[k:k + 1, :] = gates[k]
    g8 = jnp.concatenate(gates + [jnp.zeros((LANES - TOP_K, tm), F32)], axis=0)
    gcol_ref[...] = g8.T
    carry_ref[...] = carry_ref[...] + jnp.sum(sel, axis=1, keepdims=True)
    cnt_ref[...] = carry_ref[...]


def _router_call(x, mod, g, w_r, b_r, seg):
    t, d = x.shape
    tm = TM_ROUTER
    e = w_r.shape[1]
    return pl.pallas_call(
        _router_kernel,
        grid=(t // tm,),
        in_specs=[
            pl.BlockSpec((tm, d), lambda i: (i, 0)),
            pl.BlockSpec((1, 6, d), lambda i: ((i * tm) // seg, 0, 0)),
            _const_spec((1, d)),
            _const_spec((e, d)),
            _const_spec((e, 1)),
        ],
        out_specs=[
            pl.BlockSpec((tm, d // 2), lambda i: (i, 0)),
            pl.BlockSpec((TOP_K, tm), lambda i: (0, i)),
            pl.BlockSpec((TOP_K, tm), lambda i: (0, i)),
            pl.BlockSpec((TOP_K, tm), lambda i: (0, i)),
            pl.BlockSpec((tm, LANES), lambda i: (i, 0)),
            _const_spec((e, LANES)),
        ],
        out_shape=[
            jax.ShapeDtypeStruct((t, d // 2), jnp.int32),
            jax.ShapeDtypeStruct((TOP_K, t), jnp.int32),
            jax.ShapeDtypeStruct((TOP_K, t), F32),
            jax.ShapeDtypeStruct((TOP_K, t), jnp.int32),
            jax.ShapeDtypeStruct((t, LANES), F32),
            jax.ShapeDtypeStruct((e, LANES), F32),
        ],
        scratch_shapes=[pltpu.VMEM((e, LANES), F32)],
        compiler_params=_params("arbitrary"),
        name="moe_router",
    )(x, mod, g.reshape(1, d), w_r.T, b_r.reshape(e, 1))


SC_CORES = 2
SC_SUBCORES = 16
SC_WORKERS = SC_CORES * SC_SUBCORES
SC_CHUNK = 64


def _dispatch_rows(h, dest, pad_idx, rows):
    t, d = h.shape
    top_k = dest.shape[0]
    per_w = t // SC_WORKERS
    n_ch = per_w // SC_CHUNK
    pad_ch = pad_idx.shape[0] // (SC_WORKERS * SC_CHUNK)
    assert n_ch * SC_CHUNK * SC_WORKERS == t
    assert pad_ch * SC_CHUNK * SC_WORKERS == pad_idx.shape[0]
    assert top_k * t + pad_idx.shape[0] == rows
    mesh = plsc.VectorSubcoreMesh(core_axis_name="c", subcore_axis_name="s")

    @functools.partial(
        pl.kernel, mesh=mesh,
        out_type=jax.ShapeDtypeStruct((rows, d), h.dtype),
        scratch_types=[pltpu.VMEM((top_k * n_ch, SC_CHUNK), jnp.int32),
                       pltpu.VMEM((pad_ch, SC_CHUNK), jnp.int32),
                       pltpu.VMEM((SC_CHUNK, d), h.dtype),
                       pltpu.SemaphoreType.DMA],
        name="sc_row_dispatch",
    )
    def dispatch(h_hbm, dest_hbm, pad_hbm, zero_hbm, out_hbm, idx_v, pad_v, rows_v, sem):
        wid = lax.axis_index("s") * SC_CORES + lax.axis_index("c")
        base = wid * per_w
        pltpu.sync_copy(dest_hbm.at[wid], idx_v)
        pltpu.sync_copy(pad_hbm.at[wid], pad_v)

        @pl.loop(0, n_ch)
        def _(j):
            pltpu.sync_copy(h_hbm.at[pl.ds(base + j * SC_CHUNK, SC_CHUNK)], rows_v)
            copies = [pltpu.async_copy(rows_v, out_hbm.at[idx_v.at[k * n_ch + j]], sem)
                      for k in range(top_k)]
            for c in copies:
                c.wait()

        pltpu.sync_copy(zero_hbm, rows_v)

        @pl.loop(0, pad_ch)
        def _(p):
            pltpu.sync_copy(rows_v, out_hbm.at[pad_v.at[p]])

    dest_w = dest.reshape(top_k, SC_WORKERS, n_ch, SC_CHUNK).transpose(1, 0, 2, 3)
    return dispatch(h, dest_w.reshape(SC_WORKERS, top_k * n_ch, SC_CHUNK),
                    pad_idx.reshape(SC_WORKERS, pad_ch, SC_CHUNK),
                    jnp.zeros((SC_CHUNK, d), h.dtype))


def _gather_rows(src, idx):
    n = idx.shape[0]
    d = src.shape[1]
    per_w = n // SC_WORKERS
    n_ch = per_w // SC_CHUNK
    assert per_w * SC_WORKERS == n and n_ch * SC_CHUNK == per_w
    mesh = plsc.VectorSubcoreMesh(core_axis_name="c", subcore_axis_name="s")

    @functools.partial(
        pl.kernel, mesh=mesh,
        out_type=jax.ShapeDtypeStruct((n, d), src.dtype),
        scratch_types=[pltpu.VMEM((n_ch, SC_CHUNK), jnp.int32),
                       pltpu.VMEM((SC_CHUNK, d), src.dtype),
                       pltpu.SemaphoreType.DMA],
        name="sc_row_gather",
    )
    def gather(src_hbm, idx_hbm, out_hbm, idx_v, rows_v, sem):
        wid = lax.axis_index("s") * SC_CORES + lax.axis_index("c")
        base = wid * per_w
        pltpu.sync_copy(idx_hbm.at[wid], idx_v)

        @pl.loop(0, n_ch)
        def _(j):
            pltpu.async_copy(src_hbm.at[idx_v.at[j]], rows_v, sem).wait()
            pltpu.sync_copy(rows_v, out_hbm.at[pl.ds(base + j * SC_CHUNK, SC_CHUNK)])

    return gather(src, idx.reshape(SC_WORKERS, n_ch, SC_CHUNK))


def _expert_kernel(blk_e_ref, nused_ref, x_ref, wgu_ref, bgu_ref, wdn_ref, bdn_ref,
                   o_ref, wgu_bf, wdn_bf):
    i = pl.program_id(0)
    e = blk_e_ref[i]
    e_prev = blk_e_ref[jnp.maximum(i - 1, 0)]
    active = i < nused_ref[0]
    d_e = wdn_ref.shape[2]

    @pl.when(active & ((i == 0) | (e != e_prev)))
    def _():
        step = 128
        for r in range(0, wgu_ref.shape[2], step):
            wgu_bf[r:r + step, :] = wgu_ref[0, 0, r:r + step, :].astype(BF16)
        for r in range(0, d_e, step):
            wdn_bf[r:r + step, :] = wdn_ref[0, 0, r:r + step, :].astype(BF16)

    @pl.when(active)
    def _():
        lo, hi = _unpack_rows(x_ref[...])
        xb = jnp.concatenate([lo.astype(BF16), hi.astype(BF16)], axis=1)
        gu = jnp.dot(xb, wgu_bf[...], preferred_element_type=F32) + bgu_ref[0, 0]
        gt = jnp.minimum(gu[:, :d_e], SWIGLU_LIMIT)
        up = jnp.clip(gu[:, d_e:], -SWIGLU_LIMIT, SWIGLU_LIMIT)
        a = (up + 1.0) * (gt * jax.nn.sigmoid(SWIGLU_ALPHA * gt))
        o_ref[...] = _pack_rows(jnp.dot(a.astype(BF16), wdn_bf[...],
                                        preferred_element_type=F32) + bdn_ref[0, 0])

    @pl.when(jnp.logical_not(active))
    def _():
        o_ref[...] = jnp.zeros_like(o_ref)


def _expert_call(xg, blk_e, n_used, w_gu, b_gu, w_dn, b_dn, layer):
    rows, half = xg.shape
    d = 2 * half
    nb = rows // MOE_ROWS
    _, n_e, _, two_de = w_gu.shape
    d_e = two_de // 2
    grid_spec = pltpu.PrefetchScalarGridSpec(
        num_scalar_prefetch=2,
        grid=(nb,),
        in_specs=[
            pl.BlockSpec((MOE_ROWS, half), lambda i, be, nu: (i, 0)),
            pl.BlockSpec((1, 1, d, two_de), lambda i, be, nu: (layer, be[i], 0, 0)),
            pl.BlockSpec((1, 1, 1, two_de), lambda i, be, nu: (layer, be[i], 0, 0)),
            pl.BlockSpec((1, 1, d_e, d), lambda i, be, nu: (layer, be[i], 0, 0)),
            pl.BlockSpec((1, 1, 1, d), lambda i, be, nu: (layer, be[i], 0, 0)),
        ],
        out_specs=pl.BlockSpec((MOE_ROWS, half), lambda i, be, nu: (i, 0)),
        scratch_shapes=[pltpu.VMEM((d, two_de), BF16), pltpu.VMEM((d_e, d), BF16)],
    )
    return pl.pallas_call(
        _expert_kernel,
        grid_spec=grid_spec,
        out_shape=jax.ShapeDtypeStruct((rows, half), jnp.int32),
        compiler_params=_params("arbitrary"),
        name="moe_experts",
    )(blk_e, n_used, xg, w_gu, b_gu.reshape(b_gu.shape[0], n_e, 1, two_de),
      w_dn, b_dn.reshape(b_dn.shape[0], n_e, 1, d))


def _combine_kernel(x_ref, y_ref, gcol_ref, mod_ref, o_ref):
    gc = gcol_ref[...]
    half = y_ref.shape[2]
    acc_lo = acc_hi = None
    for k in range(TOP_K):
        lo, hi = _unpack_rows(y_ref[k])
        gate = gc[:, k:k + 1]
        acc_lo = gate * lo if k == 0 else acc_lo + gate * lo
        acc_hi = gate * hi if k == 0 else acc_hi + gate * hi
    o_ref[:, :half] = x_ref[:, :half] + mod_ref[0, 5:6, :half] * acc_lo
    o_ref[:, half:] = x_ref[:, half:] + mod_ref[0, 5:6, half:] * acc_hi


def _combine_call(x, yk, gcol, mod, seg):
    t, d = x.shape
    tm = TM_OUT
    return pl.pallas_call(
        _combine_kernel,
        grid=(t // tm,),
        in_specs=[
            pl.BlockSpec((tm, d), lambda i: (i, 0)),
            pl.BlockSpec((TOP_K, tm, d // 2), lambda i: (0, i, 0)),
            pl.BlockSpec((tm, LANES), lambda i: (i, 0)),
            pl.BlockSpec((1, 6, d), lambda i: ((i * tm) // seg, 0, 0)),
        ],
        out_specs=pl.BlockSpec((tm, d), lambda i: (i, 0)),
        out_shape=jax.ShapeDtypeStruct((t, d), F32),
        compiler_params=_params("parallel"),
        name="moe_combine",
    )(x, yk, gcol, mod)


def _moe_layer(x, mod, g, w_r, b_r, w_gu, b_gu, w_dn, b_dn, layer, seg):
    t, d = x.shape
    h, ids, _, rank, gcol, cnt = _router_call(x, mod, g, w_r, b_r, seg)
    counts = cnt[:, 0].astype(jnp.int32)
    padded = (counts + MOE_ROWS - 1) // MOE_ROWS * MOE_ROWS
    pad_end = jnp.cumsum(padded)
    pad_start = pad_end - padded
    e_ar = jnp.arange(N_EXPERTS, dtype=jnp.int32)
    dest = rank + jnp.sum(jnp.where(ids[..., None] == e_ar, pad_start, 0), axis=-1)
    nb = -(-(t * TOP_K) // MOE_ROWS) + N_EXPERTS
    rows = nb * MOE_ROWS
    slack = padded - counts
    slack_end = jnp.cumsum(slack)
    r = jnp.arange(rows - t * TOP_K, dtype=jnp.int32)
    owner = jnp.sum((slack_end[None, :] <= r[:, None]).astype(jnp.int32), axis=1)
    first = pad_start + counts - (slack_end - slack)
    in_expert = r + jnp.sum(jnp.where(owner[:, None] == e_ar, first, 0), axis=-1)
    pad_idx = jnp.where(owner < N_EXPERTS, in_expert, pad_end[-1] + r - slack_end[-1])
    blk_row = jnp.arange(nb, dtype=jnp.int32) * MOE_ROWS
    blk_e = jnp.minimum(jnp.sum((pad_end[None, :] <= blk_row[:, None]).astype(jnp.int32), axis=1),
                        N_EXPERTS - 1)
    n_used = (pad_end[-1:] // MOE_ROWS).astype(jnp.int32)
    xg = _dispatch_rows(h, dest, pad_idx, rows)
    yb = _expert_call(xg, blk_e, n_used, w_gu, b_gu, w_dn, b_dn, layer)
    yk = _gather_rows(yb, dest.reshape(-1)).reshape(TOP_K, t, d // 2)
    return _combine_call(x, yk, gcol, mod, seg)


def kernel(x_prompt, x_sample, c_prompt, c_sample, g_mix, g_ffn, w_ada, b_ada, gm_w_in, gm_g_v, gm_w_s, gm_b_s, gm_w_out, sc_w_in, sc_w_conv, sc_w_out, mla_w_down, mla_g_q_lat, mla_g_kv_lat, mla_w_uq, mla_w_ukv, mla_g_qh, mla_g_kh, mla_w_o, moe_w_router, moe_b_router, moe_w_gu, moe_b_gu, moe_w_dn, moe_b_dn):
    bp, sp, d = x_prompt.shape
    bs, ss, _ = x_sample.shape
    depth = g_mix.shape[0]

    n_seq = bp + bs
    c_all = jnp.concatenate([c_prompt, c_sample], axis=0)
    c_pad = jnp.pad(c_all, ((0, (-n_seq) % 8), (0, 0)))
    mod_all = _ada_call(c_pad, w_ada, b_ada)[:, :n_seq].reshape(depth, n_seq, 6, d)

    gm = [(gm_w_in[j].astype(BF16), gm_g_v[j], gm_w_s[j].astype(BF16), gm_b_s[j],
           gm_w_out[j].astype(BF16)) for j in range(gm_w_in.shape[0])]
    sc = [(sc_w_in[j].astype(BF16), sc_w_conv[j], sc_w_out[j].astype(BF16))
          for j in range(sc_w_in.shape[0])]
    mla = [_mla_prepare_weights(mla_w_down[j], mla_w_uq[j], mla_w_ukv[j], mla_g_qh[j],
                                mla_g_kh[j]) + (mla_w_o[j].astype(BF16),)
           for j in range(mla_w_down.shape[0])]
    cos, sin = _rope_tables(max(sp, ss))

    def layer(i, x, mod, n_s, s_len):
        t = x.shape[0]
        kind, j = i % 3, i // 3
        if kind == 0:
            x = _gmlp_call(x, mod, g_mix[i], *gm[j], s_len)
        elif kind == 1:
            pos = (jnp.arange(t // TM_CONV, dtype=jnp.int32) * TM_CONV) % s_len
            w_in, w_conv, w_out = sc[j]
            x = _conv_call(x, mod, g_mix[i], w_in, w_conv, w_out, s_len,
                           (pos != 0).astype(jnp.int32),
                           (pos + TM_CONV != s_len).astype(jnp.int32))
        else:
            wd, uq, ukv, gh, w_o = mla[j]
            pos = (jnp.arange(t // TM_MLA, dtype=jnp.int32) * TM_MLA) % s_len
            q, k, vt = _mla_pre_call(x, mod, g_mix[i], wd, mla_g_q_lat[j], mla_g_kv_lat[j],
                                     uq, ukv, gh, cos, sin, s_len, pos // TM_MLA)
            o = _attn_call(q, k, vt, 0, n_s, s_len)
            x = _proj_res_call(x, o, mod, w_o, s_len)
        return _moe_layer(x, mod, g_ffn[i], moe_w_router[i], moe_b_router[i],
                          moe_w_gu, moe_b_gu, moe_w_dn, moe_b_dn, i, s_len)

    xp = x_prompt.reshape(bp * sp, d)
    xs = x_sample.reshape(bs * ss, d)
    for i in range(depth):
        xp = layer(i, xp, mod_all[i, :bp], bp, sp)
        xs = layer(i, xs, mod_all[i, bp:], bs, ss)
    return (xp.reshape(bp, sp, d), xs.reshape(bs, ss, d))
```

```python
import functools

import jax
import jax.numpy as jnp
from jax import lax
from jax.experimental import pallas as pl
from jax.experimental.pallas import tpu as pltpu
from jax.experimental.pallas import tpu_sc as plsc

F32 = jnp.float32
BF16 = jnp.bfloat16

NORM_EPS = 1e-6
GM_GROUPS = 8
GM_CHUNK = 128
MLA_HEADS = 8
MLA_Q_RANK = 384
MLA_KV_RANK = 256
MLA_NOPE = 128
MLA_ROPE = 64
MLA_V = 128
MLA_DK = MLA_NOPE + MLA_ROPE
ROPE_THETA = 10000.0
N_EXPERTS = 32
TOP_K = 4
SWIGLU_LIMIT = 7.0
SWIGLU_ALPHA = 1.702
LOG2_E = 1.4426950408889634

LANES = 128
HALO = 16
VMEM_LIMIT = 56 * 1024 * 1024

TM_GMLP = 256
TM_CONV = 512
TM_MLA = 256
TM_ROUTER = 512
TM_OUT = 512
TQ_ATTN = 256
ATTN_SUBTILES = 2
TK_ATTN = 2048
MOE_ROWS = 512
MOE_SKIP = 256


def _params(*sem):
    return pltpu.CompilerParams(dimension_semantics=sem, vmem_limit_bytes=VMEM_LIMIT)


def _const_spec(shape):
    nd = len(shape)
    return pl.BlockSpec(shape, lambda *_: (0,) * nd)


def _norm_mod(x, g, shift, scale):
    ms = jnp.mean(x * x, axis=-1, keepdims=True)
    return (x * lax.rsqrt(ms + NORM_EPS)) * g * (1.0 + scale) + shift


def _pack_rows(x):
    half = x.shape[1] // 2
    lo = pltpu.bitcast(x[:, :half].astype(BF16).astype(F32), jnp.uint32)
    hi = pltpu.bitcast(x[:, half:].astype(BF16).astype(F32), jnp.uint32)
    word = lax.shift_right_logical(lo, jnp.uint32(16)) | (hi & jnp.uint32(0xFFFF0000))
    return pltpu.bitcast(word, jnp.int32)


def _unpack_rows(w):
    u = pltpu.bitcast(w, jnp.uint32)
    lo = pltpu.bitcast(lax.shift_left(u, jnp.uint32(16)), F32)
    hi = pltpu.bitcast(u & jnp.uint32(0xFFFF0000), F32)
    return lo, hi


def _ada_kernel(c_ref, w_ref, b_ref, o_ref):
    c = c_ref[...]
    a = (c * jax.nn.sigmoid(c)).astype(BF16)
    w = w_ref[0].astype(BF16)
    o_ref[0] = jnp.dot(a, w, preferred_element_type=F32) + b_ref[0]


def _ada_call(c_pad, w_ada, b_ada):
    depth, d, n = w_ada.shape
    tn = 1536
    rows = c_pad.shape[0]
    return pl.pallas_call(
        _ada_kernel,
        grid=(depth, n // tn),
        in_specs=[
            pl.BlockSpec((rows, d), lambda l, j: (0, 0)),
            pl.BlockSpec((1, d, tn), lambda l, j: (l, 0, j)),
            pl.BlockSpec((1, 1, tn), lambda l, j: (l, 0, j)),
        ],
        out_specs=pl.BlockSpec((1, rows, tn), lambda l, j: (l, 0, j)),
        out_shape=jax.ShapeDtypeStruct((depth, rows, n), F32),
        compiler_params=_params("arbitrary", "arbitrary"),
        name="adaln_mod",
    )(c_pad, w_ada, b_ada.reshape(depth, 1, n))


def _gmlp_kernel(x_ref, mod_ref, g_ref, win_ref, gv_ref, ws_ref, bs_ref, wout_ref,
                 o_ref, gated_ref):
    x = x_ref[...]
    tm = x.shape[0]
    width = gv_ref.shape[1]
    gw = width // GM_GROUPS
    h = _norm_mod(x, g_ref[...], mod_ref[0, 0:1, :], mod_ref[0, 1:2, :]).astype(BF16)
    uv = jax.nn.gelu(jnp.dot(h, win_ref[...], preferred_element_type=F32))
    u = uv[:, :width]
    v = uv[:, width:]
    v = v * lax.rsqrt(jnp.mean(v * v, axis=-1, keepdims=True) + NORM_EPS) * gv_ref[...]
    vb = v.astype(BF16)
    bs = bs_ref[...]
    for c in range(tm // GM_CHUNK):
        r0 = c * GM_CHUNK
        for g in range(GM_GROUPS):
            c0 = g * gw
            vm = jnp.dot(ws_ref[g], vb[r0:r0 + GM_CHUNK, c0:c0 + gw],
                         preferred_element_type=F32) + bs[:, g:g + 1]
            gated_ref[r0:r0 + GM_CHUNK, c0:c0 + gw] = (
                u[r0:r0 + GM_CHUNK, c0:c0 + gw] * vm).astype(BF16)
    out = jnp.dot(gated_ref[...], wout_ref[...], preferred_element_type=F32)
    o_ref[...] = x + mod_ref[0, 2:3, :] * out


def _gmlp_call(x, mod, g, w_in, g_v, w_s, b_s, w_out, seg):
    t, d = x.shape
    tm = TM_GMLP
    width = g_v.shape[-1]
    return pl.pallas_call(
        _gmlp_kernel,
        grid=(t // tm,),
        in_specs=[
            pl.BlockSpec((tm, d), lambda i: (i, 0)),
            pl.BlockSpec((1, 6, d), lambda i: ((i * tm) // seg, 0, 0)),
            _const_spec((1, d)),
            _const_spec(w_in.shape),
            _const_spec((1, width)),
            _const_spec(w_s.shape),
            _const_spec((GM_CHUNK, GM_GROUPS)),
            _const_spec(w_out.shape),
        ],
        out_specs=pl.BlockSpec((tm, d), lambda i: (i, 0)),
        out_shape=jax.ShapeDtypeStruct((t, d), F32),
        scratch_shapes=[pltpu.VMEM((tm, width), BF16)],
        compiler_params=_params("parallel"),
        name="gmlp_mixer",
    )(x, mod, g.reshape(1, d), w_in.astype(BF16), g_v.reshape(1, width),
      w_s.astype(BF16), b_s.T, w_out.astype(BF16))


def _conv_kernel(start_ref, end_ref, x_ref, xp_ref, xn_ref, mod_ref, g_ref, win_ref,
                 wc_ref, wout_ref, o_ref, h_ref, z_ref):
    i = pl.program_id(0)
    x = x_ref[...]
    tm, d = x.shape
    g = g_ref[...]
    sh = mod_ref[0, 0:1, :]
    sc = mod_ref[0, 1:2, :]
    h_ref[0:HALO, :] = _norm_mod(xp_ref[...], g, sh, sc).astype(BF16)
    h_ref[HALO:HALO + tm, :] = _norm_mod(x, g, sh, sc).astype(BF16)
    h_ref[HALO + tm:, :] = _norm_mod(xn_ref[...], g, sh, sc).astype(BF16)
    bcx = jnp.dot(h_ref[...], win_ref[...], preferred_element_type=F32)
    w = bcx.shape[1] // 3
    bg = bcx[HALO:HALO + tm, :w]
    z_ref[...] = bcx[:, w:2 * w] * bcx[:, 2 * w:]
    keep_prev = (start_ref[i] == 0).astype(F32)
    keep_next = (end_ref[i] == 0).astype(F32)
    z_ref[0:HALO, :] = z_ref[0:HALO, :] * keep_prev
    z_ref[HALO + tm:, :] = z_ref[HALO + tm:, :] * keep_next
    y = (wc_ref[0:1, :] * z_ref[HALO - 1:HALO - 1 + tm, :]
         + wc_ref[1:2, :] * z_ref[HALO:HALO + tm, :]
         + wc_ref[2:3, :] * z_ref[HALO + 1:HALO + 1 + tm, :])
    out = jnp.dot((bg * y).astype(BF16), wout_ref[...], preferred_element_type=F32)
    o_ref[...] = x + mod_ref[0, 2:3, :] * out


def _conv_call(x, mod, g, w_in, w_conv, w_out, seg, tile_start, tile_end):
    t, d = x.shape
    tm = TM_CONV
    w = w_conv.shape[-1]
    hb = tm // HALO
    last = t // HALO - 1
    grid_spec = pltpu.PrefetchScalarGridSpec(
        num_scalar_prefetch=2,
        grid=(t // tm,),
        in_specs=[
            pl.BlockSpec((tm, d), lambda i, *_: (i, 0)),
            pl.BlockSpec((HALO, d), lambda i, *_: (jnp.maximum(i * hb - 1, 0), 0)),
            pl.BlockSpec((HALO, d), lambda i, *_: (jnp.minimum((i + 1) * hb, last), 0)),
            pl.BlockSpec((1, 6, d), lambda i, *_: ((i * tm) // seg, 0, 0)),
            _const_spec((1, d)),
            _const_spec(w_in.shape),
            _const_spec(w_conv.shape),
            _const_spec(w_out.shape),
        ],
        out_specs=pl.BlockSpec((tm, d), lambda i, *_: (i, 0)),
        scratch_shapes=[pltpu.VMEM((tm + 2 * HALO, d), BF16),
                        pltpu.VMEM((tm + 2 * HALO, w), F32)],
    )
    return pl.pallas_call(
        _conv_kernel,
        grid_spec=grid_spec,
        out_shape=jax.ShapeDtypeStruct((t, d), F32),
        compiler_params=_params("parallel"),
        name="shortconv_mixer",
    )(tile_start, tile_end, x, x, x, mod, g.reshape(1, d), w_in.astype(BF16),
      w_conv, w_out.astype(BF16))


QH = 3 * LANES
KH = 2 * LANES


def _mla_pre_kernel(pos_ref, x_ref, mod_ref, g_ref, wd_ref, gq_ref, gkv_ref, wuq_ref,
                    wukv_ref, gh_ref, cos_ref, sin_ref, q_ref, k_ref, v_ref):
    del pos_ref
    x = x_ref[...]
    h = _norm_mod(x, g_ref[...], mod_ref[0, 0:1, :], mod_ref[0, 1:2, :]).astype(BF16)
    lat = jnp.dot(h, wd_ref[...], preferred_element_type=F32)
    ql = lat[:, :MLA_Q_RANK]
    kvl = lat[:, MLA_Q_RANK:MLA_Q_RANK + MLA_KV_RANK]
    pe = lat[:, MLA_Q_RANK + MLA_KV_RANK:MLA_Q_RANK + MLA_KV_RANK + LANES]
    pe_sw = lat[:, MLA_Q_RANK + MLA_KV_RANK + LANES:]
    ql = ql * lax.rsqrt(jnp.mean(ql * ql, axis=-1, keepdims=True) + NORM_EPS) * gq_ref[...]
    kvl = kvl * lax.rsqrt(jnp.mean(kvl * kvl, axis=-1, keepdims=True) + NORM_EPS) * gkv_ref[...]
    q = jnp.dot(ql.astype(BF16), wuq_ref[...], preferred_element_type=F32)
    kv = jnp.dot(kvl.astype(BF16), wukv_ref[...], preferred_element_type=F32)
    cos = cos_ref[...]
    sin = sin_ref[...]
    gqn, gqr, gqs = gh_ref[0:1, :], gh_ref[1:2, :], gh_ref[2:3, :]
    gkn, gkr, gks = gh_ref[3:4, :], gh_ref[4:5, :], gh_ref[5:6, :]
    pe_ss = jnp.sum(pe * pe, axis=-1, keepdims=True)
    qscale = MLA_DK ** -0.5 * LOG2_E
    for hd in range(MLA_HEADS):
        qn = q[:, hd * QH:hd * QH + LANES]
        qr = q[:, hd * QH + LANES:hd * QH + 2 * LANES]
        qs = q[:, hd * QH + 2 * LANES:(hd + 1) * QH]
        ss = jnp.sum(qn * qn, axis=-1, keepdims=True) + jnp.sum(qr * qr, axis=-1, keepdims=True)
        rinv = lax.rsqrt(ss * (1.0 / MLA_DK) + NORM_EPS) * qscale
        q_ref[hd, :, 0:LANES] = (qn * rinv * gqn).astype(BF16)
        q_ref[hd, :, LANES:] = ((qr * rinv * gqr) * cos + (qs * rinv * gqs) * sin).astype(BF16)
        kn = kv[:, hd * 2 * LANES:hd * 2 * LANES + LANES]
        ss = jnp.sum(kn * kn, axis=-1, keepdims=True) + pe_ss
        rinv = lax.rsqrt(ss * (1.0 / MLA_DK) + NORM_EPS)
        k_ref[hd, :, 0:LANES] = (kn * rinv * gkn).astype(BF16)
        k_ref[hd, :, LANES:] = ((pe * rinv * gkr) * cos + (pe_sw * rinv * gks) * sin).astype(BF16)
        v_ref[hd] = kv[:, hd * 2 * LANES + LANES:(hd + 1) * 2 * LANES].T.astype(BF16)


def _mla_prepare_weights(w_down, w_uq, w_ukv, g_qh, g_kh):
    half = MLA_ROPE // 2
    pad = LANES - MLA_ROPE

    def swap(a):
        return jnp.concatenate([a[..., half:], a[..., :half]], axis=-1)

    def pad_lanes(a):
        return jnp.pad(a, [(0, 0)] * (a.ndim - 1) + [(0, pad)])

    d = w_down.shape[0]
    pe_w = w_down[:, MLA_Q_RANK + MLA_KV_RANK:]
    wd = jnp.concatenate([w_down[:, :MLA_Q_RANK + MLA_KV_RANK],
                          pad_lanes(pe_w), pad_lanes(swap(pe_w))], axis=1)
    uq = w_uq.reshape(MLA_Q_RANK, MLA_HEADS, MLA_DK)
    uq_r = uq[..., MLA_NOPE:]
    uq = jnp.concatenate([uq[..., :MLA_NOPE], pad_lanes(uq_r), pad_lanes(swap(uq_r))], axis=-1)
    uq = uq.reshape(MLA_Q_RANK, MLA_HEADS * QH)

    def gains(gv):
        r = gv[MLA_NOPE:]
        return [gv[:MLA_NOPE], pad_lanes(r), pad_lanes(swap(r))]

    gh = jnp.stack(gains(g_qh) + gains(g_kh) + [jnp.zeros((LANES,), F32)] * 2)
    return wd.astype(BF16), uq.astype(BF16), w_ukv.astype(BF16), gh


def _rope_tables(s):
    half = MLA_ROPE // 2
    inv = 1.0 / (ROPE_THETA ** (jnp.arange(0, MLA_ROPE, 2, dtype=F32) / MLA_ROPE))
    ang = jnp.arange(s, dtype=F32)[:, None] * inv[None, :]
    cos, sin = jnp.cos(ang), jnp.sin(ang)
    zeros = jnp.zeros((s, LANES - MLA_ROPE), F32)
    del half
    return (jnp.concatenate([cos, cos, zeros], axis=1),
            jnp.concatenate([-sin, sin, zeros], axis=1))


def _mla_pre_call(x, mod, g, wd, g_q_lat, g_kv_lat, uq, ukv, gh, cos, sin, seg, tile_pos):
    t, d = x.shape
    tm = TM_MLA
    grid_spec = pltpu.PrefetchScalarGridSpec(
        num_scalar_prefetch=1,
        grid=(t // tm,),
        in_specs=[
            pl.BlockSpec((tm, d), lambda i, p: (i, 0)),
            pl.BlockSpec((1, 6, d), lambda i, p: ((i * tm) // seg, 0, 0)),
            _const_spec((1, d)),
            _const_spec(wd.shape),
            _const_spec((1, MLA_Q_RANK)),
            _const_spec((1, MLA_KV_RANK)),
            _const_spec(uq.shape),
            _const_spec(ukv.shape),
            _const_spec(gh.shape),
            pl.BlockSpec((tm, LANES), lambda i, p: (p[i], 0)),
            pl.BlockSpec((tm, LANES), lambda i, p: (p[i], 0)),
        ],
        out_specs=[
            pl.BlockSpec((MLA_HEADS, tm, KH), lambda i, p: (0, i, 0)),
            pl.BlockSpec((MLA_HEADS, tm, KH), lambda i, p: (0, i, 0)),
            pl.BlockSpec((MLA_HEADS, MLA_V, tm), lambda i, p: (0, 0, i)),
        ],
    )
    return pl.pallas_call(
        _mla_pre_kernel,
        grid_spec=grid_spec,
        out_shape=[jax.ShapeDtypeStruct((MLA_HEADS, t, KH), BF16),
                   jax.ShapeDtypeStruct((MLA_HEADS, t, KH), BF16),
                   jax.ShapeDtypeStruct((MLA_HEADS, MLA_V, t), BF16)],
        compiler_params=_params("parallel"),
        name="mla_qkv",
    )(tile_pos, x, mod, g.reshape(1, d), wd, g_q_lat.reshape(1, -1), g_kv_lat.reshape(1, -1),
      uq, ukv, gh, cos, sin)


def _attn_kernel(q_ref, k_ref, vt_ref, o_ref, s_buf, p_buf):
    tk = TK_ATTN
    n = k_ref.shape[1] // tk
    nt = (((1,), (1,)), ((), ()))
    tq = q_ref.shape[1] // ATTN_SUBTILES
    subs = range(ATTN_SUBTILES)
    qs = [q_ref[0, sub * tq:(sub + 1) * tq, :] for sub in subs]

    def scores(sub, j):
        return lax.dot_general(k_ref[0, j * tk:(j + 1) * tk, :], qs[sub], nt,
                               preferred_element_type=F32)

    def values(sub, j):
        return jnp.dot(vt_ref[0, :, j * tk:(j + 1) * tk], p_buf[sub, j % 2],
                       preferred_element_type=F32)

    m = [jnp.full((1, tq), -jnp.inf, F32) for _ in subs]
    l = [jnp.zeros((1, tq), F32) for _ in subs]
    acc = [jnp.zeros((MLA_V, tq), F32) for _ in subs]
    alpha = [None for _ in subs]
    for sub in subs:
        s_buf[sub, 0] = scores(sub, 0)
    for j in range(n):
        cur = j % 2
        for sub in subs:
            if j + 1 < n:
                s_buf[sub, 1 - cur] = scores(sub, j + 1)
            if j > 0:
                acc[sub] = alpha[sub] * acc[sub] + values(sub, j - 1)
            s = s_buf[sub, cur]
            m_new = jnp.maximum(m[sub], jnp.max(s, axis=0, keepdims=True))
            alpha[sub] = jnp.exp2(m[sub] - m_new)
            p = jnp.exp2(s - m_new)
            l[sub] = alpha[sub] * l[sub] + jnp.sum(p, axis=0, keepdims=True)
            m[sub] = m_new
            p_buf[sub, cur] = p.astype(BF16)
    for sub in subs:
        a = alpha[sub] * acc[sub] + values(sub, n - 1)
        o_ref[sub * tq:(sub + 1) * tq, :] = (a / l[sub]).T.astype(BF16)


def _attn_call(q, k, vt, tok0, n_seq, s_len):
    tq, tk = TQ_ATTN * ATTN_SUBTILES, TK_ATTN
    assert s_len % tk == 0
    nq = s_len // tq
    qb0 = tok0 // tq
    sb0 = tok0 // s_len
    return pl.pallas_call(
        _attn_kernel,
        grid=(n_seq, MLA_HEADS, nq),
        in_specs=[
            pl.BlockSpec((1, tq, KH), lambda s, h, i: (h, qb0 + s * nq + i, 0)),
            pl.BlockSpec((1, s_len, KH), lambda s, h, i: (h, sb0 + s, 0)),
            pl.BlockSpec((1, MLA_V, s_len), lambda s, h, i: (h, 0, sb0 + s)),
        ],
        out_specs=pl.BlockSpec((tq, MLA_V), lambda s, h, i: (s * nq + i, h)),
        out_shape=jax.ShapeDtypeStruct((n_seq * s_len, MLA_HEADS * MLA_V), BF16),
        scratch_shapes=[pltpu.VMEM((ATTN_SUBTILES, 2, tk, TQ_ATTN), F32),
                        pltpu.VMEM((ATTN_SUBTILES, 2, tk, TQ_ATTN), BF16)],
        compiler_params=_params("parallel", "parallel", "arbitrary"),
        name="mla_attention",
    )(q, k, vt)


def _proj_res_kernel(x_ref, o_ref, mod_ref, w_ref, out_ref):
    out = jnp.dot(o_ref[...], w_ref[...], preferred_element_type=F32)
    out_ref[...] = x_ref[...] + mod_ref[0, 2:3, :] * out


def _proj_res_call(x, o, mod, w_o, seg):
    t, d = x.shape
    tm = TM_OUT
    return pl.pallas_call(
        _proj_res_kernel,
        grid=(t // tm,),
        in_specs=[
            pl.BlockSpec((tm, d), lambda i: (i, 0)),
            pl.BlockSpec((tm, o.shape[1]), lambda i: (i, 0)),
            pl.BlockSpec((1, 6, d), lambda i: ((i * tm) // seg, 0, 0)),
            _const_spec(w_o.shape),
        ],
        out_specs=pl.BlockSpec((tm, d), lambda i: (i, 0)),
        out_shape=jax.ShapeDtypeStruct((t, d), F32),
        compiler_params=_params("parallel"),
        name="mla_out_proj",
    )(x, o, mod, w_o.astype(BF16))


def _router_kernel(x_ref, mod_ref, g_ref, wr_ref, br_ref,
                   h_ref, ids_ref, gates_ref, rank_ref, gcol_ref, cnt_ref, carry_ref):
    i = pl.program_id(0)

    @pl.when(i == 0)
    def _():
        carry_ref[...] = jnp.zeros_like(carry_ref)

    x = x_ref[...]
    tm = x.shape[0]
    h = _norm_mod(x, g_ref[...], mod_ref[0, 3:4, :], mod_ref[0, 4:5, :])
    h_ref[...] = _pack_rows(h)
    logits = lax.dot_general(wr_ref[...], h, (((1,), (1,)), ((), ())),
                             precision=lax.Precision.HIGHEST,
                             preferred_element_type=F32) + br_ref[...]
    eidx = lax.broadcasted_iota(jnp.int32, logits.shape, 0)
    lg = logits
    sel = jnp.zeros(logits.shape, F32)
    vals, ids = [], []
    for _ in range(TOP_K):
        m = jnp.max(lg, axis=0, keepdims=True)
        idx = jnp.min(jnp.where(lg == m, eidx, N_EXPERTS), axis=0, keepdims=True)
        onehot = eidx == idx
        vals.append(m)
        ids.append(idx)
        lg = jnp.where(onehot, -jnp.inf, lg)
        sel = jnp.where(onehot, 1.0, sel)
    ex = [jnp.exp(v - vals[0]) for v in vals]
    den = ex[0] + ex[1] + ex[2] + ex[3]
    gates = [e / den for e in ex]
    r_io = lax.broadcasted_iota(jnp.int32, (tm, tm), 0)
    c_io = lax.broadcasted_iota(jnp.int32, (tm, tm), 1)
    before = jnp.where(r_io < c_io, 1.0, 0.0).astype(BF16)
    cum = jnp.dot(sel.astype(BF16), before, preferred_element_type=F32) + carry_ref[:, 0:1]
    for k in range(TOP_K):
        rk = jnp.sum(jnp.where(eidx == ids[k], cum, 0.0), axis=0, keepdims=True)
        rank_ref[k:k + 1, :] = rk.astype(jnp.int32)
        ids_ref[k:k + 1, :] = ids[k]
        gates_ref[k:k + 1, :] = gates[k]
    g8 = jnp.concatenate(gates + [jnp.zeros((LANES - TOP_K, tm), F32)], axis=0)
    gcol_ref[...] = g8.T
    carry_ref[...] = carry_ref[...] + jnp.sum(sel, axis=1, keepdims=True)
    cnt_ref[...] = carry_ref[...]


def _router_call(x, mod, g, w_r, b_r, seg):
    t, d = x.shape
    tm = TM_ROUTER
    e = w_r.shape[1]
    return pl.pallas_call(
        _router_kernel,
        grid=(t // tm,),
        in_specs=[
            pl.BlockSpec((tm, d), lambda i: (i, 0)),
            pl.BlockSpec((1, 6, d), lambda i: ((i * tm) // seg, 0, 0)),
            _const_spec((1, d)),
            _const_spec((e, d)),
            _const_spec((e, 1)),
        ],
        out_specs=[
            pl.BlockSpec((tm, d // 2), lambda i: (i, 0)),
            pl.BlockSpec((TOP_K, tm), lambda i: (0, i)),
            pl.BlockSpec((TOP_K, tm), lambda i: (0, i)),
            pl.BlockSpec((TOP_K, tm), lambda i: (0, i)),
            pl.BlockSpec((tm, LANES), lambda i: (i, 0)),
            _const_spec((e, LANES)),
        ],
        out_shape=[
            jax.ShapeDtypeStruct((t, d // 2), jnp.int32),
            jax.ShapeDtypeStruct((TOP_K, t), jnp.int32),
            jax.ShapeDtypeStruct((TOP_K, t), F32),
            jax.ShapeDtypeStruct((TOP_K, t), jnp.int32),
            jax.ShapeDtypeStruct((t, LANES), F32),
            jax.ShapeDtypeStruct((e, LANES), F32),
        ],
        scratch_shapes=[pltpu.VMEM((e, LANES), F32)],
        compiler_params=_params("arbitrary"),
        name="moe_router",
    )(x, mod, g.reshape(1, d), w_r.T, b_r.reshape(e, 1))


SC_CORES = 2
SC_SUBCORES = 16
SC_WORKERS = SC_CORES * SC_SUBCORES
SC_CHUNK = 64


def _dispatch_rows(h, dest, pad_idx, rows):
    t, d = h.shape
    top_k = dest.shape[0]
    per_w = t // SC_WORKERS
    n_ch = per_w // SC_CHUNK
    pad_ch = pad_idx.shape[0] // (SC_WORKERS * SC_CHUNK)
    assert n_ch * SC_CHUNK * SC_WORKERS == t
    assert pad_ch * SC_CHUNK * SC_WORKERS == pad_idx.shape[0]
    assert top_k * t + pad_idx.shape[0] == rows
    mesh = plsc.VectorSubcoreMesh(core_axis_name="c", subcore_axis_name="s")

    @functools.partial(
        pl.kernel, mesh=mesh,
        out_type=jax.ShapeDtypeStruct((rows, d), h.dtype),
        scratch_types=[pltpu.VMEM((top_k * n_ch, SC_CHUNK), jnp.int32),
                       pltpu.VMEM((pad_ch, SC_CHUNK), jnp.int32),
                       pltpu.VMEM((SC_CHUNK, d), h.dtype),
                       pltpu.SemaphoreType.DMA],
        name="sc_row_dispatch",
    )
    def dispatch(h_hbm, dest_hbm, pad_hbm, zero_hbm, out_hbm, idx_v, pad_v, rows_v, sem):
        wid = lax.axis_index("s") * SC_CORES + lax.axis_index("c")
        base = wid * per_w
        pltpu.sync_copy(dest_hbm.at[wid], idx_v)
        pltpu.sync_copy(pad_hbm.at[wid], pad_v)

        @pl.loop(0, n_ch)
        def _(j):
            pltpu.sync_copy(h_hbm.at[pl.ds(base + j * SC_CHUNK, SC_CHUNK)], rows_v)
            copies = [pltpu.async_copy(rows_v, out_hbm.at[idx_v.at[k * n_ch + j]], sem)
                      for k in range(top_k)]
            for c in copies:
                c.wait()

        pltpu.sync_copy(zero_hbm, rows_v)

        @pl.loop(0, pad_ch)
        def _(p):
            pltpu.sync_copy(rows_v, out_hbm.at[pad_v.at[p]])

    dest_w = dest.reshape(top_k, SC_WORKERS, n_ch, SC_CHUNK).transpose(1, 0, 2, 3)
    return dispatch(h, dest_w.reshape(SC_WORKERS, top_k * n_ch, SC_CHUNK),
                    pad_idx.reshape(SC_WORKERS, pad_ch, SC_CHUNK),
                    jnp.zeros((SC_CHUNK, d), h.dtype))


def _gather_rows(src, idx):
    n = idx.shape[0]
    d = src.shape[1]
    per_w = n // SC_WORKERS
    n_ch = per_w // SC_CHUNK
    assert per_w * SC_WORKERS == n and n_ch * SC_CHUNK == per_w
    mesh = plsc.VectorSubcoreMesh(core_axis_name="c", subcore_axis_name="s")

    @functools.partial(
        pl.kernel, mesh=mesh,
        out_type=jax.ShapeDtypeStruct((n, d), src.dtype),
        scratch_types=[pltpu.VMEM((n_ch, SC_CHUNK), jnp.int32),
                       pltpu.VMEM((SC_CHUNK, d), src.dtype),
                       pltpu.SemaphoreType.DMA],
        name="sc_row_gather",
    )
    def gather(src_hbm, idx_hbm, out_hbm, idx_v, rows_v, sem):
        wid = lax.axis_index("s") * SC_CORES + lax.axis_index("c")
        base = wid * per_w
        pltpu.sync_copy(idx_hbm.at[wid], idx_v)

        @pl.loop(0, n_ch)
        def _(j):
            pltpu.async_copy(src_hbm.at[idx_v.at[j]], rows_v, sem).wait()
            pltpu.sync_copy(rows_v, out_hbm.at[pl.ds(base + j * SC_CHUNK, SC_CHUNK)])

    return gather(src, idx.reshape(SC_WORKERS, n_ch, SC_CHUNK))


def _expert_kernel(blk_e_ref, nvalid_ref, x_ref, wgu_ref, bgu_ref, wdn_ref, bdn_ref,
                   o_ref, wgu_bf, wdn_bf):
    i = pl.program_id(0)
    e = blk_e_ref[i]
    e_prev = blk_e_ref[jnp.maximum(i - 1, 0)]
    n_valid = nvalid_ref[i]
    active = n_valid > 0
    d_e = wdn_ref.shape[2]

    @pl.when(active & ((i == 0) | (e != e_prev)))
    def _():
        step = 128
        for r in range(0, wgu_ref.shape[2], step):
            wgu_bf[r:r + step, :] = wgu_ref[0, 0, r:r + step, :].astype(BF16)
        for r in range(0, d_e, step):
            wdn_bf[r:r + step, :] = wdn_ref[0, 0, r:r + step, :].astype(BF16)

    for r0 in range(0, MOE_ROWS, MOE_SKIP):
        @pl.when(n_valid > r0)
        def _():
            lo, hi = _unpack_rows(x_ref[r0:r0 + MOE_SKIP, :])
            xb = jnp.concatenate([lo.astype(BF16), hi.astype(BF16)], axis=1)
            gu = jnp.dot(xb, wgu_bf[...], preferred_element_type=F32) + bgu_ref[0, 0]
            gt = jnp.minimum(gu[:, :d_e], SWIGLU_LIMIT)
            up = jnp.clip(gu[:, d_e:], -SWIGLU_LIMIT, SWIGLU_LIMIT)
            a = (up + 1.0) * (gt * jax.nn.sigmoid(SWIGLU_ALPHA * gt))
            o_ref[r0:r0 + MOE_SKIP, :] = _pack_rows(
                jnp.dot(a.astype(BF16), wdn_bf[...], preferred_element_type=F32) + bdn_ref[0, 0])

        @pl.when(n_valid <= r0)
        def _():
            o_ref[r0:r0 + MOE_SKIP, :] = jnp.zeros((MOE_SKIP, o_ref.shape[1]), o_ref.dtype)


def _expert_call(xg, blk_e, n_valid, w_gu, b_gu, w_dn, b_dn, layer):
    rows, half = xg.shape
    d = 2 * half
    nb = rows // MOE_ROWS
    _, n_e, _, two_de = w_gu.shape
    d_e = two_de // 2
    grid_spec = pltpu.PrefetchScalarGridSpec(
        num_scalar_prefetch=2,
        grid=(nb,),
        in_specs=[
            pl.BlockSpec((MOE_ROWS, half), lambda i, be, nu: (i, 0)),
            pl.BlockSpec((1, 1, d, two_de), lambda i, be, nu: (layer, be[i], 0, 0)),
            pl.BlockSpec((1, 1, 1, two_de), lambda i, be, nu: (layer, be[i], 0, 0)),
            pl.BlockSpec((1, 1, d_e, d), lambda i, be, nu: (layer, be[i], 0, 0)),
            pl.BlockSpec((1, 1, 1, d), lambda i, be, nu: (layer, be[i], 0, 0)),
        ],
        out_specs=pl.BlockSpec((MOE_ROWS, half), lambda i, be, nu: (i, 0)),
        scratch_shapes=[pltpu.VMEM((d, two_de), BF16), pltpu.VMEM((d_e, d), BF16)],
    )
    return pl.pallas_call(
        _expert_kernel,
        grid_spec=grid_spec,
        out_shape=jax.ShapeDtypeStruct((rows, half), jnp.int32),
        compiler_params=_params("arbitrary"),
        name="moe_experts",
    )(blk_e, n_valid, xg, w_gu, b_gu.reshape(b_gu.shape[0], n_e, 1, two_de),
      w_dn, b_dn.reshape(b_dn.shape[0], n_e, 1, d))


def _combine_kernel(x_ref, y_ref, gcol_ref, mod_ref, o_ref):
    gc = gcol_ref[...]
    half = y_ref.shape[2]
    acc_lo = acc_hi = None
    for k in range(TOP_K):
        lo, hi = _unpack_rows(y_ref[k])
        gate = gc[:, k:k + 1]
        acc_lo = gate * lo if k == 0 else acc_lo + gate * lo
        acc_hi = gate * hi if k == 0 else acc_hi + gate * hi
    o_ref[:, :half] = x_ref[:, :half] + mod_ref[0, 5:6, :half] * acc_lo
    o_ref[:, half:] = x_ref[:, half:] + mod_ref[0, 5:6, half:] * acc_hi


def _combine_call(x, yk, gcol, mod, seg):
    t, d = x.shape
    tm = TM_OUT
    return pl.pallas_call(
        _combine_kernel,
        grid=(t // tm,),
        in_specs=[
            pl.BlockSpec((tm, d), lambda i: (i, 0)),
            pl.BlockSpec((TOP_K, tm, d // 2), lambda i: (0, i, 0)),
            pl.BlockSpec((tm, LANES), lambda i: (i, 0)),
            pl.BlockSpec((1, 6, d), lambda i: ((i * tm) // seg, 0, 0)),
        ],
        out_specs=pl.BlockSpec((tm, d), lambda i: (i, 0)),
        out_shape=jax.ShapeDtypeStruct((t, d), F32),
        compiler_params=_params("parallel"),
        name="moe_combine",
    )(x, yk, gcol, mod)


def _moe_layer(x, mod, g, w_r, b_r, w_gu, b_gu, w_dn, b_dn, layer, seg):
    t, d = x.shape
    h, ids, _, rank, gcol, cnt = _router_call(x, mod, g, w_r, b_r, seg)
    counts = cnt[:, 0].astype(jnp.int32)
    padded = (counts + MOE_ROWS - 1) // MOE_ROWS * MOE_ROWS
    pad_end = jnp.cumsum(padded)
    pad_start = pad_end - padded
    e_ar = jnp.arange(N_EXPERTS, dtype=jnp.int32)
    dest = rank + jnp.sum(jnp.where(ids[..., None] == e_ar, pad_start, 0), axis=-1)
    nb = -(-(t * TOP_K) // MOE_ROWS) + N_EXPERTS
    rows = nb * MOE_ROWS
    slack = padded - counts
    slack_end = jnp.cumsum(slack)
    r = jnp.arange(rows - t * TOP_K, dtype=jnp.int32)
    owner = jnp.sum((slack_end[None, :] <= r[:, None]).astype(jnp.int32), axis=1)
    first = pad_start + counts - (slack_end - slack)
    in_expert = r + jnp.sum(jnp.where(owner[:, None] == e_ar, first, 0), axis=-1)
    pad_idx = jnp.where(owner < N_EXPERTS, in_expert, pad_end[-1] + r - slack_end[-1])
    blk_row = jnp.arange(nb, dtype=jnp.int32) * MOE_ROWS
    blk_e = jnp.minimum(jnp.sum((pad_end[None, :] <= blk_row[:, None]).astype(jnp.int32), axis=1),
                        N_EXPERTS - 1)
    tok_end = jnp.sum(jnp.where(blk_e[:, None] == e_ar, pad_start + counts, 0), axis=-1)
    n_valid = jnp.where(blk_row < pad_end[-1], jnp.clip(tok_end - blk_row, 0, MOE_ROWS), 0)
    xg = _dispatch_rows(h, dest, pad_idx, rows)
    yb = _expert_call(xg, blk_e, n_valid, w_gu, b_gu, w_dn, b_dn, layer)
    yk = _gather_rows(yb, dest.reshape(-1)).reshape(TOP_K, t, d // 2)
    return _combine_call(x, yk, gcol, mod, seg)


def kernel(x_prompt, x_sample, c_prompt, c_sample, g_mix, g_ffn, w_ada, b_ada, gm_w_in, gm_g_v, gm_w_s, gm_b_s, gm_w_out, sc_w_in, sc_w_conv, sc_w_out, mla_w_down, mla_g_q_lat, mla_g_kv_lat, mla_w_uq, mla_w_ukv, mla_g_qh, mla_g_kh, mla_w_o, moe_w_router, moe_b_router, moe_w_gu, moe_b_gu, moe_w_dn, moe_b_dn):
    bp, sp, d = x_prompt.shape
    bs, ss, _ = x_sample.shape
    depth = g_mix.shape[0]

    n_seq = bp + bs
    c_all = jnp.concatenate([c_prompt, c_sample], axis=0)
    c_pad = jnp.pad(c_all, ((0, (-n_seq) % 8), (0, 0)))
    mod_all = _ada_call(c_pad, w_ada, b_ada)[:, :n_seq].reshape(depth, n_seq, 6, d)

    gm = [(gm_w_in[j].astype(BF16), gm_g_v[j], gm_w_s[j].astype(BF16), gm_b_s[j],
           gm_w_out[j].astype(BF16)) for j in range(gm_w_in.shape[0])]
    sc = [(sc_w_in[j].astype(BF16), sc_w_conv[j], sc_w_out[j].astype(BF16))
          for j in range(sc_w_in.shape[0])]
    mla = [_mla_prepare_weights(mla_w_down[j], mla_w_uq[j], mla_w_ukv[j], mla_g_qh[j],
                                mla_g_kh[j]) + (mla_w_o[j].astype(BF16),)
           for j in range(mla_w_down.shape[0])]
    cos, sin = _rope_tables(max(sp, ss))

    def layer(i, x, mod, n_s, s_len):
        t = x.shape[0]
        kind, j = i % 3, i // 3
        if kind == 0:
            x = _gmlp_call(x, mod, g_mix[i], *gm[j], s_len)
        elif kind == 1:
            pos = (jnp.arange(t // TM_CONV, dtype=jnp.int32) * TM_CONV) % s_len
            w_in, w_conv, w_out = sc[j]
            x = _conv_call(x, mod, g_mix[i], w_in, w_conv, w_out, s_len,
                           (pos != 0).astype(jnp.int32),
                           (pos + TM_CONV != s_len).astype(jnp.int32))
        else:
            wd, uq, ukv, gh, w_o = mla[j]
            pos = (jnp.arange(t // TM_MLA, dtype=jnp.int32) * TM_MLA) % s_len
            q, k, vt = _mla_pre_call(x, mod, g_mix[i], wd, mla_g_q_lat[j], mla_g_kv_lat[j],
                                     uq, ukv, gh, cos, sin, s_len, pos // TM_MLA)
            o = _attn_call(q, k, vt, 0, n_s, s_len)
            x = _proj_res_call(x, o, mod, w_o, s_len)
        return _moe_layer(x, mod, g_ffn[i], moe_w_router[i], moe_b_router[i],
                          moe_w_gu, moe_b_gu, moe_w_dn, moe_b_dn, i, s_len)

    xp = x_prompt.reshape(bp * sp, d)
    xs = x_sample.reshape(bs * ss, d)
    for i in range(depth):
        xp = layer(i, xp, mod_all[i, :bp], bp, sp)
        xs = layer(i, xs, mod_all[i, bp:], bs, ss)
    return (xp.reshape(bp, sp, d), xs.reshape(bs, ss, d))
```

```python
import functools

import jax
import jax.numpy as jnp
from jax import lax
from jax.experimental import pallas as pl
from jax.experimental.pallas import tpu as pltpu
from jax.experimental.pallas import tpu_sc as plsc

F32 = jnp.float32
BF16 = jnp.bfloat16

NORM_EPS = 1e-6
GM_GROUPS = 8
GM_CHUNK = 128
MLA_HEADS = 8
MLA_Q_RANK = 384
MLA_KV_RANK = 256
MLA_NOPE = 128
MLA_ROPE = 64
MLA_V = 128
MLA_DK = MLA_NOPE + MLA_ROPE
ROPE_THETA = 10000.0
N_EXPERTS = 32
TOP_K = 4
SWIGLU_LIMIT = 7.0
SWIGLU_ALPHA = 1.702
LOG2_E = 1.4426950408889634

LANES = 128
HALO = 16
VMEM_LIMIT = 56 * 1024 * 1024

TM_GMLP = 256
TM_CONV = 512
TM_MLA = 256
TM_ROUTER = 512
TM_OUT = 512
TQ_ATTN = 256
ATTN_SUBTILES = 2
TK_ATTN = 2048
MOE_ROWS = 512
COMBINE_CHUNKS = 4


def _params(*sem):
    return pltpu.CompilerParams(dimension_semantics=sem, vmem_limit_bytes=VMEM_LIMIT)


def _const_spec(shape):
    nd = len(shape)
    return pl.BlockSpec(shape, lambda *_: (0,) * nd)


def _norm_mod(x, g, shift, scale):
    ms = jnp.mean(x * x, axis=-1, keepdims=True)
    return (x * lax.rsqrt(ms + NORM_EPS)) * g * (1.0 + scale) + shift


def _pack_rows(x):
    half = x.shape[1] // 2
    lo = pltpu.bitcast(x[:, :half].astype(BF16).astype(F32), jnp.uint32)
    hi = pltpu.bitcast(x[:, half:].astype(BF16).astype(F32), jnp.uint32)
    word = lax.shift_right_logical(lo, jnp.uint32(16)) | (hi & jnp.uint32(0xFFFF0000))
    return pltpu.bitcast(word, jnp.int32)


def _unpack_rows(w):
    u = pltpu.bitcast(w, jnp.uint32)
    lo = pltpu.bitcast(lax.shift_left(u, jnp.uint32(16)), F32)
    hi = pltpu.bitcast(u & jnp.uint32(0xFFFF0000), F32)
    return lo, hi


def _ada_kernel(c_ref, w_ref, b_ref, o_ref):
    c = c_ref[...]
    a = (c * jax.nn.sigmoid(c)).astype(BF16)
    w = w_ref[0].astype(BF16)
    o_ref[0] = jnp.dot(a, w, preferred_element_type=F32) + b_ref[0]


def _ada_call(c_pad, w_ada, b_ada):
    depth, d, n = w_ada.shape
    tn = 1536
    rows = c_pad.shape[0]
    return pl.pallas_call(
        _ada_kernel,
        grid=(depth, n // tn),
        in_specs=[
            pl.BlockSpec((rows, d), lambda l, j: (0, 0)),
            pl.BlockSpec((1, d, tn), lambda l, j: (l, 0, j)),
            pl.BlockSpec((1, 1, tn), lambda l, j: (l, 0, j)),
        ],
        out_specs=pl.BlockSpec((1, rows, tn), lambda l, j: (l, 0, j)),
        out_shape=jax.ShapeDtypeStruct((depth, rows, n), F32),
        compiler_params=_params("arbitrary", "arbitrary"),
        name="adaln_mod",
    )(c_pad, w_ada, b_ada.reshape(depth, 1, n))


def _gmlp_kernel(x_ref, mod_ref, g_ref, win_ref, gv_ref, ws_ref, bs_ref, wout_ref,
                 o_ref, gated_ref):
    x = x_ref[...]
    tm = x.shape[0]
    width = gv_ref.shape[1]
    gw = width // GM_GROUPS
    h = _norm_mod(x, g_ref[...], mod_ref[0, 0:1, :], mod_ref[0, 1:2, :]).astype(BF16)
    uv = jax.nn.gelu(jnp.dot(h, win_ref[...], preferred_element_type=F32))
    u = uv[:, :width]
    v = uv[:, width:]
    v = v * lax.rsqrt(jnp.mean(v * v, axis=-1, keepdims=True) + NORM_EPS) * gv_ref[...]
    vb = v.astype(BF16)
    bs = bs_ref[...]
    for c in range(tm // GM_CHUNK):
        r0 = c * GM_CHUNK
        for g in range(GM_GROUPS):
            c0 = g * gw
            vm = jnp.dot(ws_ref[g], vb[r0:r0 + GM_CHUNK, c0:c0 + gw],
                         preferred_element_type=F32) + bs[:, g:g + 1]
            gated_ref[r0:r0 + GM_CHUNK, c0:c0 + gw] = (
                u[r0:r0 + GM_CHUNK, c0:c0 + gw] * vm).astype(BF16)
    out = jnp.dot(gated_ref[...], wout_ref[...], preferred_element_type=F32)
    o_ref[...] = x + mod_ref[0, 2:3, :] * out


def _gmlp_call(x, mod, g, w_in, g_v, w_s, b_s, w_out, seg):
    t, d = x.shape
    tm = TM_GMLP
    width = g_v.shape[-1]
    return pl.pallas_call(
        _gmlp_kernel,
        grid=(t // tm,),
        in_specs=[
            pl.BlockSpec((tm, d), lambda i: (i, 0)),
            pl.BlockSpec((1, 6, d), lambda i: ((i * tm) // seg, 0, 0)),
            _const_spec((1, d)),
            _const_spec(w_in.shape),
            _const_spec((1, width)),
            _const_spec(w_s.shape),
            _const_spec((GM_CHUNK, GM_GROUPS)),
            _const_spec(w_out.shape),
        ],
        out_specs=pl.BlockSpec((tm, d), lambda i: (i, 0)),
        out_shape=jax.ShapeDtypeStruct((t, d), F32),
        scratch_shapes=[pltpu.VMEM((tm, width), BF16)],
        compiler_params=_params("parallel"),
        name="gmlp_mixer",
    )(x, mod, g.reshape(1, d), w_in.astype(BF16), g_v.reshape(1, width),
      w_s.astype(BF16), b_s.T, w_out.astype(BF16))


def _conv_kernel(start_ref, end_ref, x_ref, xp_ref, xn_ref, mod_ref, g_ref, win_ref,
                 wc_ref, wout_ref, o_ref, h_ref, z_ref):
    i = pl.program_id(0)
    x = x_ref[...]
    tm, d = x.shape
    g = g_ref[...]
    sh = mod_ref[0, 0:1, :]
    sc = mod_ref[0, 1:2, :]
    h_ref[0:HALO, :] = _norm_mod(xp_ref[...], g, sh, sc).astype(BF16)
    h_ref[HALO:HALO + tm, :] = _norm_mod(x, g, sh, sc).astype(BF16)
    h_ref[HALO + tm:, :] = _norm_mod(xn_ref[...], g, sh, sc).astype(BF16)
    bcx = jnp.dot(h_ref[...], win_ref[...], preferred_element_type=F32)
    w = bcx.shape[1] // 3
    bg = bcx[HALO:HALO + tm, :w]
    z_ref[...] = bcx[:, w:2 * w] * bcx[:, 2 * w:]
    keep_prev = (start_ref[i] == 0).astype(F32)
    keep_next = (end_ref[i] == 0).astype(F32)
    z_ref[0:HALO, :] = z_ref[0:HALO, :] * keep_prev
    z_ref[HALO + tm:, :] = z_ref[HALO + tm:, :] * keep_next
    y = (wc_ref[0:1, :] * z_ref[HALO - 1:HALO - 1 + tm, :]
         + wc_ref[1:2, :] * z_ref[HALO:HALO + tm, :]
         + wc_ref[2:3, :] * z_ref[HALO + 1:HALO + 1 + tm, :])
    out = jnp.dot((bg * y).astype(BF16), wout_ref[...], preferred_element_type=F32)
    o_ref[...] = x + mod_ref[0, 2:3, :] * out


def _conv_call(x, mod, g, w_in, w_conv, w_out, seg, tile_start, tile_end):
    t, d = x.shape
    tm = TM_CONV
    w = w_conv.shape[-1]
    hb = tm // HALO
    last = t // HALO - 1
    grid_spec = pltpu.PrefetchScalarGridSpec(
        num_scalar_prefetch=2,
        grid=(t // tm,),
        in_specs=[
            pl.BlockSpec((tm, d), lambda i, *_: (i, 0)),
            pl.BlockSpec((HALO, d), lambda i, *_: (jnp.maximum(i * hb - 1, 0), 0)),
            pl.BlockSpec((HALO, d), lambda i, *_: (jnp.minimum((i + 1) * hb, last), 0)),
            pl.BlockSpec((1, 6, d), lambda i, *_: ((i * tm) // seg, 0, 0)),
            _const_spec((1, d)),
            _const_spec(w_in.shape),
            _const_spec(w_conv.shape),
            _const_spec(w_out.shape),
        ],
        out_specs=pl.BlockSpec((tm, d), lambda i, *_: (i, 0)),
        scratch_shapes=[pltpu.VMEM((tm + 2 * HALO, d), BF16),
                        pltpu.VMEM((tm + 2 * HALO, w), F32)],
    )
    return pl.pallas_call(
        _conv_kernel,
        grid_spec=grid_spec,
        out_shape=jax.ShapeDtypeStruct((t, d), F32),
        compiler_params=_params("parallel"),
        name="shortconv_mixer",
    )(tile_start, tile_end, x, x, x, mod, g.reshape(1, d), w_in.astype(BF16),
      w_conv, w_out.astype(BF16))


QH = 3 * LANES
KH = 2 * LANES


def _mla_pre_kernel(pos_ref, x_ref, mod_ref, g_ref, wd_ref, gq_ref, gkv_ref, wuq_ref,
                    wukv_ref, gh_ref, cos_ref, sin_ref, q_ref, k_ref, v_ref):
    del pos_ref
    x = x_ref[...]
    h = _norm_mod(x, g_ref[...], mod_ref[0, 0:1, :], mod_ref[0, 1:2, :]).astype(BF16)
    lat = jnp.dot(h, wd_ref[...], preferred_element_type=F32)
    ql = lat[:, :MLA_Q_RANK]
    kvl = lat[:, MLA_Q_RANK:MLA_Q_RANK + MLA_KV_RANK]
    pe = lat[:, MLA_Q_RANK + MLA_KV_RANK:MLA_Q_RANK + MLA_KV_RANK + LANES]
    pe_sw = lat[:, MLA_Q_RANK + MLA_KV_RANK + LANES:]
    ql = ql * lax.rsqrt(jnp.mean(ql * ql, axis=-1, keepdims=True) + NORM_EPS) * gq_ref[...]
    kvl = kvl * lax.rsqrt(jnp.mean(kvl * kvl, axis=-1, keepdims=True) + NORM_EPS) * gkv_ref[...]
    q = jnp.dot(ql.astype(BF16), wuq_ref[...], preferred_element_type=F32)
    kv = jnp.dot(kvl.astype(BF16), wukv_ref[...], preferred_element_type=F32)
    cos = cos_ref[...]
    sin = sin_ref[...]
    gqn, gqr, gqs = gh_ref[0:1, :], gh_ref[1:2, :], gh_ref[2:3, :]
    gkn, gkr, gks = gh_ref[3:4, :], gh_ref[4:5, :], gh_ref[5:6, :]
    pe_ss = jnp.sum(pe * pe, axis=-1, keepdims=True)
    qscale = MLA_DK ** -0.5 * LOG2_E
    for hd in range(MLA_HEADS):
        qn = q[:, hd * QH:hd * QH + LANES]
        qr = q[:, hd * QH + LANES:hd * QH + 2 * LANES]
        qs = q[:, hd * QH + 2 * LANES:(hd + 1) * QH]
        ss = jnp.sum(qn * qn, axis=-1, keepdims=True) + jnp.sum(qr * qr, axis=-1, keepdims=True)
        rinv = lax.rsqrt(ss * (1.0 / MLA_DK) + NORM_EPS) * qscale
        q_ref[hd, :, 0:LANES] = (qn * rinv * gqn).astype(BF16)
        q_ref[hd, :, LANES:] = ((qr * rinv * gqr) * cos + (qs * rinv * gqs) * sin).astype(BF16)
        kn = kv[:, hd * 2 * LANES:hd * 2 * LANES + LANES]
        ss = jnp.sum(kn * kn, axis=-1, keepdims=True) + pe_ss
        rinv = lax.rsqrt(ss * (1.0 / MLA_DK) + NORM_EPS)
        k_ref[hd, :, 0:LANES] = (kn * rinv * gkn).astype(BF16)
        k_ref[hd, :, LANES:] = ((pe * rinv * gkr) * cos + (pe_sw * rinv * gks) * sin).astype(BF16)
        v_ref[hd] = kv[:, hd * 2 * LANES + LANES:(hd + 1) * 2 * LANES].T.astype(BF16)


def _mla_prepare_weights(w_down, w_uq, w_ukv, g_qh, g_kh):
    half = MLA_ROPE // 2
    pad = LANES - MLA_ROPE

    def swap(a):
        return jnp.concatenate([a[..., half:], a[..., :half]], axis=-1)

    def pad_lanes(a):
        return jnp.pad(a, [(0, 0)] * (a.ndim - 1) + [(0, pad)])

    d = w_down.shape[0]
    pe_w = w_down[:, MLA_Q_RANK + MLA_KV_RANK:]
    wd = jnp.concatenate([w_down[:, :MLA_Q_RANK + MLA_KV_RANK],
                          pad_lanes(pe_w), pad_lanes(swap(pe_w))], axis=1)
    uq = w_uq.reshape(MLA_Q_RANK, MLA_HEADS, MLA_DK)
    uq_r = uq[..., MLA_NOPE:]
    uq = jnp.concatenate([uq[..., :MLA_NOPE], pad_lanes(uq_r), pad_lanes(swap(uq_r))], axis=-1)
    uq = uq.reshape(MLA_Q_RANK, MLA_HEADS * QH)

    def gains(gv):
        r = gv[MLA_NOPE:]
        return [gv[:MLA_NOPE], pad_lanes(r), pad_lanes(swap(r))]

    gh = jnp.stack(gains(g_qh) + gains(g_kh) + [jnp.zeros((LANES,), F32)] * 2)
    return wd.astype(BF16), uq.astype(BF16), w_ukv.astype(BF16), gh


def _rope_tables(s):
    half = MLA_ROPE // 2
    inv = 1.0 / (ROPE_THETA ** (jnp.arange(0, MLA_ROPE, 2, dtype=F32) / MLA_ROPE))
    ang = jnp.arange(s, dtype=F32)[:, None] * inv[None, :]
    cos, sin = jnp.cos(ang), jnp.sin(ang)
    zeros = jnp.zeros((s, LANES - MLA_ROPE), F32)
    del half
    return (jnp.concatenate([cos, cos, zeros], axis=1),
            jnp.concatenate([-sin, sin, zeros], axis=1))


def _mla_pre_call(x, mod, g, wd, g_q_lat, g_kv_lat, uq, ukv, gh, cos, sin, seg, tile_pos):
    t, d = x.shape
    tm = TM_MLA
    grid_spec = pltpu.PrefetchScalarGridSpec(
        num_scalar_prefetch=1,
        grid=(t // tm,),
        in_specs=[
            pl.BlockSpec((tm, d), lambda i, p: (i, 0)),
            pl.BlockSpec((1, 6, d), lambda i, p: ((i * tm) // seg, 0, 0)),
            _const_spec((1, d)),
            _const_spec(wd.shape),
            _const_spec((1, MLA_Q_RANK)),
            _const_spec((1, MLA_KV_RANK)),
            _const_spec(uq.shape),
            _const_spec(ukv.shape),
            _const_spec(gh.shape),
            pl.BlockSpec((tm, LANES), lambda i, p: (p[i], 0)),
            pl.BlockSpec((tm, LANES), lambda i, p: (p[i], 0)),
        ],
        out_specs=[
            pl.BlockSpec((MLA_HEADS, tm, KH), lambda i, p: (0, i, 0)),
            pl.BlockSpec((MLA_HEADS, tm, KH), lambda i, p: (0, i, 0)),
            pl.BlockSpec((MLA_HEADS, MLA_V, tm), lambda i, p: (0, 0, i)),
        ],
    )
    return pl.pallas_call(
        _mla_pre_kernel,
        grid_spec=grid_spec,
        out_shape=[jax.ShapeDtypeStruct((MLA_HEADS, t, KH), BF16),
                   jax.ShapeDtypeStruct((MLA_HEADS, t, KH), BF16),
                   jax.ShapeDtypeStruct((MLA_HEADS, MLA_V, t), BF16)],
        compiler_params=_params("parallel"),
        name="mla_qkv",
    )(tile_pos, x, mod, g.reshape(1, d), wd, g_q_lat.reshape(1, -1), g_kv_lat.reshape(1, -1),
      uq, ukv, gh, cos, sin)


def _attn_kernel(q_ref, k_ref, vt_ref, o_ref, s_buf, p_buf):
    tk = TK_ATTN
    n = k_ref.shape[1] // tk
    nt = (((1,), (1,)), ((), ()))
    tq = q_ref.shape[1] // ATTN_SUBTILES
    subs = range(ATTN_SUBTILES)
    qs = [q_ref[0, sub * tq:(sub + 1) * tq, :] for sub in subs]

    def scores(sub, j):
        return lax.dot_general(k_ref[0, j * tk:(j + 1) * tk, :], qs[sub], nt,
                               preferred_element_type=F32)

    def values(sub, j):
        return jnp.dot(vt_ref[0, :, j * tk:(j + 1) * tk], p_buf[sub, j % 2],
                       preferred_element_type=F32)

    m = [jnp.full((1, tq), -jnp.inf, F32) for _ in subs]
    l = [jnp.zeros((1, tq), F32) for _ in subs]
    acc = [jnp.zeros((MLA_V, tq), F32) for _ in subs]
    alpha = [None for _ in subs]
    for sub in subs:
        s_buf[sub, 0] = scores(sub, 0)
    for j in range(n):
        cur = j % 2
        for sub in subs:
            if j + 1 < n:
                s_buf[sub, 1 - cur] = scores(sub, j + 1)
            if j > 0:
                acc[sub] = alpha[sub] * acc[sub] + values(sub, j - 1)
            s = s_buf[sub, cur]
            m_new = jnp.maximum(m[sub], jnp.max(s, axis=0, keepdims=True))
            alpha[sub] = jnp.exp2(m[sub] - m_new)
            p = jnp.exp2(s - m_new)
            l[sub] = alpha[sub] * l[sub] + jnp.sum(p, axis=0, keepdims=True)
            m[sub] = m_new
            p_buf[sub, cur] = p.astype(BF16)
    for sub in subs:
        a = alpha[sub] * acc[sub] + values(sub, n - 1)
        o_ref[sub * tq:(sub + 1) * tq, :] = (a / l[sub]).T.astype(BF16)


def _attn_call(q, k, vt, tok0, n_seq, s_len):
    tq, tk = TQ_ATTN * ATTN_SUBTILES, TK_ATTN
    assert s_len % tk == 0
    nq = s_len // tq
    qb0 = tok0 // tq
    sb0 = tok0 // s_len
    return pl.pallas_call(
        _attn_kernel,
        grid=(n_seq, MLA_HEADS, nq),
        in_specs=[
            pl.BlockSpec((1, tq, KH), lambda s, h, i: (h, qb0 + s * nq + i, 0)),
            pl.BlockSpec((1, s_len, KH), lambda s, h, i: (h, sb0 + s, 0)),
            pl.BlockSpec((1, MLA_V, s_len), lambda s, h, i: (h, 0, sb0 + s)),
        ],
        out_specs=pl.BlockSpec((tq, MLA_V), lambda s, h, i: (s * nq + i, h)),
        out_shape=jax.ShapeDtypeStruct((n_seq * s_len, MLA_HEADS * MLA_V), BF16),
        scratch_shapes=[pltpu.VMEM((ATTN_SUBTILES, 2, tk, TQ_ATTN), F32),
                        pltpu.VMEM((ATTN_SUBTILES, 2, tk, TQ_ATTN), BF16)],
        compiler_params=_params("parallel", "parallel", "arbitrary"),
        name="mla_attention",
    )(q, k, vt)


def _proj_res_kernel(x_ref, o_ref, mod_ref, w_ref, out_ref):
    out = jnp.dot(o_ref[...], w_ref[...], preferred_element_type=F32)
    out_ref[...] = x_ref[...] + mod_ref[0, 2:3, :] * out


def _proj_res_call(x, o, mod, w_o, seg):
    t, d = x.shape
    tm = TM_OUT
    return pl.pallas_call(
        _proj_res_kernel,
        grid=(t // tm,),
        in_specs=[
            pl.BlockSpec((tm, d), lambda i: (i, 0)),
            pl.BlockSpec((tm, o.shape[1]), lambda i: (i, 0)),
            pl.BlockSpec((1, 6, d), lambda i: ((i * tm) // seg, 0, 0)),
            _const_spec(w_o.shape),
        ],
        out_specs=pl.BlockSpec((tm, d), lambda i: (i, 0)),
        out_shape=jax.ShapeDtypeStruct((t, d), F32),
        compiler_params=_params("parallel"),
        name="mla_out_proj",
    )(x, o, mod, w_o.astype(BF16))


def _router_kernel(x_ref, mod_ref, g_ref, wr_hi_ref, wr_lo_ref, br_ref,
                   h_ref, ids_ref, gates_ref, rank_ref, gcol_ref, cnt_ref, carry_ref):
    i = pl.program_id(0)

    @pl.when(i == 0)
    def _():
        carry_ref[...] = jnp.zeros_like(carry_ref)

    x = x_ref[...]
    tm = x.shape[0]
    h = _norm_mod(x, g_ref[...], mod_ref[0, 3:4, :], mod_ref[0, 4:5, :])
    h_ref[...] = _pack_rows(h)
    h_hi = h.astype(BF16)
    h_lo = (h - h_hi.astype(F32)).astype(BF16)
    nt = (((1,), (1,)), ((), ()))
    logits = (lax.dot_general(wr_hi_ref[...], h_hi, nt, preferred_element_type=F32)
              + lax.dot_general(wr_hi_ref[...], h_lo, nt, preferred_element_type=F32)
              + lax.dot_general(wr_lo_ref[...], h_hi, nt, preferred_element_type=F32)
              + br_ref[...])
    eidx = lax.broadcasted_iota(jnp.int32, logits.shape, 0)
    lg = logits
    sel = jnp.zeros(logits.shape, F32)
    vals, ids = [], []
    for _ in range(TOP_K):
        m = jnp.max(lg, axis=0, keepdims=True)
        idx = jnp.min(jnp.where(lg == m, eidx, N_EXPERTS), axis=0, keepdims=True)
        onehot = eidx == idx
        vals.append(m)
        ids.append(idx)
        lg = jnp.where(onehot, -jnp.inf, lg)
        sel = jnp.where(onehot, 1.0, sel)
    ex = [jnp.exp(v - vals[0]) for v in vals]
    den = ex[0] + ex[1] + ex[2] + ex[3]
    gates = [e / den for e in ex]
    r_io = lax.broadcasted_iota(jnp.int32, (tm, tm), 0)
    c_io = lax.broadcasted_iota(jnp.int32, (tm, tm), 1)
    before = jnp.where(r_io < c_io, 1.0, 0.0).astype(BF16)
    cum = jnp.dot(sel.astype(BF16), before, preferred_element_type=F32) + carry_ref[:, 0:1]
    for k in range(TOP_K):
        rk = jnp.sum(jnp.where(eidx == ids[k], cum, 0.0), axis=0, keepdims=True)
        rank_ref[k:k + 1, :] = rk.astype(jnp.int32)
        ids_ref[k:k + 1, :] = ids[k]
        gates_ref[k:k + 1, :] = gates[k]
    g8 = jnp.concatenate(gates + [jnp.zeros((LANES - TOP_K, tm), F32)], axis=0)
    gcol_ref[...] = g8.T
    carry_ref[...] = carry_ref[...] + jnp.sum(sel, axis=1, keepdims=True)
    cnt_ref[...] = carry_ref[...]


def _router_call(x, mod, g, w_r, b_r, seg):
    t, d = x.shape
    tm = TM_ROUTER
    e = w_r.shape[1]
    wr_hi = w_r.T.astype(BF16)
    wr_lo = (w_r.T - wr_hi.astype(F32)).astype(BF16)
    return pl.pallas_call(
        _router_kernel,
        grid=(t // tm,),
        in_specs=[
            pl.BlockSpec((tm, d), lambda i: (i, 0)),
            pl.BlockSpec((1, 6, d), lambda i: ((i * tm) // seg, 0, 0)),
            _const_spec((1, d)),
            _const_spec((e, d)),
            _const_spec((e, d)),
            _const_spec((e, 1)),
        ],
        out_specs=[
            pl.BlockSpec((tm, d // 2), lambda i: (i, 0)),
            pl.BlockSpec((TOP_K, tm), lambda i: (0, i)),
            pl.BlockSpec((TOP_K, tm), lambda i: (0, i)),
            pl.BlockSpec((TOP_K, tm), lambda i: (0, i)),
            pl.BlockSpec((tm, LANES), lambda i: (i, 0)),
            _const_spec((e, LANES)),
        ],
        out_shape=[
            jax.ShapeDtypeStruct((t, d // 2), jnp.int32),
            jax.ShapeDtypeStruct((TOP_K, t), jnp.int32),
            jax.ShapeDtypeStruct((TOP_K, t), F32),
            jax.ShapeDtypeStruct((TOP_K, t), jnp.int32),
            jax.ShapeDtypeStruct((t, LANES), F32),
            jax.ShapeDtypeStruct((e, LANES), F32),
        ],
        scratch_shapes=[pltpu.VMEM((e, LANES), F32)],
        compiler_params=_params("arbitrary"),
        name="moe_router",
    )(x, mod, g.reshape(1, d), wr_hi, wr_lo, b_r.reshape(e, 1))


SC_CORES = 2
SC_SUBCORES = 16
SC_WORKERS = SC_CORES * SC_SUBCORES
SC_CHUNK = 64


def _dispatch_rows(h, dest, pad_idx, rows):
    t, d = h.shape
    top_k = dest.shape[0]
    per_w = t // SC_WORKERS
    n_ch = per_w // SC_CHUNK
    pad_ch = pad_idx.shape[0] // (SC_WORKERS * SC_CHUNK)
    assert n_ch * SC_CHUNK * SC_WORKERS == t
    assert pad_ch * SC_CHUNK * SC_WORKERS == pad_idx.shape[0]
    assert top_k * t + pad_idx.shape[0] == rows
    mesh = plsc.VectorSubcoreMesh(core_axis_name="c", subcore_axis_name="s")

    @functools.partial(
        pl.kernel, mesh=mesh,
        out_type=jax.ShapeDtypeStruct((rows, d), h.dtype),
        scratch_types=[pltpu.VMEM((top_k * n_ch, SC_CHUNK), jnp.int32),
                       pltpu.VMEM((pad_ch, SC_CHUNK), jnp.int32),
                       pltpu.VMEM((SC_CHUNK, d), h.dtype),
                       pltpu.SemaphoreType.DMA],
        name="sc_row_dispatch",
    )
    def dispatch(h_hbm, dest_hbm, pad_hbm, zero_hbm, out_hbm, idx_v, pad_v, rows_v, sem):
        wid = lax.axis_index("s") * SC_CORES + lax.axis_index("c")
        base = wid * per_w
        pltpu.sync_copy(dest_hbm.at[wid], idx_v)
        pltpu.sync_copy(pad_hbm.at[wid], pad_v)

        @pl.loop(0, n_ch)
        def _(j):
            pltpu.sync_copy(h_hbm.at[pl.ds(base + j * SC_CHUNK, SC_CHUNK)], rows_v)
            copies = [pltpu.async_copy(rows_v, out_hbm.at[idx_v.at[k * n_ch + j]], sem)
                      for k in range(top_k)]
            for c in copies:
                c.wait()

        pltpu.sync_copy(zero_hbm, rows_v)

        @pl.loop(0, pad_ch)
        def _(p):
            pltpu.sync_copy(rows_v, out_hbm.at[pad_v.at[p]])

    dest_w = dest.reshape(top_k, SC_WORKERS, n_ch, SC_CHUNK).transpose(1, 0, 2, 3)
    return dispatch(h, dest_w.reshape(SC_WORKERS, top_k * n_ch, SC_CHUNK),
                    pad_idx.reshape(SC_WORKERS, pad_ch, SC_CHUNK),
                    jnp.zeros((SC_CHUNK, d), h.dtype))


def _gather_rows(src, idx):
    n = idx.shape[0]
    d = src.shape[1]
    per_w = n // SC_WORKERS
    n_ch = per_w // SC_CHUNK
    assert per_w * SC_WORKERS == n and n_ch * SC_CHUNK == per_w
    mesh = plsc.VectorSubcoreMesh(core_axis_name="c", subcore_axis_name="s")

    @functools.partial(
        pl.kernel, mesh=mesh,
        out_type=jax.ShapeDtypeStruct((n, d), src.dtype),
        scratch_types=[pltpu.VMEM((n_ch, SC_CHUNK), jnp.int32),
                       pltpu.VMEM((SC_CHUNK, d), src.dtype),
                       pltpu.SemaphoreType.DMA],
        name="sc_row_gather",
    )
    def gather(src_hbm, idx_hbm, out_hbm, idx_v, rows_v, sem):
        wid = lax.axis_index("s") * SC_CORES + lax.axis_index("c")
        base = wid * per_w
        pltpu.sync_copy(idx_hbm.at[wid], idx_v)

        @pl.loop(0, n_ch)
        def _(j):
            pltpu.async_copy(src_hbm.at[idx_v.at[j]], rows_v, sem).wait()
            pltpu.sync_copy(rows_v, out_hbm.at[pl.ds(base + j * SC_CHUNK, SC_CHUNK)])

    return gather(src, idx.reshape(SC_WORKERS, n_ch, SC_CHUNK))


def _expert_kernel(blk_e_ref, nvalid_ref, x_ref, wgu_ref, bgu_ref, wdn_ref, bdn_ref,
                   o_ref, wgu_bf, wdn_bf):
    i = pl.program_id(0)
    e = blk_e_ref[i]
    e_prev = blk_e_ref[jnp.maximum(i - 1, 0)]
    active = nvalid_ref[i] > 0
    d_e = wdn_ref.shape[2]

    @pl.when(active & ((i == 0) | (e != e_prev)))
    def _():
        step = 128
        for r in range(0, wgu_ref.shape[2], step):
            wgu_bf[r:r + step, :] = wgu_ref[0, 0, r:r + step, :].astype(BF16)
        for r in range(0, d_e, step):
            wdn_bf[r:r + step, :] = wdn_ref[0, 0, r:r + step, :].astype(BF16)

    @pl.when(active)
    def _():
        lo, hi = _unpack_rows(x_ref[...])
        xb = jnp.concatenate([lo.astype(BF16), hi.astype(BF16)], axis=1)
        gu = jnp.dot(xb, wgu_bf[...], preferred_element_type=F32) + bgu_ref[0, 0]
        gt = jnp.minimum(gu[:, :d_e], SWIGLU_LIMIT)
        up = jnp.clip(gu[:, d_e:], -SWIGLU_LIMIT, SWIGLU_LIMIT)
        a = (up + 1.0) * (gt * jax.nn.sigmoid(SWIGLU_ALPHA * gt))
        o_ref[...] = _pack_rows(jnp.dot(a.astype(BF16), wdn_bf[...],
                                        preferred_element_type=F32) + bdn_ref[0, 0])

    @pl.when(jnp.logical_not(active))
    def _():
        o_ref[...] = jnp.zeros_like(o_ref)


def _expert_call(xg, blk_e, n_valid, w_gu, b_gu, w_dn, b_dn, layer):
    rows, half = xg.shape
    d = 2 * half
    nb = rows // MOE_ROWS
    _, n_e, _, two_de = w_gu.shape
    d_e = two_de // 2
    grid_spec = pltpu.PrefetchScalarGridSpec(
        num_scalar_prefetch=2,
        grid=(nb,),
        in_specs=[
            pl.BlockSpec((MOE_ROWS, half), lambda i, be, nu: (i, 0)),
            pl.BlockSpec((1, 1, d, two_de), lambda i, be, nu: (layer, be[i], 0, 0)),
            pl.BlockSpec((1, 1, 1, two_de), lambda i, be, nu: (layer, be[i], 0, 0)),
            pl.BlockSpec((1, 1, d_e, d), lambda i, be, nu: (layer, be[i], 0, 0)),
            pl.BlockSpec((1, 1, 1, d), lambda i, be, nu: (layer, be[i], 0, 0)),
        ],
        out_specs=pl.BlockSpec((MOE_ROWS, half), lambda i, be, nu: (i, 0)),
        scratch_shapes=[pltpu.VMEM((d, two_de), BF16), pltpu.VMEM((d_e, d), BF16)],
    )
    return pl.pallas_call(
        _expert_kernel,
        grid_spec=grid_spec,
        out_shape=jax.ShapeDtypeStruct((rows, half), jnp.int32),
        compiler_params=_params("arbitrary"),
        name="moe_experts",
    )(blk_e, n_valid, xg, w_gu, b_gu.reshape(b_gu.shape[0], n_e, 1, two_de),
      w_dn, b_dn.reshape(b_dn.shape[0], n_e, 1, d))


def _combine_kernel(x_ref, y_ref, gcol_ref, mod_ref, o_ref):
    gc = gcol_ref[...]
    half = y_ref.shape[2]
    acc_lo = acc_hi = None
    for k in range(TOP_K):
        lo, hi = _unpack_rows(y_ref[k])
        gate = gc[:, k:k + 1]
        acc_lo = gate * lo if k == 0 else acc_lo + gate * lo
        acc_hi = gate * hi if k == 0 else acc_hi + gate * hi
    o_ref[:, :half] = x_ref[:, :half] + mod_ref[0, 5:6, :half] * acc_lo
    o_ref[:, half:] = x_ref[:, half:] + mod_ref[0, 5:6, half:] * acc_hi


def _combine_call(x, yk, gcol, mod, seg, chunk):
    t, d = x.shape
    tm = TM_OUT
    nb = yk.shape[1] // tm
    b0 = chunk * nb
    return pl.pallas_call(
        _combine_kernel,
        grid=(nb,),
        in_specs=[
            pl.BlockSpec((tm, d), lambda i: (b0 + i, 0)),
            pl.BlockSpec((TOP_K, tm, d // 2), lambda i: (0, i, 0)),
            pl.BlockSpec((tm, LANES), lambda i: (b0 + i, 0)),
            pl.BlockSpec((1, 6, d), lambda i: (((b0 + i) * tm) // seg, 0, 0)),
        ],
        out_specs=pl.BlockSpec((tm, d), lambda i: (b0 + i, 0)),
        out_shape=jax.ShapeDtypeStruct((t, d), F32),
        input_output_aliases={0: 0},
        compiler_params=_params("parallel"),
        name="moe_combine",
    )(x, yk, gcol, mod)


def _moe_layer(x, mod, g, w_r, b_r, w_gu, b_gu, w_dn, b_dn, layer, seg):
    t, d = x.shape
    h, ids, _, rank, gcol, cnt = _router_call(x, mod, g, w_r, b_r, seg)
    counts = cnt[:, 0].astype(jnp.int32)
    padded = (counts + MOE_ROWS - 1) // MOE_ROWS * MOE_ROWS
    pad_end = jnp.cumsum(padded)
    pad_start = pad_end - padded
    e_ar = jnp.arange(N_EXPERTS, dtype=jnp.int32)
    dest = rank + jnp.sum(jnp.where(ids[..., None] == e_ar, pad_start, 0), axis=-1)
    nb = -(-(t * TOP_K) // MOE_ROWS) + N_EXPERTS
    rows = nb * MOE_ROWS
    slack = padded - counts
    slack_end = jnp.cumsum(slack)
    r = jnp.arange(rows - t * TOP_K, dtype=jnp.int32)
    owner = jnp.sum((slack_end[None, :] <= r[:, None]).astype(jnp.int32), axis=1)
    first = pad_start + counts - (slack_end - slack)
    in_expert = r + jnp.sum(jnp.where(owner[:, None] == e_ar, first, 0), axis=-1)
    pad_idx = jnp.where(owner < N_EXPERTS, in_expert, pad_end[-1] + r - slack_end[-1])
    blk_row = jnp.arange(nb, dtype=jnp.int32) * MOE_ROWS
    blk_e = jnp.minimum(jnp.sum((pad_end[None, :] <= blk_row[:, None]).astype(jnp.int32), axis=1),
                        N_EXPERTS - 1)
    tok_end = jnp.sum(jnp.where(blk_e[:, None] == e_ar, pad_start + counts, 0), axis=-1)
    n_valid = jnp.where(blk_row < pad_end[-1], jnp.clip(tok_end - blk_row, 0, MOE_ROWS), 0)
    xg = _dispatch_rows(h, dest, pad_idx, rows)
    yb = _expert_call(xg, blk_e, n_valid, w_gu, b_gu, w_dn, b_dn, layer)
    tc = t // COMBINE_CHUNKS
    for c in range(COMBINE_CHUNKS):
        idx = dest[:, c * tc:(c + 1) * tc].reshape(-1)
        yk = _gather_rows(yb, idx).reshape(TOP_K, tc, d // 2)
        x = _combine_call(x, yk, gcol, mod, seg, c)
    return x


def kernel(x_prompt, x_sample, c_prompt, c_sample, g_mix, g_ffn, w_ada, b_ada, gm_w_in, gm_g_v, gm_w_s, gm_b_s, gm_w_out, sc_w_in, sc_w_conv, sc_w_out, mla_w_down, mla_g_q_lat, mla_g_kv_lat, mla_w_uq, mla_w_ukv, mla_g_qh, mla_g_kh, mla_w_o, moe_w_router, moe_b_router, moe_w_gu, moe_b_gu, moe_w_dn, moe_b_dn):
    bp, sp, d = x_prompt.shape
    bs, ss, _ = x_sample.shape
    tp, ts = bp * sp, bs * ss
    t = tp + ts
    depth = g_mix.shape[0]
    seg = min(sp, ss)
    assert sp % seg == 0 and ss % seg == 0

    x = jnp.concatenate([x_prompt.reshape(tp, d), x_sample.reshape(ts, d)], axis=0)

    n_seq = bp + bs
    c_all = jnp.concatenate([c_prompt, c_sample], axis=0)
    c_pad = jnp.pad(c_all, ((0, (-n_seq) % 8), (0, 0)))
    mod_all = _ada_call(c_pad, w_ada, b_ada)[:, :n_seq].reshape(depth, n_seq, 6, d)
    seg_seq = jnp.concatenate([jnp.repeat(jnp.arange(bp), sp // seg),
                               bp + jnp.repeat(jnp.arange(bs), ss // seg)])
    mod_seg = mod_all[:, seg_seq]

    def tile_meta(tm):
        t0 = jnp.arange(t // tm, dtype=jnp.int32) * tm
        pos = jnp.where(t0 < tp, t0 % sp, (t0 - tp) % ss)
        slen = jnp.where(t0 < tp, sp, ss)
        return pos, slen

    for i in range(depth):
        mod = mod_seg[i]
        kind, j = i % 3, i // 3
        if kind == 0:
            x = _gmlp_call(x, mod, g_mix[i], gm_w_in[j], gm_g_v[j], gm_w_s[j], gm_b_s[j],
                           gm_w_out[j], seg)
        elif kind == 1:
            pos, slen = tile_meta(TM_CONV)
            x = _conv_call(x, mod, g_mix[i], sc_w_in[j], sc_w_conv[j], sc_w_out[j], seg,
                           (pos != 0).astype(jnp.int32),
                           (pos + TM_CONV != slen).astype(jnp.int32))
        else:
            wd, uq, ukv, gh = _mla_prepare_weights(mla_w_down[j], mla_w_uq[j], mla_w_ukv[j],
                                                  mla_g_qh[j], mla_g_kh[j])
            cos, sin = _rope_tables(max(sp, ss))
            pos, _ = tile_meta(TM_MLA)
            q, k, vt = _mla_pre_call(x, mod, g_mix[i], wd, mla_g_q_lat[j], mla_g_kv_lat[j],
                                     uq, ukv, gh, cos, sin, seg, pos // TM_MLA)
            o = jnp.concatenate([_attn_call(q, k, vt, 0, bp, sp),
                                 _attn_call(q, k, vt, tp, bs, ss)], axis=0)
            x = _proj_res_call(x, o, mod, mla_w_o[j], seg)
        x = _moe_layer(x, mod, g_ffn[i], moe_w_router[i], moe_b_router[i],
                       moe_w_gu, moe_b_gu, moe_w_dn, moe_b_dn, i, seg)

    return (x[:tp].reshape(bp, sp, d), x[tp:].reshape(bs, ss, d))
```

```python
import functools

import jax
import jax.numpy as jnp
from jax import lax
from jax.experimental import pallas as pl
from jax.experimental.pallas import tpu as pltpu
from jax.experimental.pallas import tpu_sc as plsc

F32 = jnp.float32
BF16 = jnp.bfloat16

NORM_EPS = 1e-6
GM_GROUPS = 8
GM_CHUNK = 128
MLA_HEADS = 8
MLA_Q_RANK = 384
MLA_KV_RANK = 256
MLA_NOPE = 128
MLA_ROPE = 64
MLA_V = 128
MLA_DK = MLA_NOPE + MLA_ROPE
ROPE_THETA = 10000.0
N_EXPERTS = 32
TOP_K = 4
SWIGLU_LIMIT = 7.0
SWIGLU_ALPHA = 1.702
LOG2_E = 1.4426950408889634

LANES = 128
HALO = 16
VMEM_LIMIT = 56 * 1024 * 1024

TM_GMLP = 512
TM_CONV = 512
TM_MLA = 256
TM_ROUTER = 512
TM_OUT = 512
TQ_ATTN = 256
ATTN_SUBTILES = 2
TK_ATTN = 2048
MOE_ROWS = 512
COMBINE_CHUNKS = 4


def _params(*sem):
    return pltpu.CompilerParams(dimension_semantics=sem, vmem_limit_bytes=VMEM_LIMIT)


def _const_spec(shape):
    nd = len(shape)
    return pl.BlockSpec(shape, lambda *_: (0,) * nd)


def _norm_mod(x, g, shift, scale):
    ms = jnp.mean(x * x, axis=-1, keepdims=True)
    return (x * lax.rsqrt(ms + NORM_EPS)) * g * (1.0 + scale) + shift


def _pack_rows(x):
    half = x.shape[1] // 2
    lo = pltpu.bitcast(x[:, :half].astype(BF16).astype(F32), jnp.uint32)
    hi = pltpu.bitcast(x[:, half:].astype(BF16).astype(F32), jnp.uint32)
    word = lax.shift_right_logical(lo, jnp.uint32(16)) | (hi & jnp.uint32(0xFFFF0000))
    return pltpu.bitcast(word, jnp.int32)


def _unpack_rows(w):
    u = pltpu.bitcast(w, jnp.uint32)
    lo = pltpu.bitcast(lax.shift_left(u, jnp.uint32(16)), F32)
    hi = pltpu.bitcast(u & jnp.uint32(0xFFFF0000), F32)
    return lo, hi


def _ada_kernel(c_ref, w_ref, b_ref, o_ref):
    c = c_ref[...]
    a = (c * jax.nn.sigmoid(c)).astype(BF16)
    w = w_ref[0].astype(BF16)
    o_ref[0] = jnp.dot(a, w, preferred_element_type=F32) + b_ref[0]


def _ada_call(c_pad, w_ada, b_ada):
    depth, d, n = w_ada.shape
    tn = 1536
    rows = c_pad.shape[0]
    return pl.pallas_call(
        _ada_kernel,
        grid=(depth, n // tn),
        in_specs=[
            pl.BlockSpec((rows, d), lambda l, j: (0, 0)),
            pl.BlockSpec((1, d, tn), lambda l, j: (l, 0, j)),
            pl.BlockSpec((1, 1, tn), lambda l, j: (l, 0, j)),
        ],
        out_specs=pl.BlockSpec((1, rows, tn), lambda l, j: (l, 0, j)),
        out_shape=jax.ShapeDtypeStruct((depth, rows, n), F32),
        compiler_params=_params("arbitrary", "arbitrary"),
        name="adaln_mod",
    )(c_pad, w_ada, b_ada.reshape(depth, 1, n))


def _gmlp_kernel(*refs, n_first):
    if n_first is None:
        x = refs[0][...]
        refs = refs[1:]
    else:
        x = jnp.where(pl.program_id(0) < n_first, refs[0][...], refs[1][...])
        refs = refs[2:]
    mod_ref, g_ref, win_ref, gv_ref, ws_ref, bs_ref, wout_ref, o_ref, gated_ref = refs
    tm = x.shape[0]
    width = gv_ref.shape[1]
    gw = width // GM_GROUPS
    h = _norm_mod(x, g_ref[...], mod_ref[0, 0:1, :], mod_ref[0, 1:2, :]).astype(BF16)
    uv = jax.nn.gelu(jnp.dot(h, win_ref[...], preferred_element_type=F32))
    u = uv[:, :width]
    v = uv[:, width:]
    v = v * lax.rsqrt(jnp.mean(v * v, axis=-1, keepdims=True) + NORM_EPS) * gv_ref[...]
    vb = v.astype(BF16)
    bs = bs_ref[...]
    for c in range(tm // GM_CHUNK):
        r0 = c * GM_CHUNK
        for g in range(GM_GROUPS):
            c0 = g * gw
            vm = jnp.dot(ws_ref[g], vb[r0:r0 + GM_CHUNK, c0:c0 + gw],
                         preferred_element_type=F32) + bs[:, g:g + 1]
            gated_ref[r0:r0 + GM_CHUNK, c0:c0 + gw] = (
                u[r0:r0 + GM_CHUNK, c0:c0 + gw] * vm).astype(BF16)
    out = jnp.dot(gated_ref[...], wout_ref[...], preferred_element_type=F32)
    o_ref[...] = x + mod_ref[0, 2:3, :] * out


def _gmlp_call(xs, mod, g, w_in, g_v, w_s, b_s, w_out, seg):
    tm = TM_GMLP
    if isinstance(xs, tuple):
        xa, xb = xs
        na, nb = xa.shape[0] // tm, xb.shape[0] // tm
        t, d = xa.shape[0] + xb.shape[0], xa.shape[1]
        n_first = na
        x_specs = [pl.BlockSpec((tm, d), lambda i: (jnp.minimum(i, na - 1), 0)),
                   pl.BlockSpec((tm, d), lambda i: (jnp.clip(i - na, 0, nb - 1), 0))]
    else:
        xs = (xs,)
        t, d = xs[0].shape
        n_first = None
        x_specs = [pl.BlockSpec((tm, d), lambda i: (i, 0))]
    width = g_v.shape[-1]
    return pl.pallas_call(
        functools.partial(_gmlp_kernel, n_first=n_first),
        grid=(t // tm,),
        in_specs=x_specs + [
            pl.BlockSpec((1, 6, d), lambda i: ((i * tm) // seg, 0, 0)),
            _const_spec((1, d)),
            _const_spec(w_in.shape),
            _const_spec((1, width)),
            _const_spec(w_s.shape),
            _const_spec((GM_CHUNK, GM_GROUPS)),
            _const_spec(w_out.shape),
        ],
        out_specs=pl.BlockSpec((tm, d), lambda i: (i, 0)),
        out_shape=jax.ShapeDtypeStruct((t, d), F32),
        scratch_shapes=[pltpu.VMEM((tm, width), BF16)],
        compiler_params=_params("parallel"),
        name="gmlp_mixer",
    )(*xs, mod, g.reshape(1, d), w_in.astype(BF16), g_v.reshape(1, width),
      w_s.astype(BF16), b_s.T, w_out.astype(BF16))


def _conv_kernel(start_ref, end_ref, x_ref, xp_ref, xn_ref, mod_ref, g_ref, win_ref,
                 wc_ref, wout_ref, o_ref, h_ref, z_ref):
    i = pl.program_id(0)
    x = x_ref[...]
    tm, d = x.shape
    g = g_ref[...]
    sh = mod_ref[0, 0:1, :]
    sc = mod_ref[0, 1:2, :]
    h_ref[0:HALO, :] = _norm_mod(xp_ref[...], g, sh, sc).astype(BF16)
    h_ref[HALO:HALO + tm, :] = _norm_mod(x, g, sh, sc).astype(BF16)
    h_ref[HALO + tm:, :] = _norm_mod(xn_ref[...], g, sh, sc).astype(BF16)
    bcx = jnp.dot(h_ref[...], win_ref[...], preferred_element_type=F32)
    w = bcx.shape[1] // 3
    bg = bcx[HALO:HALO + tm, :w]
    z_ref[...] = bcx[:, w:2 * w] * bcx[:, 2 * w:]
    keep_prev = (start_ref[i] == 0).astype(F32)
    keep_next = (end_ref[i] == 0).astype(F32)
    z_ref[0:HALO, :] = z_ref[0:HALO, :] * keep_prev
    z_ref[HALO + tm:, :] = z_ref[HALO + tm:, :] * keep_next
    y = (wc_ref[0:1, :] * z_ref[HALO - 1:HALO - 1 + tm, :]
         + wc_ref[1:2, :] * z_ref[HALO:HALO + tm, :]
         + wc_ref[2:3, :] * z_ref[HALO + 1:HALO + 1 + tm, :])
    out = jnp.dot((bg * y).astype(BF16), wout_ref[...], preferred_element_type=F32)
    o_ref[...] = x + mod_ref[0, 2:3, :] * out


def _conv_call(x, mod, g, w_in, w_conv, w_out, seg, tile_start, tile_end):
    t, d = x.shape
    tm = TM_CONV
    w = w_conv.shape[-1]
    hb = tm // HALO
    last = t // HALO - 1
    grid_spec = pltpu.PrefetchScalarGridSpec(
        num_scalar_prefetch=2,
        grid=(t // tm,),
        in_specs=[
            pl.BlockSpec((tm, d), lambda i, *_: (i, 0)),
            pl.BlockSpec((HALO, d), lambda i, *_: (jnp.maximum(i * hb - 1, 0), 0)),
            pl.BlockSpec((HALO, d), lambda i, *_: (jnp.minimum((i + 1) * hb, last), 0)),
            pl.BlockSpec((1, 6, d), lambda i, *_: ((i * tm) // seg, 0, 0)),
            _const_spec((1, d)),
            _const_spec(w_in.shape),
            _const_spec(w_conv.shape),
            _const_spec(w_out.shape),
        ],
        out_specs=pl.BlockSpec((tm, d), lambda i, *_: (i, 0)),
        scratch_shapes=[pltpu.VMEM((tm + 2 * HALO, d), BF16),
                        pltpu.VMEM((tm + 2 * HALO, w), F32)],
    )
    return pl.pallas_call(
        _conv_kernel,
        grid_spec=grid_spec,
        out_shape=jax.ShapeDtypeStruct((t, d), F32),
        compiler_params=_params("parallel"),
        name="shortconv_mixer",
    )(tile_start, tile_end, x, x, x, mod, g.reshape(1, d), w_in.astype(BF16),
      w_conv, w_out.astype(BF16))


QH = 3 * LANES
KH = 2 * LANES


def _mla_pre_kernel(pos_ref, x_ref, mod_ref, g_ref, wd_ref, gq_ref, gkv_ref, wuq_ref,
                    wukv_ref, gh_ref, cos_ref, sin_ref, q_ref, k_ref, v_ref):
    del pos_ref
    x = x_ref[...]
    h = _norm_mod(x, g_ref[...], mod_ref[0, 0:1, :], mod_ref[0, 1:2, :]).astype(BF16)
    lat = jnp.dot(h, wd_ref[...], preferred_element_type=F32)
    ql = lat[:, :MLA_Q_RANK]
    kvl = lat[:, MLA_Q_RANK:MLA_Q_RANK + MLA_KV_RANK]
    pe = lat[:, MLA_Q_RANK + MLA_KV_RANK:MLA_Q_RANK + MLA_KV_RANK + LANES]
    pe_sw = lat[:, MLA_Q_RANK + MLA_KV_RANK + LANES:]
    ql = ql * lax.rsqrt(jnp.mean(ql * ql, axis=-1, keepdims=True) + NORM_EPS) * gq_ref[...]
    kvl = kvl * lax.rsqrt(jnp.mean(kvl * kvl, axis=-1, keepdims=True) + NORM_EPS) * gkv_ref[...]
    q = jnp.dot(ql.astype(BF16), wuq_ref[...], preferred_element_type=F32)
    kv = jnp.dot(kvl.astype(BF16), wukv_ref[...], preferred_element_type=F32)
    cos = cos_ref[...]
    sin = sin_ref[...]
    gqn, gqr, gqs = gh_ref[0:1, :], gh_ref[1:2, :], gh_ref[2:3, :]
    gkn, gkr, gks = gh_ref[3:4, :], gh_ref[4:5, :], gh_ref[5:6, :]
    pe_ss = jnp.sum(pe * pe, axis=-1, keepdims=True)
    qscale = MLA_DK ** -0.5 * LOG2_E
    for hd in range(MLA_HEADS):
        qn = q[:, hd * QH:hd * QH + LANES]
        qr = q[:, hd * QH + LANES:hd * QH + 2 * LANES]
        qs = q[:, hd * QH + 2 * LANES:(hd + 1) * QH]
        ss = jnp.sum(qn * qn, axis=-1, keepdims=True) + jnp.sum(qr * qr, axis=-1, keepdims=True)
        rinv = lax.rsqrt(ss * (1.0 / MLA_DK) + NORM_EPS) * qscale
        q_ref[hd, :, 0:LANES] = (qn * rinv * gqn).astype(BF16)
        q_ref[hd, :, LANES:] = ((qr * rinv * gqr) * cos + (qs * rinv * gqs) * sin).astype(BF16)
        kn = kv[:, hd * 2 * LANES:hd * 2 * LANES + LANES]
        ss = jnp.sum(kn * kn, axis=-1, keepdims=True) + pe_ss
        rinv = lax.rsqrt(ss * (1.0 / MLA_DK) + NORM_EPS)
        k_ref[hd, :, 0:LANES] = (kn * rinv * gkn).astype(BF16)
        k_ref[hd, :, LANES:] = ((pe * rinv * gkr) * cos + (pe_sw * rinv * gks) * sin).astype(BF16)
        v_ref[hd] = kv[:, hd * 2 * LANES + LANES:(hd + 1) * 2 * LANES].T.astype(BF16)


def _mla_prepare_weights(w_down, w_uq, w_ukv, g_qh, g_kh):
    half = MLA_ROPE // 2
    pad = LANES - MLA_ROPE

    def swap(a):
        return jnp.concatenate([a[..., half:], a[..., :half]], axis=-1)

    def pad_lanes(a):
        return jnp.pad(a, [(0, 0)] * (a.ndim - 1) + [(0, pad)])

    d = w_down.shape[0]
    pe_w = w_down[:, MLA_Q_RANK + MLA_KV_RANK:]
    wd = jnp.concatenate([w_down[:, :MLA_Q_RANK + MLA_KV_RANK],
                          pad_lanes(pe_w), pad_lanes(swap(pe_w))], axis=1)
    uq = w_uq.reshape(MLA_Q_RANK, MLA_HEADS, MLA_DK)
    uq_r = uq[..., MLA_NOPE:]
    uq = jnp.concatenate([uq[..., :MLA_NOPE], pad_lanes(uq_r), pad_lanes(swap(uq_r))], axis=-1)
    uq = uq.reshape(MLA_Q_RANK, MLA_HEADS * QH)

    def gains(gv):
        r = gv[MLA_NOPE:]
        return [gv[:MLA_NOPE], pad_lanes(r), pad_lanes(swap(r))]

    gh = jnp.stack(gains(g_qh) + gains(g_kh) + [jnp.zeros((LANES,), F32)] * 2)
    return wd.astype(BF16), uq.astype(BF16), w_ukv.astype(BF16), gh


def _rope_tables(s):
    half = MLA_ROPE // 2
    inv = 1.0 / (ROPE_THETA ** (jnp.arange(0, MLA_ROPE, 2, dtype=F32) / MLA_ROPE))
    ang = jnp.arange(s, dtype=F32)[:, None] * inv[None, :]
    cos, sin = jnp.cos(ang), jnp.sin(ang)
    zeros = jnp.zeros((s, LANES - MLA_ROPE), F32)
    del half
    return (jnp.concatenate([cos, cos, zeros], axis=1),
            jnp.concatenate([-sin, sin, zeros], axis=1))


def _mla_pre_call(x, mod, g, wd, g_q_lat, g_kv_lat, uq, ukv, gh, cos, sin, seg, tile_pos):
    t, d = x.shape
    tm = TM_MLA
    grid_spec = pltpu.PrefetchScalarGridSpec(
        num_scalar_prefetch=1,
        grid=(t // tm,),
        in_specs=[
            pl.BlockSpec((tm, d), lambda i, p: (i, 0)),
            pl.BlockSpec((1, 6, d), lambda i, p: ((i * tm) // seg, 0, 0)),
            _const_spec((1, d)),
            _const_spec(wd.shape),
            _const_spec((1, MLA_Q_RANK)),
            _const_spec((1, MLA_KV_RANK)),
            _const_spec(uq.shape),
            _const_spec(ukv.shape),
            _const_spec(gh.shape),
            pl.BlockSpec((tm, LANES), lambda i, p: (p[i], 0)),
            pl.BlockSpec((tm, LANES), lambda i, p: (p[i], 0)),
        ],
        out_specs=[
            pl.BlockSpec((MLA_HEADS, tm, KH), lambda i, p: (0, i, 0)),
            pl.BlockSpec((MLA_HEADS, tm, KH), lambda i, p: (0, i, 0)),
            pl.BlockSpec((MLA_HEADS, MLA_V, tm), lambda i, p: (0, 0, i)),
        ],
    )
    return pl.pallas_call(
        _mla_pre_kernel,
        grid_spec=grid_spec,
        out_shape=[jax.ShapeDtypeStruct((MLA_HEADS, t, KH), BF16),
                   jax.ShapeDtypeStruct((MLA_HEADS, t, KH), BF16),
                   jax.ShapeDtypeStruct((MLA_HEADS, MLA_V, t), BF16)],
        compiler_params=_params("parallel"),
        name="mla_qkv",
    )(tile_pos, x, mod, g.reshape(1, d), wd, g_q_lat.reshape(1, -1), g_kv_lat.reshape(1, -1),
      uq, ukv, gh, cos, sin)


def _attn_kernel(q_ref, k_ref, vt_ref, o_ref, s_buf, p_buf):
    tk = TK_ATTN
    n = k_ref.shape[1] // tk
    nt = (((1,), (1,)), ((), ()))
    tq = q_ref.shape[1] // ATTN_SUBTILES
    subs = range(ATTN_SUBTILES)
    qs = [q_ref[0, sub * tq:(sub + 1) * tq, :] for sub in subs]

    def scores(sub, j):
        return lax.dot_general(k_ref[0, j * tk:(j + 1) * tk, :], qs[sub], nt,
                               preferred_element_type=F32)

    def values(sub, j):
        return jnp.dot(vt_ref[0, :, j * tk:(j + 1) * tk], p_buf[sub, j % 2],
                       preferred_element_type=F32)

    m = [jnp.full((1, tq), -jnp.inf, F32) for _ in subs]
    l = [jnp.zeros((1, tq), F32) for _ in subs]
    acc = [jnp.zeros((MLA_V, tq), F32) for _ in subs]
    alpha = [None for _ in subs]
    for sub in subs:
        s_buf[sub, 0] = scores(sub, 0)
    for j in range(n):
        cur = j % 2
        for sub in subs:
            if j + 1 < n:
                s_buf[sub, 1 - cur] = scores(sub, j + 1)
            if j > 0:
                acc[sub] = alpha[sub] * acc[sub] + values(sub, j - 1)
            s = s_buf[sub, cur]
            m_new = jnp.maximum(m[sub], jnp.max(s, axis=0, keepdims=True))
            alpha[sub] = jnp.exp2(m[sub] - m_new)
            p = jnp.exp2(s - m_new)
            l[sub] = alpha[sub] * l[sub] + jnp.sum(p, axis=0, keepdims=True)
            m[sub] = m_new
            p_buf[sub, cur] = p.astype(BF16)
    for sub in subs:
        a = alpha[sub] * acc[sub] + values(sub, n - 1)
        o_ref[sub * tq:(sub + 1) * tq, :] = (a / l[sub]).T.astype(BF16)


def _attn_call(q, k, vt, tok0, n_seq, s_len):
    tq, tk = TQ_ATTN * ATTN_SUBTILES, TK_ATTN
    assert s_len % tk == 0
    nq = s_len // tq
    qb0 = tok0 // tq
    sb0 = tok0 // s_len
    return pl.pallas_call(
        _attn_kernel,
        grid=(n_seq, MLA_HEADS, nq),
        in_specs=[
            pl.BlockSpec((1, tq, KH), lambda s, h, i: (h, qb0 + s * nq + i, 0)),
            pl.BlockSpec((1, s_len, KH), lambda s, h, i: (h, sb0 + s, 0)),
            pl.BlockSpec((1, MLA_V, s_len), lambda s, h, i: (h, 0, sb0 + s)),
        ],
        out_specs=pl.BlockSpec((tq, MLA_V), lambda s, h, i: (s * nq + i, h)),
        out_shape=jax.ShapeDtypeStruct((n_seq * s_len, MLA_HEADS * MLA_V), BF16),
        scratch_shapes=[pltpu.VMEM((ATTN_SUBTILES, 2, tk, TQ_ATTN), F32),
                        pltpu.VMEM((ATTN_SUBTILES, 2, tk, TQ_ATTN), BF16)],
        compiler_params=_params("parallel", "parallel", "arbitrary"),
        name="mla_attention",
    )(q, k, vt)


def _proj_res_kernel(x_ref, o_ref, mod_ref, w_ref, out_ref):
    out = jnp.dot(o_ref[...], w_ref[...], preferred_element_type=F32)
    out_ref[...] = x_ref[...] + mod_ref[0, 2:3, :] * out


def _proj_res_call(x, o, mod, w_o, seg):
    t, d = x.shape
    tm = TM_OUT
    return pl.pallas_call(
        _proj_res_kernel,
        grid=(t // tm,),
        in_specs=[
            pl.BlockSpec((tm, d), lambda i: (i, 0)),
            pl.BlockSpec((tm, o.shape[1]), lambda i: (i, 0)),
            pl.BlockSpec((1, 6, d), lambda i: ((i * tm) // seg, 0, 0)),
            _const_spec(w_o.shape),
        ],
        out_specs=pl.BlockSpec((tm, d), lambda i: (i, 0)),
        out_shape=jax.ShapeDtypeStruct((t, d), F32),
        compiler_params=_params("parallel"),
        name="mla_out_proj",
    )(x, o, mod, w_o.astype(BF16))


def _router_kernel(x_ref, mod_ref, g_ref, wr_hi_ref, wr_lo_ref, br_ref,
                   h_ref, ids_ref, gates_ref, rank_ref, gcol_ref, cnt_ref, carry_ref):
    i = pl.program_id(0)

    @pl.when(i == 0)
    def _():
        carry_ref[...] = jnp.zeros_like(carry_ref)

    x = x_ref[...]
    tm = x.shape[0]
    h = _norm_mod(x, g_ref[...], mod_ref[0, 3:4, :], mod_ref[0, 4:5, :])
    h_ref[...] = _pack_rows(h)
    h_hi = h.astype(BF16)
    h_lo = (h - h_hi.astype(F32)).astype(BF16)
    nt = (((1,), (1,)), ((), ()))
    logits = (lax.dot_general(wr_hi_ref[...], h_hi, nt, preferred_element_type=F32)
              + lax.dot_general(wr_hi_ref[...], h_lo, nt, preferred_element_type=F32)
              + lax.dot_general(wr_lo_ref[...], h_hi, nt, preferred_element_type=F32)
              + br_ref[...])
    eidx = lax.broadcasted_iota(jnp.int32, logits.shape, 0)
    lg = logits
    sel = jnp.zeros(logits.shape, F32)
    vals, ids = [], []
    for _ in range(TOP_K):
        m = jnp.max(lg, axis=0, keepdims=True)
        idx = jnp.min(jnp.where(lg == m, eidx, N_EXPERTS), axis=0, keepdims=True)
        onehot = eidx == idx
        vals.append(m)
        ids.append(idx)
        lg = jnp.where(onehot, -jnp.inf, lg)
        sel = jnp.where(onehot, 1.0, sel)
    ex = [jnp.exp(v - vals[0]) for v in vals]
    den = ex[0] + ex[1] + ex[2] + ex[3]
    gates = [e / den for e in ex]
    r_io = lax.broadcasted_iota(jnp.int32, (tm, tm), 0)
    c_io = lax.broadcasted_iota(jnp.int32, (tm, tm), 1)
    before = jnp.where(r_io < c_io, 1.0, 0.0).astype(BF16)
    cum = jnp.dot(sel.astype(BF16), before, preferred_element_type=F32) + carry_ref[:, 0:1]
    for k in range(TOP_K):
        rk = jnp.sum(jnp.where(eidx == ids[k], cum, 0.0), axis=0, keepdims=True)
        rank_ref[k:k + 1, :] = rk.astype(jnp.int32)
        ids_ref[k:k + 1, :] = ids[k]
        gates_ref[k:k + 1, :] = gates[k]
    g8 = jnp.concatenate(gates + [jnp.zeros((LANES - TOP_K, tm), F32)], axis=0)
    gcol_ref[...] = g8.T
    carry_ref[...] = carry_ref[...] + jnp.sum(sel, axis=1, keepdims=True)
    cnt_ref[...] = carry_ref[...]


def _router_call(x, mod, g, w_r, b_r, seg):
    t, d = x.shape
    tm = TM_ROUTER
    e = w_r.shape[1]
    wr_hi = w_r.T.astype(BF16)
    wr_lo = (w_r.T - wr_hi.astype(F32)).astype(BF16)
    return pl.pallas_call(
        _router_kernel,
        grid=(t // tm,),
        in_specs=[
            pl.BlockSpec((tm, d), lambda i: (i, 0)),
            pl.BlockSpec((1, 6, d), lambda i: ((i * tm) // seg, 0, 0)),
            _const_spec((1, d)),
            _const_spec((e, d)),
            _const_spec((e, d)),
            _const_spec((e, 1)),
        ],
        out_specs=[
            pl.BlockSpec((tm, d // 2), lambda i: (i, 0)),
            pl.BlockSpec((TOP_K, tm), lambda i: (0, i)),
            pl.BlockSpec((TOP_K, tm), lambda i: (0, i)),
            pl.BlockSpec((TOP_K, tm), lambda i: (0, i)),
            pl.BlockSpec((tm, LANES), lambda i: (i, 0)),
            _const_spec((e, LANES)),
        ],
        out_shape=[
            jax.ShapeDtypeStruct((t, d // 2), jnp.int32),
            jax.ShapeDtypeStruct((TOP_K, t), jnp.int32),
            jax.ShapeDtypeStruct((TOP_K, t), F32),
            jax.ShapeDtypeStruct((TOP_K, t), jnp.int32),
            jax.ShapeDtypeStruct((t, LANES), F32),
            jax.ShapeDtypeStruct((e, LANES), F32),
        ],
        scratch_shapes=[pltpu.VMEM((e, LANES), F32)],
        compiler_params=_params("arbitrary"),
        name="moe_router",
    )(x, mod, g.reshape(1, d), wr_hi, wr_lo, b_r.reshape(e, 1))


SC_CORES = 2
SC_SUBCORES = 16
SC_WORKERS = SC_CORES * SC_SUBCORES
SC_CHUNK = 64


def _dispatch_rows(h, dest, pad_idx, rows):
    t, d = h.shape
    top_k = dest.shape[0]
    per_w = t // SC_WORKERS
    n_ch = per_w // SC_CHUNK
    pad_ch = pad_idx.shape[0] // (SC_WORKERS * SC_CHUNK)
    assert n_ch * SC_CHUNK * SC_WORKERS == t
    assert pad_ch * SC_CHUNK * SC_WORKERS == pad_idx.shape[0]
    assert top_k * t + pad_idx.shape[0] == rows
    mesh = plsc.VectorSubcoreMesh(core_axis_name="c", subcore_axis_name="s")

    @functools.partial(
        pl.kernel, mesh=mesh,
        out_type=jax.ShapeDtypeStruct((rows, d), h.dtype),
        scratch_types=[pltpu.VMEM((top_k * n_ch, SC_CHUNK), jnp.int32),
                       pltpu.VMEM((pad_ch, SC_CHUNK), jnp.int32),
                       pltpu.VMEM((SC_CHUNK, d), h.dtype),
                       pltpu.SemaphoreType.DMA],
        name="sc_row_dispatch",
    )
    def dispatch(h_hbm, dest_hbm, pad_hbm, zero_hbm, out_hbm, idx_v, pad_v, rows_v, sem):
        wid = lax.axis_index("s") * SC_CORES + lax.axis_index("c")
        base = wid * per_w
        pltpu.sync_copy(dest_hbm.at[wid], idx_v)
        pltpu.sync_copy(pad_hbm.at[wid], pad_v)

        @pl.loop(0, n_ch)
        def _(j):
            pltpu.sync_copy(h_hbm.at[pl.ds(base + j * SC_CHUNK, SC_CHUNK)], rows_v)
            copies = [pltpu.async_copy(rows_v, out_hbm.at[idx_v.at[k * n_ch + j]], sem)
                      for k in range(top_k)]
            for c in copies:
                c.wait()

        pltpu.sync_copy(zero_hbm, rows_v)

        @pl.loop(0, pad_ch)
        def _(p):
            pltpu.sync_copy(rows_v, out_hbm.at[pad_v.at[p]])

    dest_w = dest.reshape(top_k, SC_WORKERS, n_ch, SC_CHUNK).transpose(1, 0, 2, 3)
    return dispatch(h, dest_w.reshape(SC_WORKERS, top_k * n_ch, SC_CHUNK),
                    pad_idx.reshape(SC_WORKERS, pad_ch, SC_CHUNK),
                    jnp.zeros((SC_CHUNK, d), h.dtype))


def _gather_rows(src, idx):
    n = idx.shape[0]
    d = src.shape[1]
    per_w = n // SC_WORKERS
    n_ch = per_w // SC_CHUNK
    assert per_w * SC_WORKERS == n and n_ch * SC_CHUNK == per_w
    mesh = plsc.VectorSubcoreMesh(core_axis_name="c", subcore_axis_name="s")

    @functools.partial(
        pl.kernel, mesh=mesh,
        out_type=jax.ShapeDtypeStruct((n, d), src.dtype),
        scratch_types=[pltpu.VMEM((n_ch, SC_CHUNK), jnp.int32),
                       pltpu.VMEM((SC_CHUNK, d), src.dtype),
                       pltpu.SemaphoreType.DMA],
        name="sc_row_gather",
    )
    def gather(src_hbm, idx_hbm, out_hbm, idx_v, rows_v, sem):
        wid = lax.axis_index("s") * SC_CORES + lax.axis_index("c")
        base = wid * per_w
        pltpu.sync_copy(idx_hbm.at[wid], idx_v)

        @pl.loop(0, n_ch)
        def _(j):
            pltpu.async_copy(src_hbm.at[idx_v.at[j]], rows_v, sem).wait()
            pltpu.sync_copy(rows_v, out_hbm.at[pl.ds(base + j * SC_CHUNK, SC_CHUNK)])

    return gather(src, idx.reshape(SC_WORKERS, n_ch, SC_CHUNK))


def _expert_kernel(blk_e_ref, nvalid_ref, x_ref, wgu_ref, bgu_ref, wdn_ref, bdn_ref,
                   o_ref, wgu_bf, wdn_bf):
    i = pl.program_id(0)
    e = blk_e_ref[i]
    e_prev = blk_e_ref[jnp.maximum(i - 1, 0)]
    active = nvalid_ref[i] > 0
    d_e = wdn_ref.shape[2]

    @pl.when(active & ((i == 0) | (e != e_prev)))
    def _():
        step = 128
        for r in range(0, wgu_ref.shape[2], step):
            wgu_bf[r:r + step, :] = wgu_ref[0, 0, r:r + step, :].astype(BF16)
        for r in range(0, d_e, step):
            wdn_bf[r:r + step, :] = wdn_ref[0, 0, r:r + step, :].astype(BF16)

    @pl.when(active)
    def _():
        lo, hi = _unpack_rows(x_ref[...])
        xb = jnp.concatenate([lo.astype(BF16), hi.astype(BF16)], axis=1)
        gu = jnp.dot(xb, wgu_bf[...], preferred_element_type=F32) + bgu_ref[0, 0]
        gt = jnp.minimum(gu[:, :d_e], SWIGLU_LIMIT)
        up = jnp.clip(gu[:, d_e:], -SWIGLU_LIMIT, SWIGLU_LIMIT)
        a = (up + 1.0) * (gt * jax.nn.sigmoid(SWIGLU_ALPHA * gt))
        o_ref[...] = _pack_rows(jnp.dot(a.astype(BF16), wdn_bf[...],
                                        preferred_element_type=F32) + bdn_ref[0, 0])

    @pl.when(jnp.logical_not(active))
    def _():
        o_ref[...] = jnp.zeros_like(o_ref)


def _expert_call(xg, blk_e, n_valid, w_gu, b_gu, w_dn, b_dn, layer):
    rows, half = xg.shape
    d = 2 * half
    nb = rows // MOE_ROWS
    _, n_e, _, two_de = w_gu.shape
    d_e = two_de // 2
    grid_spec = pltpu.PrefetchScalarGridSpec(
        num_scalar_prefetch=2,
        grid=(nb,),
        in_specs=[
            pl.BlockSpec((MOE_ROWS, half), lambda i, be, nu: (i, 0)),
            pl.BlockSpec((1, 1, d, two_de), lambda i, be, nu: (layer, be[i], 0, 0)),
            pl.BlockSpec((1, 1, 1, two_de), lambda i, be, nu: (layer, be[i], 0, 0)),
            pl.BlockSpec((1, 1, d_e, d), lambda i, be, nu: (layer, be[i], 0, 0)),
            pl.BlockSpec((1, 1, 1, d), lambda i, be, nu: (layer, be[i], 0, 0)),
        ],
        out_specs=pl.BlockSpec((MOE_ROWS, half), lambda i, be, nu: (i, 0)),
        scratch_shapes=[pltpu.VMEM((d, two_de), BF16), pltpu.VMEM((d_e, d), BF16)],
    )
    return pl.pallas_call(
        _expert_kernel,
        grid_spec=grid_spec,
        out_shape=jax.ShapeDtypeStruct((rows, half), jnp.int32),
        compiler_params=_params("arbitrary"),
        name="moe_experts",
    )(blk_e, n_valid, xg, w_gu, b_gu.reshape(b_gu.shape[0], n_e, 1, two_de),
      w_dn, b_dn.reshape(b_dn.shape[0], n_e, 1, d))


def _combine_kernel(x_ref, y_ref, gcol_ref, mod_ref, o_ref):
    gc = gcol_ref[...]
    half = y_ref.shape[2]
    acc_lo = acc_hi = None
    for k in range(TOP_K):
        lo, hi = _unpack_rows(y_ref[k])
        gate = gc[:, k:k + 1]
        acc_lo = gate * lo if k == 0 else acc_lo + gate * lo
        acc_hi = gate * hi if k == 0 else acc_hi + gate * hi
    o_ref[:, :half] = x_ref[:, :half] + mod_ref[0, 5:6, :half] * acc_lo
    o_ref[:, half:] = x_ref[:, half:] + mod_ref[0, 5:6, half:] * acc_hi


def _combine_call(x, yk, gcol, mod, seg, tok0, in_place):
    t, d = x.shape
    tm = TM_OUT
    tc = yk.shape[1]
    nb = tc // tm
    b0 = tok0 // tm
    ob = b0 if in_place else 0
    return pl.pallas_call(
        _combine_kernel,
        grid=(nb,),
        in_specs=[
            pl.BlockSpec((tm, d), lambda i: (b0 + i, 0)),
            pl.BlockSpec((TOP_K, tm, d // 2), lambda i: (0, i, 0)),
            pl.BlockSpec((tm, LANES), lambda i: (b0 + i, 0)),
            pl.BlockSpec((1, 6, d), lambda i: (((b0 + i) * tm) // seg, 0, 0)),
        ],
        out_specs=pl.BlockSpec((tm, d), lambda i: (ob + i, 0)),
        out_shape=jax.ShapeDtypeStruct((t if in_place else tc, d), F32),
        input_output_aliases={0: 0} if in_place else {},
        compiler_params=_params("parallel"),
        name="moe_combine",
    )(x, yk, gcol, mod)


def _moe_layer(x, mod, g, w_r, b_r, w_gu, b_gu, w_dn, b_dn, layer, seg, out_split=None):
    t, d = x.shape
    h, ids, _, rank, gcol, cnt = _router_call(x, mod, g, w_r, b_r, seg)
    counts = cnt[:, 0].astype(jnp.int32)
    padded = (counts + MOE_ROWS - 1) // MOE_ROWS * MOE_ROWS
    pad_end = jnp.cumsum(padded)
    pad_start = pad_end - padded
    e_ar = jnp.arange(N_EXPERTS, dtype=jnp.int32)
    dest = rank + jnp.sum(jnp.where(ids[..., None] == e_ar, pad_start, 0), axis=-1)
    nb = -(-(t * TOP_K) // MOE_ROWS) + N_EXPERTS
    rows = nb * MOE_ROWS
    slack = padded - counts
    slack_end = jnp.cumsum(slack)
    r = jnp.arange(rows - t * TOP_K, dtype=jnp.int32)
    owner = jnp.sum((slack_end[None, :] <= r[:, None]).astype(jnp.int32), axis=1)
    first = pad_start + counts - (slack_end - slack)
    in_expert = r + jnp.sum(jnp.where(owner[:, None] == e_ar, first, 0), axis=-1)
    pad_idx = jnp.where(owner < N_EXPERTS, in_expert, pad_end[-1] + r - slack_end[-1])
    blk_row = jnp.arange(nb, dtype=jnp.int32) * MOE_ROWS
    blk_e = jnp.minimum(jnp.sum((pad_end[None, :] <= blk_row[:, None]).astype(jnp.int32), axis=1),
                        N_EXPERTS - 1)
    tok_end = jnp.sum(jnp.where(blk_e[:, None] == e_ar, pad_start + counts, 0), axis=-1)
    n_valid = jnp.where(blk_row < pad_end[-1], jnp.clip(tok_end - blk_row, 0, MOE_ROWS), 0)
    xg = _dispatch_rows(h, dest, pad_idx, rows)
    yb = _expert_call(xg, blk_e, n_valid, w_gu, b_gu, w_dn, b_dn, layer)
    if out_split is None:
        bounds = [c * (t // COMBINE_CHUNKS) for c in range(COMBINE_CHUNKS + 1)]
    else:
        bounds = [0, out_split, t]
    outs = []
    for lo, hi in zip(bounds[:-1], bounds[1:]):
        yk = _gather_rows(yb, dest[:, lo:hi].reshape(-1)).reshape(TOP_K, hi - lo, d // 2)
        if out_split is None:
            x = _combine_call(x, yk, gcol, mod, seg, lo, True)
        else:
            outs.append(_combine_call(x, yk, gcol, mod, seg, lo, False))
    return x if out_split is None else tuple(outs)


def kernel(x_prompt, x_sample, c_prompt, c_sample, g_mix, g_ffn, w_ada, b_ada, gm_w_in, gm_g_v, gm_w_s, gm_b_s, gm_w_out, sc_w_in, sc_w_conv, sc_w_out, mla_w_down, mla_g_q_lat, mla_g_kv_lat, mla_w_uq, mla_w_ukv, mla_g_qh, mla_g_kh, mla_w_o, moe_w_router, moe_b_router, moe_w_gu, moe_b_gu, moe_w_dn, moe_b_dn):
    bp, sp, d = x_prompt.shape
    bs, ss, _ = x_sample.shape
    tp, ts = bp * sp, bs * ss
    t = tp + ts
    depth = g_mix.shape[0]
    seg = min(sp, ss)
    assert sp % seg == 0 and ss % seg == 0

    x = (x_prompt.reshape(tp, d), x_sample.reshape(ts, d))

    n_seq = bp + bs
    c_all = jnp.concatenate([c_prompt, c_sample], axis=0)
    c_pad = jnp.pad(c_all, ((0, (-n_seq) % 8), (0, 0)))
    mod_all = _ada_call(c_pad, w_ada, b_ada)[:, :n_seq].reshape(depth, n_seq, 6, d)
    seg_seq = jnp.concatenate([jnp.repeat(jnp.arange(bp), sp // seg),
                               bp + jnp.repeat(jnp.arange(bs), ss // seg)])
    mod_seg = mod_all[:, seg_seq]

    def tile_meta(tm):
        t0 = jnp.arange(t // tm, dtype=jnp.int32) * tm
        pos = jnp.where(t0 < tp, t0 % sp, (t0 - tp) % ss)
        slen = jnp.where(t0 < tp, sp, ss)
        return pos, slen

    for i in range(depth):
        mod = mod_seg[i]
        kind, j = i % 3, i // 3
        if kind == 0:
            x = _gmlp_call(x, mod, g_mix[i], gm_w_in[j], gm_g_v[j], gm_w_s[j], gm_b_s[j],
                           gm_w_out[j], seg)
        elif kind == 1:
            pos, slen = tile_meta(TM_CONV)
            x = _conv_call(x, mod, g_mix[i], sc_w_in[j], sc_w_conv[j], sc_w_out[j], seg,
                           (pos != 0).astype(jnp.int32),
                           (pos + TM_CONV != slen).astype(jnp.int32))
        else:
            wd, uq, ukv, gh = _mla_prepare_weights(mla_w_down[j], mla_w_uq[j], mla_w_ukv[j],
                                                  mla_g_qh[j], mla_g_kh[j])
            cos, sin = _rope_tables(max(sp, ss))
            pos, _ = tile_meta(TM_MLA)
            q, k, vt = _mla_pre_call(x, mod, g_mix[i], wd, mla_g_q_lat[j], mla_g_kv_lat[j],
                                     uq, ukv, gh, cos, sin, seg, pos // TM_MLA)
            o = jnp.concatenate([_attn_call(q, k, vt, 0, bp, sp),
                                 _attn_call(q, k, vt, tp, bs, ss)], axis=0)
            x = _proj_res_call(x, o, mod, mla_w_o[j], seg)
        x = _moe_layer(x, mod, g_ffn[i], moe_w_router[i], moe_b_router[i],
                       moe_w_gu, moe_b_gu, moe_w_dn, moe_b_dn, i, seg,
                       out_split=tp if i == depth - 1 else None)

    return (x[0].reshape(bp, sp, d), x[1].reshape(bs, ss, d))
```

```python
import functools

import jax
import jax.numpy as jnp
from jax import lax
from jax.experimental import pallas as pl
from jax.experimental.pallas import tpu as pltpu
from jax.experimental.pallas import tpu_sc as plsc

F32 = jnp.float32
BF16 = jnp.bfloat16

NORM_EPS = 1e-6
GM_GROUPS = 8
GM_CHUNK = 128
MLA_HEADS = 8
MLA_Q_RANK = 384
MLA_KV_RANK = 256
MLA_NOPE = 128
MLA_ROPE = 64
MLA_V = 128
MLA_DK = MLA_NOPE + MLA_ROPE
ROPE_THETA = 10000.0
N_EXPERTS = 32
TOP_K = 4
SWIGLU_LIMIT = 7.0
SWIGLU_ALPHA = 1.702
LOG2_E = 1.4426950408889634

LANES = 128
HALO = 16
VMEM_LIMIT = 56 * 1024 * 1024

TM_GMLP = 512
TM_CONV = 512
TM_MLA = 512
TM_ROUTER = 512
TM_OUT = 512
TQ_ATTN = 256
ATTN_SUBTILES = 2
TK_ATTN = 2048
MOE_ROWS = 512
COMBINE_CHUNKS = 4


def _params(*sem):
    return pltpu.CompilerParams(dimension_semantics=sem, vmem_limit_bytes=VMEM_LIMIT)


def _const_spec(shape):
    nd = len(shape)
    return pl.BlockSpec(shape, lambda *_: (0,) * nd)


def _norm_mod(x, g, shift, scale):
    ms = jnp.mean(x * x, axis=-1, keepdims=True)
    return (x * lax.rsqrt(ms + NORM_EPS)) * g * (1.0 + scale) + shift


def _pack_rows(x):
    return _pack_rounded(x.astype(BF16).astype(F32))


def _pack_rounded(xr):
    half = xr.shape[1] // 2
    lo = pltpu.bitcast(xr[:, :half], jnp.uint32)
    hi = pltpu.bitcast(xr[:, half:], jnp.uint32)
    word = lax.shift_right_logical(lo, jnp.uint32(16)) | (hi & jnp.uint32(0xFFFF0000))
    return pltpu.bitcast(word, jnp.int32)


def _unpack_rows(w):
    u = pltpu.bitcast(w, jnp.uint32)
    lo = pltpu.bitcast(lax.shift_left(u, jnp.uint32(16)), F32)
    hi = pltpu.bitcast(u & jnp.uint32(0xFFFF0000), F32)
    return lo, hi


def _ada_kernel(c_ref, w_ref, b_ref, o_ref):
    c = c_ref[...]
    a = (c * jax.nn.sigmoid(c)).astype(BF16)
    w = w_ref[0].astype(BF16)
    o_ref[0] = jnp.dot(a, w, preferred_element_type=F32) + b_ref[0]


def _ada_call(c_pad, w_ada, b_ada):
    depth, d, n = w_ada.shape
    tn = 1536
    rows = c_pad.shape[0]
    return pl.pallas_call(
        _ada_kernel,
        grid=(depth, n // tn),
        in_specs=[
            pl.BlockSpec((rows, d), lambda l, j: (0, 0)),
            pl.BlockSpec((1, d, tn), lambda l, j: (l, 0, j)),
            pl.BlockSpec((1, 1, tn), lambda l, j: (l, 0, j)),
        ],
        out_specs=pl.BlockSpec((1, rows, tn), lambda l, j: (l, 0, j)),
        out_shape=jax.ShapeDtypeStruct((depth, rows, n), F32),
        compiler_params=_params("arbitrary", "arbitrary"),
        name="adaln_mod",
    )(c_pad, w_ada, b_ada.reshape(depth, 1, n))


def _gmlp_kernel(*refs, n_first):
    if n_first is None:
        x = refs[0][...]
        refs = refs[1:]
    else:
        x = jnp.where(pl.program_id(0) < n_first, refs[0][...], refs[1][...])
        refs = refs[2:]
    mod_ref, g_ref, win_ref, gv_ref, ws_ref, bs_ref, wout_ref, o_ref, gated_ref = refs
    tm = x.shape[0]
    width = gv_ref.shape[1]
    gw = width // GM_GROUPS
    h = _norm_mod(x, g_ref[...], mod_ref[0, 0:1, :], mod_ref[0, 1:2, :]).astype(BF16)
    uv = jax.nn.gelu(jnp.dot(h, win_ref[...], preferred_element_type=F32))
    u = uv[:, :width]
    v = uv[:, width:]
    v = v * lax.rsqrt(jnp.mean(v * v, axis=-1, keepdims=True) + NORM_EPS) * gv_ref[...]
    vb = v.astype(BF16)
    bs = bs_ref[...]
    for c in range(tm // GM_CHUNK):
        r0 = c * GM_CHUNK
        for g in range(GM_GROUPS):
            c0 = g * gw
            vm = jnp.dot(ws_ref[g], vb[r0:r0 + GM_CHUNK, c0:c0 + gw],
                         preferred_element_type=F32) + bs[:, g:g + 1]
            gated_ref[r0:r0 + GM_CHUNK, c0:c0 + gw] = (
                u[r0:r0 + GM_CHUNK, c0:c0 + gw] * vm).astype(BF16)
    out = jnp.dot(gated_ref[...], wout_ref[...], preferred_element_type=F32)
    o_ref[...] = x + mod_ref[0, 2:3, :] * out


def _gmlp_call(xs, mod, g, w_in, g_v, w_s, b_s, w_out, seg):
    tm = TM_GMLP
    if isinstance(xs, tuple):
        xa, xb = xs
        na, nb = xa.shape[0] // tm, xb.shape[0] // tm
        t, d = xa.shape[0] + xb.shape[0], xa.shape[1]
        n_first = na
        x_specs = [pl.BlockSpec((tm, d), lambda i: (jnp.minimum(i, na - 1), 0)),
                   pl.BlockSpec((tm, d), lambda i: (jnp.clip(i - na, 0, nb - 1), 0))]
    else:
        xs = (xs,)
        t, d = xs[0].shape
        n_first = None
        x_specs = [pl.BlockSpec((tm, d), lambda i: (i, 0))]
    width = g_v.shape[-1]
    return pl.pallas_call(
        functools.partial(_gmlp_kernel, n_first=n_first),
        grid=(t // tm,),
        in_specs=x_specs + [
            pl.BlockSpec((1, 6, d), lambda i: ((i * tm) // seg, 0, 0)),
            _const_spec((1, d)),
            _const_spec(w_in.shape),
            _const_spec((1, width)),
            _const_spec(w_s.shape),
            _const_spec((GM_CHUNK, GM_GROUPS)),
            _const_spec(w_out.shape),
        ],
        out_specs=pl.BlockSpec((tm, d), lambda i: (i, 0)),
        out_shape=jax.ShapeDtypeStruct((t, d), F32),
        scratch_shapes=[pltpu.VMEM((tm, width), BF16)],
        compiler_params=_params("parallel"),
        name="gmlp_mixer",
    )(*xs, mod, g.reshape(1, d), w_in.astype(BF16), g_v.reshape(1, width),
      w_s.astype(BF16), b_s.T, w_out.astype(BF16))


def _conv_kernel(start_ref, end_ref, x_ref, xp_ref, xn_ref, mod_ref, g_ref, win_ref,
                 wc_ref, wout_ref, o_ref, h_ref, z_ref):
    i = pl.program_id(0)
    x = x_ref[...]
    tm, d = x.shape
    g = g_ref[...]
    sh = mod_ref[0, 0:1, :]
    sc = mod_ref[0, 1:2, :]
    h_ref[0:HALO, :] = _norm_mod(xp_ref[...], g, sh, sc).astype(BF16)
    h_ref[HALO:HALO + tm, :] = _norm_mod(x, g, sh, sc).astype(BF16)
    h_ref[HALO + tm:, :] = _norm_mod(xn_ref[...], g, sh, sc).astype(BF16)
    bcx = jnp.dot(h_ref[...], win_ref[...], preferred_element_type=F32)
    w = bcx.shape[1] // 3
    bg = bcx[HALO:HALO + tm, :w]
    z_ref[...] = bcx[:, w:2 * w] * bcx[:, 2 * w:]
    keep_prev = (start_ref[i] == 0).astype(F32)
    keep_next = (end_ref[i] == 0).astype(F32)
    z_ref[0:HALO, :] = z_ref[0:HALO, :] * keep_prev
    z_ref[HALO + tm:, :] = z_ref[HALO + tm:, :] * keep_next
    y = (wc_ref[0:1, :] * z_ref[HALO - 1:HALO - 1 + tm, :]
         + wc_ref[1:2, :] * z_ref[HALO:HALO + tm, :]
         + wc_ref[2:3, :] * z_ref[HALO + 1:HALO + 1 + tm, :])
    out = jnp.dot((bg * y).astype(BF16), wout_ref[...], preferred_element_type=F32)
    o_ref[...] = x + mod_ref[0, 2:3, :] * out


def _conv_call(x, mod, g, w_in, w_conv, w_out, seg, tile_start, tile_end):
    t, d = x.shape
    tm = TM_CONV
    w = w_conv.shape[-1]
    hb = tm // HALO
    last = t // HALO - 1
    grid_spec = pltpu.PrefetchScalarGridSpec(
        num_scalar_prefetch=2,
        grid=(t // tm,),
        in_specs=[
            pl.BlockSpec((tm, d), lambda i, *_: (i, 0)),
            pl.BlockSpec((HALO, d), lambda i, *_: (jnp.maximum(i * hb - 1, 0), 0)),
            pl.BlockSpec((HALO, d), lambda i, *_: (jnp.minimum((i + 1) * hb, last), 0)),
            pl.BlockSpec((1, 6, d), lambda i, *_: ((i * tm) // seg, 0, 0)),
            _const_spec((1, d)),
            _const_spec(w_in.shape),
            _const_spec(w_conv.shape),
            _const_spec(w_out.shape),
        ],
        out_specs=pl.BlockSpec((tm, d), lambda i, *_: (i, 0)),
        scratch_shapes=[pltpu.VMEM((tm + 2 * HALO, d), BF16),
                        pltpu.VMEM((tm + 2 * HALO, w), F32)],
    )
    return pl.pallas_call(
        _conv_kernel,
        grid_spec=grid_spec,
        out_shape=jax.ShapeDtypeStruct((t, d), F32),
        compiler_params=_params("parallel"),
        name="shortconv_mixer",
    )(tile_start, tile_end, x, x, x, mod, g.reshape(1, d), w_in.astype(BF16),
      w_conv, w_out.astype(BF16))


QH = 3 * LANES
KH = 2 * LANES


def _mla_pre_kernel(pos_ref, x_ref, mod_ref, g_ref, wd_ref, gq_ref, gkv_ref, wuq_ref,
                    wukv_ref, gh_ref, cos_ref, sin_ref, q_ref, k_ref, v_ref):
    del pos_ref
    x = x_ref[...]
    h = _norm_mod(x, g_ref[...], mod_ref[0, 0:1, :], mod_ref[0, 1:2, :]).astype(BF16)
    lat = jnp.dot(h, wd_ref[...], preferred_element_type=F32)
    ql = lat[:, :MLA_Q_RANK]
    kvl = lat[:, MLA_Q_RANK:MLA_Q_RANK + MLA_KV_RANK]
    pe = lat[:, MLA_Q_RANK + MLA_KV_RANK:MLA_Q_RANK + MLA_KV_RANK + LANES]
    pe_sw = lat[:, MLA_Q_RANK + MLA_KV_RANK + LANES:]
    ql = ql * lax.rsqrt(jnp.mean(ql * ql, axis=-1, keepdims=True) + NORM_EPS) * gq_ref[...]
    kvl = kvl * lax.rsqrt(jnp.mean(kvl * kvl, axis=-1, keepdims=True) + NORM_EPS) * gkv_ref[...]
    q = jnp.dot(ql.astype(BF16), wuq_ref[...], preferred_element_type=F32)
    kv = jnp.dot(kvl.astype(BF16), wukv_ref[...], preferred_element_type=F32)
    cos = cos_ref[...]
    sin = sin_ref[...]
    gqn, gqr, gqs = gh_ref[0:1, :], gh_ref[1:2, :], gh_ref[2:3, :]
    gkn, gkr, gks = gh_ref[3:4, :], gh_ref[4:5, :], gh_ref[5:6, :]
    pe_ss = jnp.sum(pe * pe, axis=-1, keepdims=True)
    qscale = MLA_DK ** -0.5 * LOG2_E
    for hd in range(MLA_HEADS):
        qn = q[:, hd * QH:hd * QH + LANES]
        qr = q[:, hd * QH + LANES:hd * QH + 2 * LANES]
        qs = q[:, hd * QH + 2 * LANES:(hd + 1) * QH]
        ss = jnp.sum(qn * qn, axis=-1, keepdims=True) + jnp.sum(qr * qr, axis=-1, keepdims=True)
        rinv = lax.rsqrt(ss * (1.0 / MLA_DK) + NORM_EPS) * qscale
        q_ref[hd, :, 0:LANES] = (qn * rinv * gqn).astype(BF16)
        q_ref[hd, :, LANES:] = ((qr * rinv * gqr) * cos + (qs * rinv * gqs) * sin).astype(BF16)
        kn = kv[:, hd * 2 * LANES:hd * 2 * LANES + LANES]
        ss = jnp.sum(kn * kn, axis=-1, keepdims=True) + pe_ss
        rinv = lax.rsqrt(ss * (1.0 / MLA_DK) + NORM_EPS)
        k_ref[hd, :, 0:LANES] = (kn * rinv * gkn).astype(BF16)
        k_ref[hd, :, LANES:] = ((pe * rinv * gkr) * cos + (pe_sw * rinv * gks) * sin).astype(BF16)
        v_ref[hd] = kv[:, hd * 2 * LANES + LANES:(hd + 1) * 2 * LANES].T.astype(BF16)


def _mla_prepare_weights(w_down, w_uq, w_ukv, g_qh, g_kh):
    half = MLA_ROPE // 2
    pad = LANES - MLA_ROPE

    def swap(a):
        return jnp.concatenate([a[..., half:], a[..., :half]], axis=-1)

    def pad_lanes(a):
        return jnp.pad(a, [(0, 0)] * (a.ndim - 1) + [(0, pad)])

    d = w_down.shape[0]
    pe_w = w_down[:, MLA_Q_RANK + MLA_KV_RANK:]
    wd = jnp.concatenate([w_down[:, :MLA_Q_RANK + MLA_KV_RANK],
                          pad_lanes(pe_w), pad_lanes(swap(pe_w))], axis=1)
    uq = w_uq.reshape(MLA_Q_RANK, MLA_HEADS, MLA_DK)
    uq_r = uq[..., MLA_NOPE:]
    uq = jnp.concatenate([uq[..., :MLA_NOPE], pad_lanes(uq_r), pad_lanes(swap(uq_r))], axis=-1)
    uq = uq.reshape(MLA_Q_RANK, MLA_HEADS * QH)

    def gains(gv):
        r = gv[MLA_NOPE:]
        return [gv[:MLA_NOPE], pad_lanes(r), pad_lanes(swap(r))]

    gh = jnp.stack(gains(g_qh) + gains(g_kh) + [jnp.zeros((LANES,), F32)] * 2)
    return wd.astype(BF16), uq.astype(BF16), w_ukv.astype(BF16), gh


def _rope_tables(s):
    half = MLA_ROPE // 2
    inv = 1.0 / (ROPE_THETA ** (jnp.arange(0, MLA_ROPE, 2, dtype=F32) / MLA_ROPE))
    ang = jnp.arange(s, dtype=F32)[:, None] * inv[None, :]
    cos, sin = jnp.cos(ang), jnp.sin(ang)
    zeros = jnp.zeros((s, LANES - MLA_ROPE), F32)
    del half
    return (jnp.concatenate([cos, cos, zeros], axis=1),
            jnp.concatenate([-sin, sin, zeros], axis=1))


def _mla_pre_call(x, mod, g, wd, g_q_lat, g_kv_lat, uq, ukv, gh, cos, sin, seg, tile_pos):
    t, d = x.shape
    tm = TM_MLA
    grid_spec = pltpu.PrefetchScalarGridSpec(
        num_scalar_prefetch=1,
        grid=(t // tm,),
        in_specs=[
            pl.BlockSpec((tm, d), lambda i, p: (i, 0)),
            pl.BlockSpec((1, 6, d), lambda i, p: ((i * tm) // seg, 0, 0)),
            _const_spec((1, d)),
            _const_spec(wd.shape),
            _const_spec((1, MLA_Q_RANK)),
            _const_spec((1, MLA_KV_RANK)),
            _const_spec(uq.shape),
            _const_spec(ukv.shape),
            _const_spec(gh.shape),
            pl.BlockSpec((tm, LANES), lambda i, p: (p[i], 0)),
            pl.BlockSpec((tm, LANES), lambda i, p: (p[i], 0)),
        ],
        out_specs=[
            pl.BlockSpec((MLA_HEADS, tm, KH), lambda i, p: (0, i, 0)),
            pl.BlockSpec((MLA_HEADS, tm, KH), lambda i, p: (0, i, 0)),
            pl.BlockSpec((MLA_HEADS, MLA_V, tm), lambda i, p: (0, 0, i)),
        ],
    )
    return pl.pallas_call(
        _mla_pre_kernel,
        grid_spec=grid_spec,
        out_shape=[jax.ShapeDtypeStruct((MLA_HEADS, t, KH), BF16),
                   jax.ShapeDtypeStruct((MLA_HEADS, t, KH), BF16),
                   jax.ShapeDtypeStruct((MLA_HEADS, MLA_V, t), BF16)],
        compiler_params=_params("parallel"),
        name="mla_qkv",
    )(tile_pos, x, mod, g.reshape(1, d), wd, g_q_lat.reshape(1, -1), g_kv_lat.reshape(1, -1),
      uq, ukv, gh, cos, sin)


def _attn_kernel(q_ref, k_ref, vt_ref, o_ref, s_buf, p_buf):
    tk = TK_ATTN
    n = k_ref.shape[1] // tk
    nt = (((1,), (1,)), ((), ()))
    tq = q_ref.shape[1] // ATTN_SUBTILES
    subs = range(ATTN_SUBTILES)
    qs = [q_ref[0, sub * tq:(sub + 1) * tq, :] for sub in subs]

    def scores(sub, j):
        return lax.dot_general(k_ref[0, j * tk:(j + 1) * tk, :], qs[sub], nt,
                               preferred_element_type=F32)

    def values(sub, j):
        return jnp.dot(vt_ref[0, :, j * tk:(j + 1) * tk], p_buf[sub, j % 2],
                       preferred_element_type=F32)

    m = [jnp.full((1, tq), -jnp.inf, F32) for _ in subs]
    l = [jnp.zeros((1, tq), F32) for _ in subs]
    acc = [jnp.zeros((MLA_V, tq), F32) for _ in subs]
    alpha = [None for _ in subs]
    for sub in subs:
        s_buf[sub, 0] = scores(sub, 0)
    for j in range(n):
        cur = j % 2
        for sub in subs:
            if j + 1 < n:
                s_buf[sub, 1 - cur] = scores(sub, j + 1)
            if j > 0:
                acc[sub] = alpha[sub] * acc[sub] + values(sub, j - 1)
            s = s_buf[sub, cur]
            m_new = jnp.maximum(m[sub], jnp.max(s, axis=0, keepdims=True))
            alpha[sub] = jnp.exp2(m[sub] - m_new)
            p = jnp.exp2(s - m_new)
            l[sub] = alpha[sub] * l[sub] + jnp.sum(p, axis=0, keepdims=True)
            m[sub] = m_new
            p_buf[sub, cur] = p.astype(BF16)
    for sub in subs:
        a = alpha[sub] * acc[sub] + values(sub, n - 1)
        o_ref[sub * tq:(sub + 1) * tq, :] = (a / l[sub]).T.astype(BF16)


def _attn_call(q, k, vt, tok0, n_seq, s_len):
    tq, tk = TQ_ATTN * ATTN_SUBTILES, TK_ATTN
    assert s_len % tk == 0
    nq = s_len // tq
    qb0 = tok0 // tq
    sb0 = tok0 // s_len
    return pl.pallas_call(
        _attn_kernel,
        grid=(n_seq, MLA_HEADS, nq),
        in_specs=[
            pl.BlockSpec((1, tq, KH), lambda s, h, i: (h, qb0 + s * nq + i, 0)),
            pl.BlockSpec((1, s_len, KH), lambda s, h, i: (h, sb0 + s, 0)),
            pl.BlockSpec((1, MLA_V, s_len), lambda s, h, i: (h, 0, sb0 + s)),
        ],
        out_specs=pl.BlockSpec((tq, MLA_V), lambda s, h, i: (s * nq + i, h)),
        out_shape=jax.ShapeDtypeStruct((n_seq * s_len, MLA_HEADS * MLA_V), BF16),
        scratch_shapes=[pltpu.VMEM((ATTN_SUBTILES, 2, tk, TQ_ATTN), F32),
                        pltpu.VMEM((ATTN_SUBTILES, 2, tk, TQ_ATTN), BF16)],
        compiler_params=_params("parallel", "parallel", "arbitrary"),
        name="mla_attention",
    )(q, k, vt)


def _proj_res_kernel(x_ref, oa_ref, ob_ref, mod_ref, w_ref, out_ref, *, n_first):
    o = jnp.where(pl.program_id(0) < n_first, oa_ref[...], ob_ref[...])
    out = jnp.dot(o, w_ref[...], preferred_element_type=F32)
    out_ref[...] = x_ref[...] + mod_ref[0, 2:3, :] * out


def _proj_res_call(x, oa, ob, mod, w_o, seg):
    t, d = x.shape
    tm = TM_OUT
    na, nb = oa.shape[0] // tm, ob.shape[0] // tm
    assert (na + nb) * tm == t
    return pl.pallas_call(
        functools.partial(_proj_res_kernel, n_first=na),
        grid=(t // tm,),
        in_specs=[
            pl.BlockSpec((tm, d), lambda i: (i, 0)),
            pl.BlockSpec((tm, oa.shape[1]), lambda i: (jnp.minimum(i, na - 1), 0)),
            pl.BlockSpec((tm, ob.shape[1]), lambda i: (jnp.clip(i - na, 0, nb - 1), 0)),
            pl.BlockSpec((1, 6, d), lambda i: ((i * tm) // seg, 0, 0)),
            _const_spec(w_o.shape),
        ],
        out_specs=pl.BlockSpec((tm, d), lambda i: (i, 0)),
        out_shape=jax.ShapeDtypeStruct((t, d), F32),
        compiler_params=_params("parallel"),
        name="mla_out_proj",
    )(x, oa, ob, mod, w_o.astype(BF16))


def _router_kernel(x_ref, mod_ref, g_ref, wr_hi_ref, wr_lo_ref, br_ref,
                   h_ref, ids_ref, gates_ref, rank_ref, gcol_ref, cnt_ref, carry_ref):
    i = pl.program_id(0)

    @pl.when(i == 0)
    def _():
        carry_ref[...] = jnp.zeros_like(carry_ref)

    x = x_ref[...]
    tm = x.shape[0]
    h = _norm_mod(x, g_ref[...], mod_ref[0, 3:4, :], mod_ref[0, 4:5, :])
    h_hi = h.astype(BF16)
    h_hi32 = h_hi.astype(F32)
    h_ref[...] = _pack_rounded(h_hi32)
    h_lo = (h - h_hi32).astype(BF16)
    nt = (((1,), (1,)), ((), ()))
    logits = (lax.dot_general(wr_hi_ref[...], h_hi, nt, preferred_element_type=F32)
              + lax.dot_general(wr_hi_ref[...], h_lo, nt, preferred_element_type=F32)
              + lax.dot_general(wr_lo_ref[...], h_hi, nt, preferred_element_type=F32)
              + br_ref[...])
    eidx = lax.broadcasted_iota(jnp.int32, logits.shape, 0)
    lg = logits
    sel = jnp.zeros(logits.shape, F32)
    vals, ids = [], []
    for _ in range(TOP_K):
        m = jnp.max(lg, axis=0, keepdims=True)
        idx = jnp.min(jnp.where(lg == m, eidx, N_EXPERTS), axis=0, keepdims=True)
        onehot = eidx == idx
        vals.append(m)
        ids.append(idx)
        lg = jnp.where(onehot, -jnp.inf, lg)
        sel = jnp.where(onehot, 1.0, sel)
    ex = [jnp.exp(v - vals[0]) for v in vals]
    den = ex[0] + ex[1] + ex[2] + ex[3]
    gates = [e / den for e in ex]
    r_io = lax.broadcasted_iota(jnp.int32, (tm, tm), 0)
    c_io = lax.broadcasted_iota(jnp.int32, (tm, tm), 1)
    before = jnp.where(r_io < c_io, 1.0, 0.0).astype(BF16)
    cum = jnp.dot(sel.astype(BF16), before, preferred_element_type=F32) + carry_ref[:, 0:1]
    for k in range(TOP_K):
        rk = jnp.sum(jnp.where(eidx == ids[k], cum, 0.0), axis=0, keepdims=True)
        rank_ref[k:k + 1, :] = rk.astype(jnp.int32)
        ids_ref[k:k + 1, :] = ids[k]
        gates_ref[k:k + 1, :] = gates[k]
    g8 = jnp.concatenate(gates + [jnp.zeros((LANES - TOP_K, tm), F32)], axis=0)
    gcol_ref[...] = g8.T
    carry_ref[...] = carry_ref[...] + jnp.sum(sel, axis=1, keepdims=True)
    cnt_ref[...] = carry_ref[...]


def _router_call(x, mod, g, w_r, b_r, seg):
    t, d = x.shape
    tm = TM_ROUTER
    e = w_r.shape[1]
    wr_hi = w_r.T.astype(BF16)
    wr_lo = (w_r.T - wr_hi.astype(F32)).astype(BF16)
    return pl.pallas_call(
        _router_kernel,
        grid=(t // tm,),
        in_specs=[
            pl.BlockSpec((tm, d), lambda i: (i, 0)),
            pl.BlockSpec((1, 6, d), lambda i: ((i * tm) // seg, 0, 0)),
            _const_spec((1, d)),
            _const_spec((e, d)),
            _const_spec((e, d)),
            _const_spec((e, 1)),
        ],
        out_specs=[
            pl.BlockSpec((tm, d // 2), lambda i: (i, 0)),
            pl.BlockSpec((TOP_K, tm), lambda i: (0, i)),
            pl.BlockSpec((TOP_K, tm), lambda i: (0, i)),
            pl.BlockSpec((TOP_K, tm), lambda i: (0, i)),
            pl.BlockSpec((tm, LANES), lambda i: (i, 0)),
            _const_spec((e, LANES)),
        ],
        out_shape=[
            jax.ShapeDtypeStruct((t, d // 2), jnp.int32),
            jax.ShapeDtypeStruct((TOP_K, t), jnp.int32),
            jax.ShapeDtypeStruct((TOP_K, t), F32),
            jax.ShapeDtypeStruct((TOP_K, t), jnp.int32),
            jax.ShapeDtypeStruct((t, LANES), F32),
            jax.ShapeDtypeStruct((e, LANES), F32),
        ],
        scratch_shapes=[pltpu.VMEM((e, LANES), F32)],
        compiler_params=_params("arbitrary"),
        name="moe_router",
    )(x, mod, g.reshape(1, d), wr_hi, wr_lo, b_r.reshape(e, 1))


SC_CORES = 2
SC_SUBCORES = 16
SC_WORKERS = SC_CORES * SC_SUBCORES
SC_CHUNK = 64


def _dispatch_rows(h, dest, pad_idx, rows):
    t, d = h.shape
    top_k = dest.shape[0]
    per_w = t // SC_WORKERS
    n_ch = per_w // SC_CHUNK
    pad_ch = pad_idx.shape[0] // (SC_WORKERS * SC_CHUNK)
    assert n_ch * SC_CHUNK * SC_WORKERS == t
    assert pad_ch * SC_CHUNK * SC_WORKERS == pad_idx.shape[0]
    assert top_k * t + pad_idx.shape[0] == rows
    mesh = plsc.VectorSubcoreMesh(core_axis_name="c", subcore_axis_name="s")

    @functools.partial(
        pl.kernel, mesh=mesh,
        out_type=jax.ShapeDtypeStruct((rows, d), h.dtype),
        scratch_types=[pltpu.VMEM((top_k * n_ch, SC_CHUNK), jnp.int32),
                       pltpu.VMEM((pad_ch, SC_CHUNK), jnp.int32),
                       pltpu.VMEM((SC_CHUNK, d), h.dtype),
                       pltpu.SemaphoreType.DMA],
        name="sc_row_dispatch",
    )
    def dispatch(h_hbm, dest_hbm, pad_hbm, zero_hbm, out_hbm, idx_v, pad_v, rows_v, sem):
        wid = lax.axis_index("s") * SC_CORES + lax.axis_index("c")
        base = wid * per_w
        pltpu.sync_copy(dest_hbm.at[wid], idx_v)
        pltpu.sync_copy(pad_hbm.at[wid], pad_v)

        @pl.loop(0, n_ch)
        def _(j):
            pltpu.sync_copy(h_hbm.at[pl.ds(base + j * SC_CHUNK, SC_CHUNK)], rows_v)
            copies = [pltpu.async_copy(rows_v, out_hbm.at[idx_v.at[k * n_ch + j]], sem)
                      for k in range(top_k)]
            for c in copies:
                c.wait()

        pltpu.sync_copy(zero_hbm, rows_v)

        @pl.loop(0, pad_ch)
        def _(p):
            pltpu.sync_copy(rows_v, out_hbm.at[pad_v.at[p]])

    dest_w = dest.reshape(top_k, SC_WORKERS, n_ch, SC_CHUNK).transpose(1, 0, 2, 3)
    return dispatch(h, dest_w.reshape(SC_WORKERS, top_k * n_ch, SC_CHUNK),
                    pad_idx.reshape(SC_WORKERS, pad_ch, SC_CHUNK),
                    jnp.zeros((SC_CHUNK, d), h.dtype))


def _gather_rows(src, idx):
    n = idx.shape[0]
    d = src.shape[1]
    per_w = n // SC_WORKERS
    n_ch = per_w // SC_CHUNK
    assert per_w * SC_WORKERS == n and n_ch * SC_CHUNK == per_w
    mesh = plsc.VectorSubcoreMesh(core_axis_name="c", subcore_axis_name="s")

    @functools.partial(
        pl.kernel, mesh=mesh,
        out_type=jax.ShapeDtypeStruct((n, d), src.dtype),
        scratch_types=[pltpu.VMEM((n_ch, SC_CHUNK), jnp.int32),
                       pltpu.VMEM((SC_CHUNK, d), src.dtype),
                       pltpu.SemaphoreType.DMA],
        name="sc_row_gather",
    )
    def gather(src_hbm, idx_hbm, out_hbm, idx_v, rows_v, sem):
        wid = lax.axis_index("s") * SC_CORES + lax.axis_index("c")
        base = wid * per_w
        pltpu.sync_copy(idx_hbm.at[wid], idx_v)

        @pl.loop(0, n_ch)
        def _(j):
            pltpu.async_copy(src_hbm.at[idx_v.at[j]], rows_v, sem).wait()
            pltpu.sync_copy(rows_v, out_hbm.at[pl.ds(base + j * SC_CHUNK, SC_CHUNK)])

    return gather(src, idx.reshape(SC_WORKERS, n_ch, SC_CHUNK))


def _expert_kernel(blk_e_ref, nvalid_ref, x_ref, wgu_ref, bgu_ref, wdn_ref, bdn_ref,
                   o_ref, wgu_bf, wdn_bf):
    i = pl.program_id(0)
    e = blk_e_ref[i]
    e_prev = blk_e_ref[jnp.maximum(i - 1, 0)]
    active = nvalid_ref[i] > 0
    d_e = wdn_ref.shape[2]

    @pl.when(active & ((i == 0) | (e != e_prev)))
    def _():
        step = 128
        for r in range(0, wgu_ref.shape[2], step):
            wgu_bf[r:r + step, :] = wgu_ref[0, 0, r:r + step, :].astype(BF16)
        for r in range(0, d_e, step):
            wdn_bf[r:r + step, :] = wdn_ref[0, 0, r:r + step, :].astype(BF16)

    @pl.when(active)
    def _():
        lo, hi = _unpack_rows(x_ref[...])
        xb = jnp.concatenate([lo.astype(BF16), hi.astype(BF16)], axis=1)
        gu = jnp.dot(xb, wgu_bf[...], preferred_element_type=F32) + bgu_ref[0, 0]
        gt = jnp.minimum(gu[:, :d_e], SWIGLU_LIMIT)
        up = jnp.clip(gu[:, d_e:], -SWIGLU_LIMIT, SWIGLU_LIMIT)
        a = (up + 1.0) * (gt * jax.nn.sigmoid(SWIGLU_ALPHA * gt))
        o_ref[...] = _pack_rows(jnp.dot(a.astype(BF16), wdn_bf[...],
                                        preferred_element_type=F32) + bdn_ref[0, 0])

    @pl.when(jnp.logical_not(active))
    def _():
        o_ref[...] = jnp.zeros_like(o_ref)


def _expert_call(xg, blk_e, n_valid, w_gu, b_gu, w_dn, b_dn, layer):
    rows, half = xg.shape
    d = 2 * half
    nb = rows // MOE_ROWS
    _, n_e, _, two_de = w_gu.shape
    d_e = two_de // 2
    grid_spec = pltpu.PrefetchScalarGridSpec(
        num_scalar_prefetch=2,
        grid=(nb,),
        in_specs=[
            pl.BlockSpec((MOE_ROWS, half), lambda i, be, nu: (i, 0)),
            pl.BlockSpec((1, 1, d, two_de), lambda i, be, nu: (layer, be[i], 0, 0)),
            pl.BlockSpec((1, 1, 1, two_de), lambda i, be, nu: (layer, be[i], 0, 0)),
            pl.BlockSpec((1, 1, d_e, d), lambda i, be, nu: (layer, be[i], 0, 0)),
            pl.BlockSpec((1, 1, 1, d), lambda i, be, nu: (layer, be[i], 0, 0)),
        ],
        out_specs=pl.BlockSpec((MOE_ROWS, half), lambda i, be, nu: (i, 0)),
        scratch_shapes=[pltpu.VMEM((d, two_de), BF16), pltpu.VMEM((d_e, d), BF16)],
    )
    return pl.pallas_call(
        _expert_kernel,
        grid_spec=grid_spec,
        out_shape=jax.ShapeDtypeStruct((rows, half), jnp.int32),
        compiler_params=_params("arbitrary"),
        name="moe_experts",
    )(blk_e, n_valid, xg, w_gu, b_gu.reshape(b_gu.shape[0], n_e, 1, two_de),
      w_dn, b_dn.reshape(b_dn.shape[0], n_e, 1, d))


def _combine_kernel(x_ref, y_ref, gcol_ref, mod_ref, o_ref):
    gc = gcol_ref[...]
    half = y_ref.shape[2]
    acc_lo = acc_hi = None
    for k in range(TOP_K):
        lo, hi = _unpack_rows(y_ref[k])
        gate = gc[:, k:k + 1]
        acc_lo = gate * lo if k == 0 else acc_lo + gate * lo
        acc_hi = gate * hi if k == 0 else acc_hi + gate * hi
    o_ref[:, :half] = x_ref[:, :half] + mod_ref[0, 5:6, :half] * acc_lo
    o_ref[:, half:] = x_ref[:, half:] + mod_ref[0, 5:6, half:] * acc_hi


def _combine_call(x, yk, gcol, mod, seg, tok0, in_place):
    t, d = x.shape
    tm = TM_OUT
    tc = yk.shape[1]
    nb = tc // tm
    b0 = tok0 // tm
    ob = b0 if in_place else 0
    return pl.pallas_call(
        _combine_kernel,
        grid=(nb,),
        in_specs=[
            pl.BlockSpec((tm, d), lambda i: (b0 + i, 0)),
            pl.BlockSpec((TOP_K, tm, d // 2), lambda i: (0, i, 0)),
            pl.BlockSpec((tm, LANES), lambda i: (b0 + i, 0)),
            pl.BlockSpec((1, 6, d), lambda i: (((b0 + i) * tm) // seg, 0, 0)),
        ],
        out_specs=pl.BlockSpec((tm, d), lambda i: (ob + i, 0)),
        out_shape=jax.ShapeDtypeStruct((t if in_place else tc, d), F32),
        input_output_aliases={0: 0} if in_place else {},
        compiler_params=_params("parallel"),
        name="moe_combine",
    )(x, yk, gcol, mod)


def _moe_layer(x, mod, g, w_r, b_r, w_gu, b_gu, w_dn, b_dn, layer, seg, out_split=None):
    t, d = x.shape
    h, ids, _, rank, gcol, cnt = _router_call(x, mod, g, w_r, b_r, seg)
    counts = cnt[:, 0].astype(jnp.int32)
    padded = (counts + MOE_ROWS - 1) // MOE_ROWS * MOE_ROWS
    pad_end = jnp.cumsum(padded)
    pad_start = pad_end - padded
    e_ar = jnp.arange(N_EXPERTS, dtype=jnp.int32)
    dest = rank + jnp.sum(jnp.where(ids[..., None] == e_ar, pad_start, 0), axis=-1)
    nb = -(-(t * TOP_K) // MOE_ROWS) + N_EXPERTS
    rows = nb * MOE_ROWS
    slack = padded - counts
    slack_end = jnp.cumsum(slack)
    r = jnp.arange(rows - t * TOP_K, dtype=jnp.int32)
    owner = jnp.sum((slack_end[None, :] <= r[:, None]).astype(jnp.int32), axis=1)
    first = pad_start + counts - (slack_end - slack)
    in_expert = r + jnp.sum(jnp.where(owner[:, None] == e_ar, first, 0), axis=-1)
    pad_idx = jnp.where(owner < N_EXPERTS, in_expert, pad_end[-1] + r - slack_end[-1])
    blk_row = jnp.arange(nb, dtype=jnp.int32) * MOE_ROWS
    blk_e = jnp.minimum(jnp.sum((pad_end[None, :] <= blk_row[:, None]).astype(jnp.int32), axis=1),
                        N_EXPERTS - 1)
    tok_end = jnp.sum(jnp.where(blk_e[:, None] == e_ar, pad_start + counts, 0), axis=-1)
    n_valid = jnp.where(blk_row < pad_end[-1], jnp.clip(tok_end - blk_row, 0, MOE_ROWS), 0)
    xg = _dispatch_rows(h, dest, pad_idx, rows)
    yb = _expert_call(xg, blk_e, n_valid, w_gu, b_gu, w_dn, b_dn, layer)
    if out_split is None:
        bounds = [c * (t // COMBINE_CHUNKS) for c in range(COMBINE_CHUNKS + 1)]
    else:
        bounds = [0, out_split, t]
    outs = []
    for lo, hi in zip(bounds[:-1], bounds[1:]):
        yk = _gather_rows(yb, dest[:, lo:hi].reshape(-1)).reshape(TOP_K, hi - lo, d // 2)
        if out_split is None:
            x = _combine_call(x, yk, gcol, mod, seg, lo, True)
        else:
            outs.append(_combine_call(x, yk, gcol, mod, seg, lo, False))
    return x if out_split is None else tuple(outs)


def kernel(x_prompt, x_sample, c_prompt, c_sample, g_mix, g_ffn, w_ada, b_ada, gm_w_in, gm_g_v, gm_w_s, gm_b_s, gm_w_out, sc_w_in, sc_w_conv, sc_w_out, mla_w_down, mla_g_q_lat, mla_g_kv_lat, mla_w_uq, mla_w_ukv, mla_g_qh, mla_g_kh, mla_w_o, moe_w_router, moe_b_router, moe_w_gu, moe_b_gu, moe_w_dn, moe_b_dn):
    bp, sp, d = x_prompt.shape
    bs, ss, _ = x_sample.shape
    tp, ts = bp * sp, bs * ss
    t = tp + ts
    depth = g_mix.shape[0]
    seg = min(sp, ss)
    assert sp % seg == 0 and ss % seg == 0

    x = (x_prompt.reshape(tp, d), x_sample.reshape(ts, d))

    n_seq = bp + bs
    c_all = jnp.concatenate([c_prompt, c_sample], axis=0)
    c_pad = jnp.pad(c_all, ((0, (-n_seq) % 8), (0, 0)))
    mod_all = _ada_call(c_pad, w_ada, b_ada)[:, :n_seq].reshape(depth, n_seq, 6, d)
    seg_seq = jnp.concatenate([jnp.repeat(jnp.arange(bp), sp // seg),
                               bp + jnp.repeat(jnp.arange(bs), ss // seg)])
    mod_seg = mod_all[:, seg_seq]

    def tile_meta(tm):
        t0 = jnp.arange(t // tm, dtype=jnp.int32) * tm
        pos = jnp.where(t0 < tp, t0 % sp, (t0 - tp) % ss)
        slen = jnp.where(t0 < tp, sp, ss)
        return pos, slen

    for i in range(depth):
        mod = mod_seg[i]
        kind, j = i % 3, i // 3
        if kind == 0:
            x = _gmlp_call(x, mod, g_mix[i], gm_w_in[j], gm_g_v[j], gm_w_s[j], gm_b_s[j],
                           gm_w_out[j], seg)
        elif kind == 1:
            pos, slen = tile_meta(TM_CONV)
            x = _conv_call(x, mod, g_mix[i], sc_w_in[j], sc_w_conv[j], sc_w_out[j], seg,
                           (pos != 0).astype(jnp.int32),
                           (pos + TM_CONV != slen).astype(jnp.int32))
        else:
            wd, uq, ukv, gh = _mla_prepare_weights(mla_w_down[j], mla_w_uq[j], mla_w_ukv[j],
                                                  mla_g_qh[j], mla_g_kh[j])
            cos, sin = _rope_tables(max(sp, ss))
            pos, _ = tile_meta(TM_MLA)
            q, k, vt = _mla_pre_call(x, mod, g_mix[i], wd, mla_g_q_lat[j], mla_g_kv_lat[j],
                                     uq, ukv, gh, cos, sin, seg, pos // TM_MLA)
            x = _proj_res_call(x, _attn_call(q, k, vt, 0, bp, sp),
                               _attn_call(q, k, vt, tp, bs, ss), mod, mla_w_o[j], seg)
        x = _moe_layer(x, mod, g_ffn[i], moe_w_router[i], moe_b_router[i],
                       moe_w_gu, moe_b_gu, moe_w_dn, moe_b_dn, i, seg,
                       out_split=tp if i == depth - 1 else None)

    return (x[0].reshape(bp, sp, d), x[1].reshape(bs, ss, d))
```

```python
import functools

import jax
import jax.numpy as jnp
from jax import lax
from jax.experimental import pallas as pl
from jax.experimental.pallas import tpu as pltpu
from jax.experimental.pallas import tpu_sc as plsc

F32 = jnp.float32
BF16 = jnp.bfloat16

NORM_EPS = 1e-6
GM_GROUPS = 8
GM_CHUNK = 128
MLA_HEADS = 8
MLA_Q_RANK = 384
MLA_KV_RANK = 256
MLA_NOPE = 128
MLA_ROPE = 64
MLA_V = 128
MLA_DK = MLA_NOPE + MLA_ROPE
ROPE_THETA = 10000.0
N_EXPERTS = 32
TOP_K = 4
SWIGLU_LIMIT = 7.0
SWIGLU_ALPHA = 1.702
LOG2_E = 1.4426950408889634

LANES = 128
HALO = 16
VMEM_LIMIT = 56 * 1024 * 1024

ADA_COL_TILES = 4
TM_GMLP = 512
TM_CONV = 512
TM_MLA = 512
TM_ROUTER = 512
TM_OUT = 512
TQ_ATTN = 256
ATTN_SUBTILES = 4
TK_ATTN = 2048
MOE_ROWS = 512
COMBINE_CHUNKS = 4


def _params(*sem):
    return pltpu.CompilerParams(dimension_semantics=sem, vmem_limit_bytes=VMEM_LIMIT)


def _const_spec(shape):
    nd = len(shape)
    return pl.BlockSpec(shape, lambda *_: (0,) * nd)


def _norm_mod(x, g, shift, scale):
    ms = jnp.mean(x * x, axis=-1, keepdims=True)
    return (x * lax.rsqrt(ms + NORM_EPS)) * g * (1.0 + scale) + shift


def _pack_rows(x):
    return _pack_rounded(x.astype(BF16).astype(F32))


def _pack_rounded(xr):
    half = xr.shape[1] // 2
    lo = pltpu.bitcast(xr[:, :half], jnp.uint32)
    hi = pltpu.bitcast(xr[:, half:], jnp.uint32)
    word = lax.shift_right_logical(lo, jnp.uint32(16)) | (hi & jnp.uint32(0xFFFF0000))
    return pltpu.bitcast(word, jnp.int32)


def _unpack_rows(w):
    u = pltpu.bitcast(w, jnp.uint32)
    lo = pltpu.bitcast(lax.shift_left(u, jnp.uint32(16)), F32)
    hi = pltpu.bitcast(u & jnp.uint32(0xFFFF0000), F32)
    return lo, hi


def _ada_kernel(c_ref, w_ref, b_ref, o_ref):
    c = c_ref[...]
    a = (c * jax.nn.sigmoid(c)).astype(BF16)
    w = w_ref[0].astype(BF16)
    o_ref[0] = jnp.dot(a, w, preferred_element_type=F32) + b_ref[0]


def _ada_call(c_pad, w_ada, b_ada):
    depth, d, n = w_ada.shape
    tn = n // ADA_COL_TILES
    rows = c_pad.shape[0]
    return pl.pallas_call(
        _ada_kernel,
        grid=(depth, n // tn),
        in_specs=[
            pl.BlockSpec((rows, d), lambda l, j: (0, 0)),
            pl.BlockSpec((1, d, tn), lambda l, j: (l, 0, j)),
            pl.BlockSpec((1, 1, tn), lambda l, j: (l, 0, j)),
        ],
        out_specs=pl.BlockSpec((1, rows, tn), lambda l, j: (l, 0, j)),
        out_shape=jax.ShapeDtypeStruct((depth, rows, n), F32),
        compiler_params=_params("arbitrary", "arbitrary"),
        name="adaln_mod",
    )(c_pad, w_ada, b_ada.reshape(depth, 1, n))


def _gmlp_kernel(*refs, n_first):
    if n_first is None:
        x = refs[0][...]
        refs = refs[1:]
    else:
        x = jnp.where(pl.program_id(0) < n_first, refs[0][...], refs[1][...])
        refs = refs[2:]
    mod_ref, g_ref, win_ref, gv_ref, ws_ref, bs_ref, wout_ref, o_ref, gated_ref = refs
    tm = x.shape[0]
    width = gv_ref.shape[1]
    gw = width // GM_GROUPS
    h = _norm_mod(x, g_ref[...], mod_ref[0, 0:1, :], mod_ref[0, 1:2, :]).astype(BF16)
    uv = jax.nn.gelu(jnp.dot(h, win_ref[...], preferred_element_type=F32))
    u = uv[:, :width]
    v = uv[:, width:]
    v = v * lax.rsqrt(jnp.mean(v * v, axis=-1, keepdims=True) + NORM_EPS) * gv_ref[...]
    vb = v.astype(BF16)
    bs = bs_ref[...]
    for c in range(tm // GM_CHUNK):
        r0 = c * GM_CHUNK
        for g in range(GM_GROUPS):
            c0 = g * gw
            vm = jnp.dot(ws_ref[g], vb[r0:r0 + GM_CHUNK, c0:c0 + gw],
                         preferred_element_type=F32) + bs[:, g:g + 1]
            gated_ref[r0:r0 + GM_CHUNK, c0:c0 + gw] = (
                u[r0:r0 + GM_CHUNK, c0:c0 + gw] * vm).astype(BF16)
    out = jnp.dot(gated_ref[...], wout_ref[...], preferred_element_type=F32)
    o_ref[...] = x + mod_ref[0, 2:3, :] * out


def _gmlp_call(xs, mod, g, w_in, g_v, w_s, b_s, w_out, seg):
    tm = TM_GMLP
    if isinstance(xs, tuple):
        xa, xb = xs
        na, nb = xa.shape[0] // tm, xb.shape[0] // tm
        t, d = xa.shape[0] + xb.shape[0], xa.shape[1]
        n_first = na
        x_specs = [pl.BlockSpec((tm, d), lambda i: (jnp.minimum(i, na - 1), 0)),
                   pl.BlockSpec((tm, d), lambda i: (jnp.clip(i - na, 0, nb - 1), 0))]
    else:
        xs = (xs,)
        t, d = xs[0].shape
        n_first = None
        x_specs = [pl.BlockSpec((tm, d), lambda i: (i, 0))]
    width = g_v.shape[-1]
    return pl.pallas_call(
        functools.partial(_gmlp_kernel, n_first=n_first),
        grid=(t // tm,),
        in_specs=x_specs + [
            pl.BlockSpec((1, 6, d), lambda i: ((i * tm) // seg, 0, 0)),
            _const_spec((1, d)),
            _const_spec(w_in.shape),
            _const_spec((1, width)),
            _const_spec(w_s.shape),
            _const_spec((GM_CHUNK, GM_GROUPS)),
            _const_spec(w_out.shape),
        ],
        out_specs=pl.BlockSpec((tm, d), lambda i: (i, 0)),
        out_shape=jax.ShapeDtypeStruct((t, d), F32),
        scratch_shapes=[pltpu.VMEM((tm, width), BF16)],
        compiler_params=_params("parallel"),
        name="gmlp_mixer",
    )(*xs, mod, g.reshape(1, d), w_in.astype(BF16), g_v.reshape(1, width),
      w_s.astype(BF16), b_s.T, w_out.astype(BF16))


def _conv_kernel(start_ref, end_ref, x_ref, xp_ref, xn_ref, mod_ref, g_ref, win_ref,
                 wc_ref, wout_ref, o_ref, h_ref, z_ref):
    i = pl.program_id(0)
    x = x_ref[...]
    tm, d = x.shape
    g = g_ref[...]
    sh = mod_ref[0, 0:1, :]
    sc = mod_ref[0, 1:2, :]
    h_ref[0:HALO, :] = _norm_mod(xp_ref[...], g, sh, sc).astype(BF16)
    h_ref[HALO:HALO + tm, :] = _norm_mod(x, g, sh, sc).astype(BF16)
    h_ref[HALO + tm:, :] = _norm_mod(xn_ref[...], g, sh, sc).astype(BF16)
    bcx = jnp.dot(h_ref[...], win_ref[...], preferred_element_type=F32)
    w = bcx.shape[1] // 3
    bg = bcx[HALO:HALO + tm, :w]
    z_ref[...] = bcx[:, w:2 * w] * bcx[:, 2 * w:]
    keep_prev = (start_ref[i] == 0).astype(F32)
    keep_next = (end_ref[i] == 0).astype(F32)
    z_ref[0:HALO, :] = z_ref[0:HALO, :] * keep_prev
    z_ref[HALO + tm:, :] = z_ref[HALO + tm:, :] * keep_next
    y = (wc_ref[0:1, :] * z_ref[HALO - 1:HALO - 1 + tm, :]
         + wc_ref[1:2, :] * z_ref[HALO:HALO + tm, :]
         + wc_ref[2:3, :] * z_ref[HALO + 1:HALO + 1 + tm, :])
    out = jnp.dot((bg * y).astype(BF16), wout_ref[...], preferred_element_type=F32)
    o_ref[...] = x + mod_ref[0, 2:3, :] * out


def _conv_call(x, mod, g, w_in, w_conv, w_out, seg, tile_start, tile_end):
    t, d = x.shape
    tm = TM_CONV
    w = w_conv.shape[-1]
    hb = tm // HALO
    last = t // HALO - 1
    grid_spec = pltpu.PrefetchScalarGridSpec(
        num_scalar_prefetch=2,
        grid=(t // tm,),
        in_specs=[
            pl.BlockSpec((tm, d), lambda i, *_: (i, 0)),
            pl.BlockSpec((HALO, d), lambda i, *_: (jnp.maximum(i * hb - 1, 0), 0)),
            pl.BlockSpec((HALO, d), lambda i, *_: (jnp.minimum((i + 1) * hb, last), 0)),
            pl.BlockSpec((1, 6, d), lambda i, *_: ((i * tm) // seg, 0, 0)),
            _const_spec((1, d)),
            _const_spec(w_in.shape),
            _const_spec(w_conv.shape),
            _const_spec(w_out.shape),
        ],
        out_specs=pl.BlockSpec((tm, d), lambda i, *_: (i, 0)),
        scratch_shapes=[pltpu.VMEM((tm + 2 * HALO, d), BF16),
                        pltpu.VMEM((tm + 2 * HALO, w), F32)],
    )
    return pl.pallas_call(
        _conv_kernel,
        grid_spec=grid_spec,
        out_shape=jax.ShapeDtypeStruct((t, d), F32),
        compiler_params=_params("parallel"),
        name="shortconv_mixer",
    )(tile_start, tile_end, x, x, x, mod, g.reshape(1, d), w_in.astype(BF16),
      w_conv, w_out.astype(BF16))


QH = 3 * LANES
KH = 2 * LANES


def _mla_pre_kernel(pos_ref, x_ref, mod_ref, g_ref, wd_ref, gq_ref, gkv_ref, wuq_ref,
                    wukv_ref, gh_ref, cos_ref, sin_ref, q_ref, k_ref, v_ref):
    del pos_ref
    x = x_ref[...]
    h = _norm_mod(x, g_ref[...], mod_ref[0, 0:1, :], mod_ref[0, 1:2, :]).astype(BF16)
    lat = jnp.dot(h, wd_ref[...], preferred_element_type=F32)
    ql = lat[:, :MLA_Q_RANK]
    kvl = lat[:, MLA_Q_RANK:MLA_Q_RANK + MLA_KV_RANK]
    pe = lat[:, MLA_Q_RANK + MLA_KV_RANK:MLA_Q_RANK + MLA_KV_RANK + LANES]
    pe_sw = lat[:, MLA_Q_RANK + MLA_KV_RANK + LANES:]
    ql = ql * lax.rsqrt(jnp.mean(ql * ql, axis=-1, keepdims=True) + NORM_EPS) * gq_ref[...]
    kvl = kvl * lax.rsqrt(jnp.mean(kvl * kvl, axis=-1, keepdims=True) + NORM_EPS) * gkv_ref[...]
    q = jnp.dot(ql.astype(BF16), wuq_ref[...], preferred_element_type=F32)
    kv = jnp.dot(kvl.astype(BF16), wukv_ref[...], preferred_element_type=F32)
    cos = cos_ref[...]
    sin = sin_ref[...]
    gqn, gqr, gqs = gh_ref[0:1, :], gh_ref[1:2, :], gh_ref[2:3, :]
    gkn, gkr, gks = gh_ref[3:4, :], gh_ref[4:5, :], gh_ref[5:6, :]
    pe_ss = jnp.sum(pe * pe, axis=-1, keepdims=True)
    qscale = MLA_DK ** -0.5 * LOG2_E
    for hd in range(MLA_HEADS):
        qn = q[:, hd * QH:hd * QH + LANES]
        qr = q[:, hd * QH + LANES:hd * QH + 2 * LANES]
        qs = q[:, hd * QH + 2 * LANES:(hd + 1) * QH]
        ss = jnp.sum(qn * qn, axis=-1, keepdims=True) + jnp.sum(qr * qr, axis=-1, keepdims=True)
        rinv = lax.rsqrt(ss * (1.0 / MLA_DK) + NORM_EPS) * qscale
        q_ref[hd, :, 0:LANES] = (qn * rinv * gqn).astype(BF16)
        q_ref[hd, :, LANES:] = ((qr * rinv * gqr) * cos + (qs * rinv * gqs) * sin).astype(BF16)
        kn = kv[:, hd * 2 * LANES:hd * 2 * LANES + LANES]
        ss = jnp.sum(kn * kn, axis=-1, keepdims=True) + pe_ss
        rinv = lax.rsqrt(ss * (1.0 / MLA_DK) + NORM_EPS)
        k_ref[hd, :, 0:LANES] = (kn * rinv * gkn).astype(BF16)
        k_ref[hd, :, LANES:] = ((pe * rinv * gkr) * cos + (pe_sw * rinv * gks) * sin).astype(BF16)
        v_ref[hd] = kv[:, hd * 2 * LANES + LANES:(hd + 1) * 2 * LANES].T.astype(BF16)


def _mla_prepare_weights(w_down, w_uq, w_ukv, g_qh, g_kh):
    half = MLA_ROPE // 2
    pad = LANES - MLA_ROPE

    def swap(a):
        return jnp.concatenate([a[..., half:], a[..., :half]], axis=-1)

    def pad_lanes(a):
        return jnp.pad(a, [(0, 0)] * (a.ndim - 1) + [(0, pad)])

    d = w_down.shape[0]
    pe_w = w_down[:, MLA_Q_RANK + MLA_KV_RANK:]
    wd = jnp.concatenate([w_down[:, :MLA_Q_RANK + MLA_KV_RANK],
                          pad_lanes(pe_w), pad_lanes(swap(pe_w))], axis=1)
    uq = w_uq.reshape(MLA_Q_RANK, MLA_HEADS, MLA_DK)
    uq_r = uq[..., MLA_NOPE:]
    uq = jnp.concatenate([uq[..., :MLA_NOPE], pad_lanes(uq_r), pad_lanes(swap(uq_r))], axis=-1)
    uq = uq.reshape(MLA_Q_RANK, MLA_HEADS * QH)

    def gains(gv):
        r = gv[MLA_NOPE:]
        return [gv[:MLA_NOPE], pad_lanes(r), pad_lanes(swap(r))]

    gh = jnp.stack(gains(g_qh) + gains(g_kh) + [jnp.zeros((LANES,), F32)] * 2)
    return wd.astype(BF16), uq.astype(BF16), w_ukv.astype(BF16), gh


def _rope_tables(s):
    inv = 1.0 / (ROPE_THETA ** (jnp.arange(0, MLA_ROPE, 2, dtype=F32) / MLA_ROPE))
    ang = jnp.arange(s, dtype=F32)[:, None] * inv[None, :]
    cos, sin = jnp.cos(ang), jnp.sin(ang)
    zeros = jnp.zeros((s, LANES - MLA_ROPE), F32)
    return (jnp.concatenate([cos, cos, zeros], axis=1),
            jnp.concatenate([-sin, sin, zeros], axis=1))


def _mla_pre_call(x, mod, g, wd, g_q_lat, g_kv_lat, uq, ukv, gh, cos, sin, seg, tile_pos):
    t, d = x.shape
    tm = TM_MLA
    grid_spec = pltpu.PrefetchScalarGridSpec(
        num_scalar_prefetch=1,
        grid=(t // tm,),
        in_specs=[
            pl.BlockSpec((tm, d), lambda i, p: (i, 0)),
            pl.BlockSpec((1, 6, d), lambda i, p: ((i * tm) // seg, 0, 0)),
            _const_spec((1, d)),
            _const_spec(wd.shape),
            _const_spec((1, MLA_Q_RANK)),
            _const_spec((1, MLA_KV_RANK)),
            _const_spec(uq.shape),
            _const_spec(ukv.shape),
            _const_spec(gh.shape),
            pl.BlockSpec((tm, LANES), lambda i, p: (p[i], 0)),
            pl.BlockSpec((tm, LANES), lambda i, p: (p[i], 0)),
        ],
        out_specs=[
            pl.BlockSpec((MLA_HEADS, tm, KH), lambda i, p: (0, i, 0)),
            pl.BlockSpec((MLA_HEADS, tm, KH), lambda i, p: (0, i, 0)),
            pl.BlockSpec((MLA_HEADS, MLA_V, tm), lambda i, p: (0, 0, i)),
        ],
    )
    return pl.pallas_call(
        _mla_pre_kernel,
        grid_spec=grid_spec,
        out_shape=[jax.ShapeDtypeStruct((MLA_HEADS, t, KH), BF16),
                   jax.ShapeDtypeStruct((MLA_HEADS, t, KH), BF16),
                   jax.ShapeDtypeStruct((MLA_HEADS, MLA_V, t), BF16)],
        compiler_params=_params("parallel"),
        name="mla_qkv",
    )(tile_pos, x, mod, g.reshape(1, d), wd, g_q_lat.reshape(1, -1), g_kv_lat.reshape(1, -1),
      uq, ukv, gh, cos, sin)


def _attn_kernel(q_ref, k_ref, vt_ref, o_ref, s_buf, p_buf):
    tk = TK_ATTN
    n = k_ref.shape[1] // tk
    nt = (((1,), (1,)), ((), ()))
    tq = q_ref.shape[1] // ATTN_SUBTILES
    subs = range(ATTN_SUBTILES)
    qs = [q_ref[0, sub * tq:(sub + 1) * tq, :] for sub in subs]

    def scores(sub, j):
        return lax.dot_general(k_ref[0, j * tk:(j + 1) * tk, :], qs[sub], nt,
                               preferred_element_type=F32)

    def values(sub, j):
        return jnp.dot(vt_ref[0, :, j * tk:(j + 1) * tk], p_buf[sub, j % 2],
                       preferred_element_type=F32)

    m = [jnp.full((1, tq), -jnp.inf, F32) for _ in subs]
    l = [jnp.zeros((1, tq), F32) for _ in subs]
    acc = [jnp.zeros((MLA_V, tq), F32) for _ in subs]
    alpha = [None for _ in subs]
    for sub in subs:
        s_buf[sub, 0] = scores(sub, 0)
    for j in range(n):
        cur = j % 2
        for sub in subs:
            if j + 1 < n:
                s_buf[sub, 1 - cur] = scores(sub, j + 1)
            if j > 0:
                acc[sub] = alpha[sub] * acc[sub] + values(sub, j - 1)
            s = s_buf[sub, cur]
            m_new = jnp.maximum(m[sub], jnp.max(s, axis=0, keepdims=True))
            alpha[sub] = jnp.exp2(m[sub] - m_new)
            p = jnp.exp2(s - m_new)
            l[sub] = alpha[sub] * l[sub] + jnp.sum(p, axis=0, keepdims=True)
            m[sub] = m_new
            p_buf[sub, cur] = p.astype(BF16)
    for sub in subs:
        a = alpha[sub] * acc[sub] + values(sub, n - 1)
        o_ref[sub * tq:(sub + 1) * tq, :] = (a / l[sub]).T.astype(BF16)


def _attn_call(q, k, vt, tok0, n_seq, s_len):
    tq, tk = TQ_ATTN * ATTN_SUBTILES, TK_ATTN
    assert s_len % tk == 0
    nq = s_len // tq
    qb0 = tok0 // tq
    sb0 = tok0 // s_len
    return pl.pallas_call(
        _attn_kernel,
        grid=(n_seq, MLA_HEADS, nq),
        in_specs=[
            pl.BlockSpec((1, tq, KH), lambda s, h, i: (h, qb0 + s * nq + i, 0)),
            pl.BlockSpec((1, s_len, KH), lambda s, h, i: (h, sb0 + s, 0)),
            pl.BlockSpec((1, MLA_V, s_len), lambda s, h, i: (h, 0, sb0 + s)),
        ],
        out_specs=pl.BlockSpec((tq, MLA_V), lambda s, h, i: (s * nq + i, h)),
        out_shape=jax.ShapeDtypeStruct((n_seq * s_len, MLA_HEADS * MLA_V), BF16),
        scratch_shapes=[pltpu.VMEM((ATTN_SUBTILES, 2, tk, TQ_ATTN), F32),
                        pltpu.VMEM((ATTN_SUBTILES, 2, tk, TQ_ATTN), BF16)],
        compiler_params=_params("parallel", "parallel", "arbitrary"),
        name="mla_attention",
    )(q, k, vt)


def _proj_res_kernel(x_ref, oa_ref, ob_ref, mod_ref, w_ref, out_ref, *, n_first):
    o = jnp.where(pl.program_id(0) < n_first, oa_ref[...], ob_ref[...])
    out = jnp.dot(o, w_ref[...], preferred_element_type=F32)
    out_ref[...] = x_ref[...] + mod_ref[0, 2:3, :] * out


def _proj_res_call(x, oa, ob, mod, w_o, seg):
    t, d = x.shape
    tm = TM_OUT
    na, nb = oa.shape[0] // tm, ob.shape[0] // tm
    assert (na + nb) * tm == t
    return pl.pallas_call(
        functools.partial(_proj_res_kernel, n_first=na),
        grid=(t // tm,),
        in_specs=[
            pl.BlockSpec((tm, d), lambda i: (i, 0)),
            pl.BlockSpec((tm, oa.shape[1]), lambda i: (jnp.minimum(i, na - 1), 0)),
            pl.BlockSpec((tm, ob.shape[1]), lambda i: (jnp.clip(i - na, 0, nb - 1), 0)),
            pl.BlockSpec((1, 6, d), lambda i: ((i * tm) // seg, 0, 0)),
            _const_spec(w_o.shape),
        ],
        out_specs=pl.BlockSpec((tm, d), lambda i: (i, 0)),
        out_shape=jax.ShapeDtypeStruct((t, d), F32),
        compiler_params=_params("parallel"),
        name="mla_out_proj",
    )(x, oa, ob, mod, w_o.astype(BF16))


def _router_kernel(x_ref, mod_ref, g_ref, wr_hi_ref, wr_lo_ref, br_ref,
                   h_ref, ids_ref, rank_ref, gcol_ref, cnt_ref, carry_ref):
    i = pl.program_id(0)

    @pl.when(i == 0)
    def _():
        carry_ref[...] = jnp.zeros_like(carry_ref)

    x = x_ref[...]
    tm = x.shape[0]
    h = _norm_mod(x, g_ref[...], mod_ref[0, 3:4, :], mod_ref[0, 4:5, :])
    h_hi = h.astype(BF16)
    h_hi32 = h_hi.astype(F32)
    h_ref[...] = _pack_rounded(h_hi32)
    h_lo = (h - h_hi32).astype(BF16)
    nt = (((1,), (1,)), ((), ()))
    logits = (lax.dot_general(wr_hi_ref[...], h_hi, nt, preferred_element_type=F32)
              + lax.dot_general(wr_hi_ref[...], h_lo, nt, preferred_element_type=F32)
              + lax.dot_general(wr_lo_ref[...], h_hi, nt, preferred_element_type=F32)
              + br_ref[...])
    eidx = lax.broadcasted_iota(jnp.int32, logits.shape, 0)
    lg = logits
    sel = jnp.zeros(logits.shape, F32)
    vals, ids = [], []
    for _ in range(TOP_K):
        m = jnp.max(lg, axis=0, keepdims=True)
        idx = jnp.min(jnp.where(lg == m, eidx, N_EXPERTS), axis=0, keepdims=True)
        onehot = eidx == idx
        vals.append(m)
        ids.append(idx)
        lg = jnp.where(onehot, -jnp.inf, lg)
        sel = jnp.where(onehot, 1.0, sel)
    ex = [jnp.exp(v - vals[0]) for v in vals]
    den = ex[0] + ex[1] + ex[2] + ex[3]
    gates = [e / den for e in ex]
    r_io = lax.broadcasted_iota(jnp.int32, (tm, tm), 0)
    c_io = lax.broadcasted_iota(jnp.int32, (tm, tm), 1)
    before = jnp.where(r_io < c_io, 1.0, 0.0).astype(BF16)
    cum = jnp.dot(sel.astype(BF16), before, preferred_element_type=F32) + carry_ref[:, 0:1]
    for k in range(TOP_K):
        rk = jnp.sum(jnp.where(eidx == ids[k], cum, 0.0), axis=0, keepdims=True)
        rank_ref[k:k + 1, :] = rk.astype(jnp.int32)
        ids_ref[k:k + 1, :] = ids[k]
    g8 = jnp.concatenate(gates + [jnp.zeros((LANES - TOP_K, tm), F32)], axis=0)
    gcol_ref[...] = g8.T
    carry_ref[...] = carry_ref[...] + jnp.sum(sel, axis=1, keepdims=True)
    cnt_ref[...] = carry_ref[...]


def _router_call(x, mod, g, w_r, b_r, seg):
    t, d = x.shape
    tm = TM_ROUTER
    e = w_r.shape[1]
    wr_hi = w_r.T.astype(BF16)
    wr_lo = (w_r.T - wr_hi.astype(F32)).astype(BF16)
    return pl.pallas_call(
        _router_kernel,
        grid=(t // tm,),
        in_specs=[
            pl.BlockSpec((tm, d), lambda i: (i, 0)),
            pl.BlockSpec((1, 6, d), lambda i: ((i * tm) // seg, 0, 0)),
            _const_spec((1, d)),
            _const_spec((e, d)),
            _const_spec((e, d)),
            _const_spec((e, 1)),
        ],
        out_specs=[
            pl.BlockSpec((tm, d // 2), lambda i: (i, 0)),
            pl.BlockSpec((TOP_K, tm), lambda i: (0, i)),
            pl.BlockSpec((TOP_K, tm), lambda i: (0, i)),
            pl.BlockSpec((tm, LANES), lambda i: (i, 0)),
            _const_spec((e, LANES)),
        ],
        out_shape=[
            jax.ShapeDtypeStruct((t, d // 2), jnp.int32),
            jax.ShapeDtypeStruct((TOP_K, t), jnp.int32),
            jax.ShapeDtypeStruct((TOP_K, t), jnp.int32),
            jax.ShapeDtypeStruct((t, LANES), F32),
            jax.ShapeDtypeStruct((e, LANES), F32),
        ],
        scratch_shapes=[pltpu.VMEM((e, LANES), F32)],
        compiler_params=_params("arbitrary"),
        name="moe_router",
    )(x, mod, g.reshape(1, d), wr_hi, wr_lo, b_r.reshape(e, 1))


SC_CORES = 2
SC_SUBCORES = 16
SC_WORKERS = SC_CORES * SC_SUBCORES
SC_CHUNK = 64


def _dispatch_rows(h, dest, pad_idx, rows):
    t, d = h.shape
    top_k = dest.shape[0]
    per_w = t // SC_WORKERS
    n_ch = per_w // SC_CHUNK
    pad_ch = pad_idx.shape[0] // (SC_WORKERS * SC_CHUNK)
    assert n_ch * SC_CHUNK * SC_WORKERS == t
    assert pad_ch * SC_CHUNK * SC_WORKERS == pad_idx.shape[0]
    assert top_k * t + pad_idx.shape[0] == rows
    mesh = plsc.VectorSubcoreMesh(core_axis_name="c", subcore_axis_name="s")

    @functools.partial(
        pl.kernel, mesh=mesh,
        out_type=jax.ShapeDtypeStruct((rows, d), h.dtype),
        scratch_types=[pltpu.VMEM((top_k * n_ch, SC_CHUNK), jnp.int32),
                       pltpu.VMEM((pad_ch, SC_CHUNK), jnp.int32),
                       pltpu.VMEM((SC_CHUNK, d), h.dtype),
                       pltpu.SemaphoreType.DMA],
        name="sc_row_dispatch",
    )
    def dispatch(h_hbm, dest_hbm, pad_hbm, zero_hbm, out_hbm, idx_v, pad_v, rows_v, sem):
        wid = lax.axis_index("s") * SC_CORES + lax.axis_index("c")
        base = wid * per_w
        pltpu.sync_copy(dest_hbm.at[wid], idx_v)
        pltpu.sync_copy(pad_hbm.at[wid], pad_v)

        @pl.loop(0, n_ch)
        def _(j):
            pltpu.sync_copy(h_hbm.at[pl.ds(base + j * SC_CHUNK, SC_CHUNK)], rows_v)
            copies = [pltpu.async_copy(rows_v, out_hbm.at[idx_v.at[k * n_ch + j]], sem)
                      for k in range(top_k)]
            for c in copies:
                c.wait()

        pltpu.sync_copy(zero_hbm, rows_v)

        @pl.loop(0, pad_ch)
        def _(p):
            pltpu.sync_copy(rows_v, out_hbm.at[pad_v.at[p]])

    dest_w = dest.reshape(top_k, SC_WORKERS, n_ch, SC_CHUNK).transpose(1, 0, 2, 3)
    return dispatch(h, dest_w.reshape(SC_WORKERS, top_k * n_ch, SC_CHUNK),
                    pad_idx.reshape(SC_WORKERS, pad_ch, SC_CHUNK),
                    jnp.zeros((SC_CHUNK, d), h.dtype))


def _gather_rows(src, idx):
    n = idx.shape[0]
    d = src.shape[1]
    per_w = n // SC_WORKERS
    n_ch = per_w // SC_CHUNK
    assert per_w * SC_WORKERS == n and n_ch * SC_CHUNK == per_w
    mesh = plsc.VectorSubcoreMesh(core_axis_name="c", subcore_axis_name="s")

    @functools.partial(
        pl.kernel, mesh=mesh,
        out_type=jax.ShapeDtypeStruct((n, d), src.dtype),
        scratch_types=[pltpu.VMEM((n_ch, SC_CHUNK), jnp.int32),
                       pltpu.VMEM((SC_CHUNK, d), src.dtype),
                       pltpu.SemaphoreType.DMA],
        name="sc_row_gather",
    )
    def gather(src_hbm, idx_hbm, out_hbm, idx_v, rows_v, sem):
        wid = lax.axis_index("s") * SC_CORES + lax.axis_index("c")
        base = wid * per_w
        pltpu.sync_copy(idx_hbm.at[wid], idx_v)

        @pl.loop(0, n_ch)
        def _(j):
            pltpu.async_copy(src_hbm.at[idx_v.at[j]], rows_v, sem).wait()
            pltpu.sync_copy(rows_v, out_hbm.at[pl.ds(base + j * SC_CHUNK, SC_CHUNK)])

    return gather(src, idx.reshape(SC_WORKERS, n_ch, SC_CHUNK))


def _expert_kernel(blk_e_ref, nvalid_ref, x_ref, wgu_ref, bgu_ref, wdn_ref, bdn_ref,
                   o_ref, wgu_bf, wdn_bf):
    i = pl.program_id(0)
    e = blk_e_ref[i]
    e_prev = blk_e_ref[jnp.maximum(i - 1, 0)]
    active = nvalid_ref[i] > 0
    d_e = wdn_ref.shape[2]

    @pl.when(active & ((i == 0) | (e != e_prev)))
    def _():
        for r in range(0, wgu_ref.shape[2], LANES):
            wgu_bf[r:r + LANES, :] = wgu_ref[0, 0, r:r + LANES, :].astype(BF16)
        for r in range(0, d_e, LANES):
            wdn_bf[r:r + LANES, :] = wdn_ref[0, 0, r:r + LANES, :].astype(BF16)

    @pl.when(active)
    def _():
        lo, hi = _unpack_rows(x_ref[...])
        xb = jnp.concatenate([lo.astype(BF16), hi.astype(BF16)], axis=1)
        gu = jnp.dot(xb, wgu_bf[...], preferred_element_type=F32) + bgu_ref[0, 0]
        gt = jnp.minimum(gu[:, :d_e], SWIGLU_LIMIT)
        up = jnp.clip(gu[:, d_e:], -SWIGLU_LIMIT, SWIGLU_LIMIT)
        a = (up + 1.0) * (gt * jax.nn.sigmoid(SWIGLU_ALPHA * gt))
        o_ref[...] = _pack_rows(jnp.dot(a.astype(BF16), wdn_bf[...],
                                        preferred_element_type=F32) + bdn_ref[0, 0])

    @pl.when(jnp.logical_not(active))
    def _():
        o_ref[...] = jnp.zeros_like(o_ref)


def _expert_call(xg, blk_e, n_valid, w_gu, b_gu, w_dn, b_dn, layer):
    rows, half = xg.shape
    d = 2 * half
    nb = rows // MOE_ROWS
    _, n_e, _, two_de = w_gu.shape
    d_e = two_de // 2
    grid_spec = pltpu.PrefetchScalarGridSpec(
        num_scalar_prefetch=2,
        grid=(nb,),
        in_specs=[
            pl.BlockSpec((MOE_ROWS, half), lambda i, be, nu: (i, 0)),
            pl.BlockSpec((1, 1, d, two_de), lambda i, be, nu: (layer, be[i], 0, 0)),
            pl.BlockSpec((1, 1, 1, two_de), lambda i, be, nu: (layer, be[i], 0, 0)),
            pl.BlockSpec((1, 1, d_e, d), lambda i, be, nu: (layer, be[i], 0, 0)),
            pl.BlockSpec((1, 1, 1, d), lambda i, be, nu: (layer, be[i], 0, 0)),
        ],
        out_specs=pl.BlockSpec((MOE_ROWS, half), lambda i, be, nu: (i, 0)),
        scratch_shapes=[pltpu.VMEM((d, two_de), BF16), pltpu.VMEM((d_e, d), BF16)],
    )
    return pl.pallas_call(
        _expert_kernel,
        grid_spec=grid_spec,
        out_shape=jax.ShapeDtypeStruct((rows, half), jnp.int32),
        compiler_params=_params("arbitrary"),
        name="moe_experts",
    )(blk_e, n_valid, xg, w_gu, b_gu.reshape(b_gu.shape[0], n_e, 1, two_de),
      w_dn, b_dn.reshape(b_dn.shape[0], n_e, 1, d))


def _combine_kernel(x_ref, y_ref, gcol_ref, mod_ref, o_ref):
    gc = gcol_ref[...]
    half = y_ref.shape[2]
    acc_lo = acc_hi = None
    for k in range(TOP_K):
        lo, hi = _unpack_rows(y_ref[k])
        gate = gc[:, k:k + 1]
        acc_lo = gate * lo if k == 0 else acc_lo + gate * lo
        acc_hi = gate * hi if k == 0 else acc_hi + gate * hi
    o_ref[:, :half] = x_ref[:, :half] + mod_ref[0, 5:6, :half] * acc_lo
    o_ref[:, half:] = x_ref[:, half:] + mod_ref[0, 5:6, half:] * acc_hi


def _combine_call(x, yk, gcol, mod, seg, tok0, in_place):
    t, d = x.shape
    tm = TM_OUT
    tc = yk.shape[1]
    nb = tc // tm
    b0 = tok0 // tm
    ob = b0 if in_place else 0
    return pl.pallas_call(
        _combine_kernel,
        grid=(nb,),
        in_specs=[
            pl.BlockSpec((tm, d), lambda i: (b0 + i, 0)),
            pl.BlockSpec((TOP_K, tm, d // 2), lambda i: (0, i, 0)),
            pl.BlockSpec((tm, LANES), lambda i: (b0 + i, 0)),
            pl.BlockSpec((1, 6, d), lambda i: (((b0 + i) * tm) // seg, 0, 0)),
        ],
        out_specs=pl.BlockSpec((tm, d), lambda i: (ob + i, 0)),
        out_shape=jax.ShapeDtypeStruct((t if in_place else tc, d), F32),
        input_output_aliases={0: 0} if in_place else {},
        compiler_params=_params("parallel"),
        name="moe_combine",
    )(x, yk, gcol, mod)


def _moe_layer(x, mod, g, w_r, b_r, w_gu, b_gu, w_dn, b_dn, layer, seg, out_split=None):
    t, d = x.shape
    h, ids, rank, gcol, cnt = _router_call(x, mod, g, w_r, b_r, seg)
    counts = cnt[:, 0].astype(jnp.int32)
    padded = (counts + MOE_ROWS - 1) // MOE_ROWS * MOE_ROWS
    pad_end = jnp.cumsum(padded)
    pad_start = pad_end - padded
    e_ar = jnp.arange(N_EXPERTS, dtype=jnp.int32)
    dest = rank + jnp.sum(jnp.where(ids[..., None] == e_ar, pad_start, 0), axis=-1)
    nb = -(-(t * TOP_K) // MOE_ROWS) + N_EXPERTS
    rows = nb * MOE_ROWS
    slack = padded - counts
    slack_end = jnp.cumsum(slack)
    r = jnp.arange(rows - t * TOP_K, dtype=jnp.int32)
    owner = jnp.sum((slack_end[None, :] <= r[:, None]).astype(jnp.int32), axis=1)
    first = pad_start + counts - (slack_end - slack)
    in_expert = r + jnp.sum(jnp.where(owner[:, None] == e_ar, first, 0), axis=-1)
    pad_idx = jnp.where(owner < N_EXPERTS, in_expert, pad_end[-1] + r - slack_end[-1])
    blk_row = jnp.arange(nb, dtype=jnp.int32) * MOE_ROWS
    blk_e = jnp.minimum(jnp.sum((pad_end[None, :] <= blk_row[:, None]).astype(jnp.int32), axis=1),
                        N_EXPERTS - 1)
    tok_end = jnp.sum(jnp.where(blk_e[:, None] == e_ar, pad_start + counts, 0), axis=-1)
    n_valid = jnp.where(blk_row < pad_end[-1], jnp.clip(tok_end - blk_row, 0, MOE_ROWS), 0)
    xg = _dispatch_rows(h, dest, pad_idx, rows)
    yb = _expert_call(xg, blk_e, n_valid, w_gu, b_gu, w_dn, b_dn, layer)
    if out_split is None:
        bounds = [c * (t // COMBINE_CHUNKS) for c in range(COMBINE_CHUNKS + 1)]
    else:
        bounds = [0, out_split, t]
    outs = []
    for lo, hi in zip(bounds[:-1], bounds[1:]):
        yk = _gather_rows(yb, dest[:, lo:hi].reshape(-1)).reshape(TOP_K, hi - lo, d // 2)
        if out_split is None:
            x = _combine_call(x, yk, gcol, mod, seg, lo, True)
        else:
            outs.append(_combine_call(x, yk, gcol, mod, seg, lo, False))
    return x if out_split is None else tuple(outs)


def kernel(x_prompt, x_sample, c_prompt, c_sample, g_mix, g_ffn, w_ada, b_ada, gm_w_in, gm_g_v, gm_w_s, gm_b_s, gm_w_out, sc_w_in, sc_w_conv, sc_w_out, mla_w_down, mla_g_q_lat, mla_g_kv_lat, mla_w_uq, mla_w_ukv, mla_g_qh, mla_g_kh, mla_w_o, moe_w_router, moe_b_router, moe_w_gu, moe_b_gu, moe_w_dn, moe_b_dn):
    bp, sp, d = x_prompt.shape
    bs, ss, _ = x_sample.shape
    tp, ts = bp * sp, bs * ss
    t = tp + ts
    depth = g_mix.shape[0]
    seg = min(sp, ss)
    assert sp % seg == 0 and ss % seg == 0

    x = (x_prompt.reshape(tp, d), x_sample.reshape(ts, d))

    n_seq = bp + bs
    c_all = jnp.concatenate([c_prompt, c_sample], axis=0)
    c_pad = jnp.pad(c_all, ((0, (-n_seq) % 8), (0, 0)))
    mod_all = _ada_call(c_pad, w_ada, b_ada)[:, :n_seq].reshape(depth, n_seq, 6, d)
    seg_seq = jnp.concatenate([jnp.repeat(jnp.arange(bp), sp // seg),
                               bp + jnp.repeat(jnp.arange(bs), ss // seg)])
    mod_seg = mod_all[:, seg_seq]

    def tile_meta(tm):
        t0 = jnp.arange(t // tm, dtype=jnp.int32) * tm
        pos = jnp.where(t0 < tp, t0 % sp, (t0 - tp) % ss)
        slen = jnp.where(t0 < tp, sp, ss)
        return pos, slen

    for i in range(depth):
        mod = mod_seg[i]
        kind, j = i % 3, i // 3
        if kind == 0:
            x = _gmlp_call(x, mod, g_mix[i], gm_w_in[j], gm_g_v[j], gm_w_s[j], gm_b_s[j],
                           gm_w_out[j], seg)
        elif kind == 1:
            pos, slen = tile_meta(TM_CONV)
            x = _conv_call(x, mod, g_mix[i], sc_w_in[j], sc_w_conv[j], sc_w_out[j], seg,
                           (pos != 0).astype(jnp.int32),
                           (pos + TM_CONV != slen).astype(jnp.int32))
        else:
            wd, uq, ukv, gh = _mla_prepare_weights(mla_w_down[j], mla_w_uq[j], mla_w_ukv[j],
                                                  mla_g_qh[j], mla_g_kh[j])
            cos, sin = _rope_tables(max(sp, ss))
            pos, _ = tile_meta(TM_MLA)
            q, k, vt = _mla_pre_call(x, mod, g_mix[i], wd, mla_g_q_lat[j], mla_g_kv_lat[j],
                                     uq, ukv, gh, cos, sin, seg, pos // TM_MLA)
            x = _proj_res_call(x, _attn_call(q, k, vt, 0, bp, sp),
                               _attn_call(q, k, vt, tp, bs, ss), mod, mla_w_o[j], seg)
        x = _moe_layer(x, mod, g_ffn[i], moe_w_router[i], moe_b_router[i],
                       moe_w_gu, moe_b_gu, moe_w_dn, moe_b_dn, i, seg,
                       out_split=tp if i == depth - 1 else None)

    return (x[0].reshape(bp, sp, d), x[1].reshape(bs, ss, d))
```
